```python
import math
import jax
import jax.numpy as jnp
from jax import lax
import numpy as np

D_MODEL = 1024
BATCH = 2
SEQ = 8192
DEPTH = 2

CONV_WIDTH = 4
ML_HEADS = 8
ML_DQK = 64
ML_DV = 128
ML_CHUNK = 64
ML_W = ML_HEADS * ML_DV
ML_QK_W = ML_HEADS * ML_DQK
SSM_HEADS = 16
SSM_HEADDIM = 64
SSM_GROUPS = 2
SSM_STATE = 128
SSM_CHUNK = 128
SSM_W = SSM_HEADS * SSM_HEADDIM
SSM_BC_W = SSM_GROUPS * SSM_STATE
FOX_HEADS = 8
FOX_HEADDIM = 128
FOX_QBLOCK = 128
FOX_W = FOX_HEADS * FOX_HEADDIM
PEER_HEADS = 8
PEER_NKEYS = 128
PEER_NEXPERTS = PEER_NKEYS * PEER_NKEYS
PEER_QDIM = 256
PEER_TOPK = 16
PEER_TOKEN_BLOCK = 128
PLE_DIM = 256
EPS = 1e-6

IN_SPLITS = (ML_QK_W, ML_QK_W, ML_W, ML_W, ML_HEADS, ML_HEADS,
             SSM_W, SSM_W, SSM_BC_W, SSM_BC_W, SSM_HEADS,
             FOX_W, FOX_W, FOX_W, FOX_HEADS,
             D_MODEL, D_MODEL, D_MODEL)
D_IN = sum(IN_SPLITS)

kernel_name = 'hybrid_mlstm_ssd_fox_peer'


def _split(a, sizes):
    offs = [int(o) for o in np.cumsum(sizes)[:-1]]
    return jnp.split(a, offs, axis=-1)


def rmsnorm(x, g):
    xf = x.astype(jnp.float32)
    y = xf * lax.rsqrt(jnp.mean(xf * xf, axis=-1, keepdims=True) + EPS)
    return (y * g.astype(jnp.float32)).astype(x.dtype)


def causal_conv(x, w, b):
    y = lax.conv_general_dilated(x, w[:, None, :].astype(x.dtype), window_strides=(1,),
                                 padding=[(CONV_WIDTH - 1, 0)],
                                 dimension_numbers=('NWC', 'WIO', 'NWC'),
                                 feature_group_count=x.shape[-1])
    return y + b.astype(x.dtype)


def to_chunks(a, size):
    bsz, s = a.shape[0], a.shape[1]
    return jnp.moveaxis(a.reshape((bsz, s // size, size) + a.shape[2:]), 1, 0)


def from_chunks(a):
    a = jnp.moveaxis(a, 0, 1)
    return a.reshape((a.shape[0], a.shape[1] * a.shape[2]) + a.shape[3:])


def mlstm(q, k, v, i_pre, f_pre):
    f32 = jnp.float32
    bsz, _, nh, dk = q.shape
    dv = v.shape[-1]
    L = ML_CHUNK
    causal = jnp.tril(jnp.ones((L, L), dtype=bool))
    qs = to_chunks(q.astype(f32) * (dk ** -0.5), L)
    ks = to_chunks(k.astype(f32), L)
    vs = to_chunks(v.astype(f32), L)
    is_ = to_chunks(i_pre, L)
    lfs = to_chunks(jax.nn.log_sigmoid(f_pre), L)

    def step(carry, inp):
        cmem, nvec, m = carry
        qb, kb, vb, ib, lfb = inp
        ib = jnp.swapaxes(ib, 1, 2)
        b = jnp.cumsum(jnp.swapaxes(lfb, 1, 2), axis=-1)
        g = b[..., -1]
        dmat = jnp.where(causal, b[..., :, None] - b[..., None, :] + ib[..., None, :], -jnp.inf)
        inter = b + m[..., None]
        m_t = jnp.maximum(inter, jnp.max(dmat, axis=-1))
        sc = jnp.einsum('bthd,bshd->bhts', qb, kb) * jnp.exp(dmat - m_t[..., None])
        w_inter = jnp.swapaxes(jnp.exp(inter - m_t), 1, 2)
        num = (jnp.einsum('bhts,bshv->bthv', sc, vb)
               + w_inter[..., None] * jnp.einsum('bthd,bhdv->bthv', qb, cmem))
        den = jnp.swapaxes(jnp.sum(sc, axis=-1), 1, 2) + w_inter * jnp.einsum('bthd,bhd->bth', qb, nvec)
        floor = jnp.swapaxes(jnp.exp(-m_t), 1, 2)
        h = num / jnp.maximum(jnp.abs(den), floor)[..., None]
        a = g[..., None] - b + ib
        m_new = jnp.maximum(g + m, jnp.max(a, axis=-1))
        wa = jnp.exp(a - m_new[..., None])
        decay = jnp.exp(g + m - m_new)
        c_new = decay[..., None, None] * cmem + jnp.einsum('bhs,bshd,bshv->bhdv', wa, kb, vb)
        n_new = decay[..., None] * nvec + jnp.einsum('bhs,bshd->bhd', wa, kb)
        return (c_new, n_new, m_new), h

    carry0 = (jnp.zeros((bsz, nh, dk, dv), f32), jnp.zeros((bsz, nh, dk), f32), jnp.zeros((bsz, nh), f32))
    _, hs = lax.scan(step, carry0, (qs, ks, vs, is_, lfs))
    return from_chunks(hs)


def ssd(x, dt, a, bmat, cmat):
    f32 = jnp.float32
    bsz, _, ng, hg, pdim = x.shape
    nst = bmat.shape[-1]
    L = SSM_CHUNK
    causal = jnp.tril(jnp.ones((L, L), dtype=bool))
    xs = to_chunks(x.astype(f32) * dt[..., None], L)
    das = to_chunks(dt * a, L)
    bs = to_chunks(bmat.astype(f32), L)
    cs = to_chunks(cmat.astype(f32), L)

    def step(state, inp):
        xb, dab, bb, cb = inp
        acum = jnp.cumsum(jnp.transpose(dab, (0, 2, 3, 1)), axis=-1)
        lmat = jnp.exp(jnp.where(causal, acum[..., :, None] - acum[..., None, :], -jnp.inf))
        cbt = jnp.einsum('btgn,bsgn->bgts', cb, bb)
        y = jnp.einsum('bghts,bsghp->btghp', cbt[:, :, None] * lmat, xb)
        y = y + (jnp.einsum('btgn,bghnp->btghp', cb, state)
                 * jnp.transpose(jnp.exp(acum), (0, 3, 1, 2))[..., None])
        decay = jnp.exp(acum[..., -1:] - acum)
        new_state = (jnp.exp(acum[..., -1])[..., None, None] * state
                     + jnp.einsum('bsgn,bghs,bsghp->bghnp', bb, decay, xb))
        return new_state, y

    state0 = jnp.zeros((bsz, ng, hg, nst, pdim), f32)
    _, ys = lax.scan(step, state0, (xs, das, bs, cs))
    return from_chunks(ys)


def forgetting_attention(q, k, v, f_pre):
    f32 = jnp.float32
    bsz, s, nh, d = q.shape
    nb = s // FOX_QBLOCK
    fcum = jnp.cumsum(jax.nn.log_sigmoid(f_pre.astype(f32)), axis=1)
    fk = jnp.swapaxes(fcum, 1, 2)
    kf = k.astype(f32) * (d ** -0.5)
    vf = v.astype(f32)
    kpos = jnp.arange(s)

    def block(args):
        qb, fqb, start = args
        logits = (jnp.einsum('bthd,bshd->bhts', qb, kf)
                  + jnp.swapaxes(fqb, 1, 2)[..., None] - fk[:, :, None, :])
        qpos = start + jnp.arange(FOX_QBLOCK)
        logits = jnp.where(kpos[None, :] <= qpos[:, None], logits, -jnp.inf)
        probs = jax.nn.softmax(logits, axis=-1)
        return jnp.einsum('bhts,bshd->bthd', probs, vf)

    out = lax.map(block, (to_chunks(q.astype(f32), FOX_QBLOCK), to_chunks(fcum, FOX_QBLOCK),
                          jnp.arange(nb) * FOX_QBLOCK))
    return from_chunks(out)


def peer(h, w_q, keys1, keys2, u, v):
    f32 = jnp.float32
    bsz, s, d = h.shape
    t = bsz * s
    hf = h.reshape(t, d)
    q = (hf @ w_q).astype(f32).reshape(t, PEER_HEADS, 2, PEER_QDIM // 2)
    s1 = jnp.einsum('thd,hkd->thk', q[:, :, 0], keys1.astype(f32))
    s2 = jnp.einsum('thd,hkd->thk', q[:, :, 1], keys2.astype(f32))
    v1, i1 = lax.top_k(s1, PEER_TOPK)
    v2, i2 = lax.top_k(s2, PEER_TOPK)
    cand_s = (v1[..., :, None] + v2[..., None, :]).reshape(t, PEER_HEADS, PEER_TOPK * PEER_TOPK)
    cand_i = (i1[..., :, None] * PEER_NKEYS + i2[..., None, :]).reshape(t, PEER_HEADS, PEER_TOPK * PEER_TOPK)
    top_s, pos = lax.top_k(cand_s, PEER_TOPK)
    idx = jnp.take_along_axis(cand_i, pos, axis=-1)
    gates = jax.nn.softmax(top_s, axis=-1).astype(h.dtype)
    nblk = t // PEER_TOKEN_BLOCK

    def block(args):
        xb, ib, gb = args
        act = jax.nn.gelu(jnp.einsum('td,thkd->thk', xb, u[ib]), approximate=False)
        return jnp.einsum('thk,thkd->td', act * gb, v[ib])

    out = lax.map(block, (hf.reshape(nblk, PEER_TOKEN_BLOCK, d),
                          idx.reshape(nblk, PEER_TOKEN_BLOCK, PEER_HEADS, PEER_TOPK),
                          gates.reshape(nblk, PEER_TOKEN_BLOCK, PEER_HEADS, PEER_TOPK)))
    return out.reshape(bsz, s, d)


def hybrid_mixer(h, w_in, ml_conv_w, ml_conv_b, ml_b_i, ml_b_f, ml_norm_g,
                 ssm_conv_w, ssm_conv_b, ssm_dt_bias, ssm_a_log, ssm_d, ssm_norm_g,
                 fox_q_norm_g, fox_k_norm_g, fox_b_f,
                 w_branch_ml, w_branch_ssm, w_branch_fox, w_out):
    f32 = jnp.float32
    bsz, s, _ = h.shape
    (ml_q, ml_k, ml_v, ml_o, ml_i, ml_f, s_z, s_x, s_b, s_c, s_dt,
     f_q, f_k, f_v, f_f, g_ml, g_ssm, g_fox) = _split(h @ w_in, IN_SPLITS)

    qk = jax.nn.silu(causal_conv(jnp.concatenate([ml_q, ml_k], axis=-1), ml_conv_w, ml_conv_b))
    q, k = jnp.split(qk, 2, axis=-1)
    hm = mlstm(q.reshape(bsz, s, ML_HEADS, ML_DQK), k.reshape(bsz, s, ML_HEADS, ML_DQK),
               ml_v.reshape(bsz, s, ML_HEADS, ML_DV),
               ml_i.astype(f32) + ml_b_i.astype(f32), ml_f.astype(f32) + ml_b_f.astype(f32))
    hm = rmsnorm(hm, ml_norm_g.reshape(ML_HEADS, ML_DV)).reshape(bsz, s, ML_W)
    y_ml = (jax.nn.sigmoid(ml_o.astype(f32)) * hm).astype(h.dtype)

    hg = SSM_HEADS // SSM_GROUPS
    xbc = jax.nn.silu(causal_conv(jnp.concatenate([s_x, s_b, s_c], axis=-1), ssm_conv_w, ssm_conv_b))
    sx, sb, sc = _split(xbc, (SSM_W, SSM_BC_W, SSM_BC_W))
    sx = sx.reshape(bsz, s, SSM_GROUPS, hg, SSM_HEADDIM)
    dt = jax.nn.softplus(s_dt.astype(f32) + ssm_dt_bias.astype(f32)).reshape(bsz, s, SSM_GROUPS, hg)
    a = -jnp.exp(ssm_a_log.astype(f32)).reshape(SSM_GROUPS, hg)
    ys = ssd(sx, dt, a, sb.reshape(bsz, s, SSM_GROUPS, SSM_STATE), sc.reshape(bsz, s, SSM_GROUPS, SSM_STATE))
    ys = ys + ssm_d.astype(f32).reshape(SSM_GROUPS, hg)[..., None] * sx.astype(f32)
    ys = ys.reshape(bsz, s, SSM_W) * jax.nn.silu(s_z.astype(f32))
    y_ssm = rmsnorm(ys.reshape(bsz, s, SSM_GROUPS, SSM_W // SSM_GROUPS),
                    ssm_norm_g.reshape(SSM_GROUPS, SSM_W // SSM_GROUPS)).reshape(bsz, s, SSM_W).astype(h.dtype)

    fq = rmsnorm(f_q.reshape(bsz, s, FOX_HEADS, FOX_HEADDIM), fox_q_norm_g)
    fk = rmsnorm(f_k.reshape(bsz, s, FOX_HEADS, FOX_HEADDIM), fox_k_norm_g)
    y_fox = forgetting_attention(fq, fk, f_v.reshape(bsz, s, FOX_HEADS, FOX_HEADDIM),
                                 f_f.astype(f32) + fox_b_f.astype(f32))
    y_fox = y_fox.reshape(bsz, s, FOX_W).astype(h.dtype)

    merged = (jax.nn.sigmoid(g_ml) * (y_ml @ w_branch_ml)
              + jax.nn.sigmoid(g_ssm) * (y_ssm @ w_branch_ssm)
              + jax.nn.sigmoid(g_fox) * (y_fox @ w_branch_fox))
    return merged @ w_out


def setup_inputs(seed: int = 0) -> dict:
    key = jax.random.key(seed)
    ks = iter(jax.random.split(key, 48))
    f32 = jnp.float32

    def nrm(shape, scale):
        return jax.random.normal(next(ks), shape, f32) * scale

    def gain(shape):
        return 1.0 + nrm(shape, 0.02)

    dt0 = jnp.exp(jax.random.uniform(next(ks), (DEPTH, SSM_HEADS), f32, math.log(1e-3), math.log(1e-1)))
    return {
        'x': nrm((BATCH, SEQ, D_MODEL), 1.0),
        'p': nrm((DEPTH, BATCH, SEQ, PLE_DIM), 1.0),
        'norm_mix_g': gain((DEPTH, D_MODEL)),
        'w_in': nrm((DEPTH, D_MODEL, D_IN), D_MODEL ** -0.5),
        'ml_conv_w': nrm((DEPTH, CONV_WIDTH, 2 * ML_QK_W), CONV_WIDTH ** -0.5),
        'ml_conv_b': nrm((DEPTH, 2 * ML_QK_W), 0.02),
        'ml_b_i': nrm((DEPTH, ML_HEADS), 0.1),
        'ml_b_f': jnp.linspace(3.0, 6.0, ML_HEADS, dtype=f32)[None, :] + nrm((DEPTH, ML_HEADS), 0.1),
        'ml_norm_g': gain((DEPTH, ML_W)),
        'ssm_conv_w': nrm((DEPTH, CONV_WIDTH, SSM_W + 2 * SSM_BC_W), CONV_WIDTH ** -0.5),
        'ssm_conv_b': nrm((DEPTH, SSM_W + 2 * SSM_BC_W), 0.02),
        'ssm_dt_bias': dt0 + jnp.log(-jnp.expm1(-dt0)),
        'ssm_a_log': jnp.log(jax.random.uniform(next(ks), (DEPTH, SSM_HEADS), f32, 1.0, 16.0)),
        'ssm_d': gain((DEPTH, SSM_HEADS)),
        'ssm_norm_g': gain((DEPTH, SSM_W)),
        'fox_q_norm_g': gain((DEPTH, FOX_HEADDIM)),
        'fox_k_norm_g': gain((DEPTH, FOX_HEADDIM)),
        'fox_b_f': jnp.linspace(1.0, 5.0, FOX_HEADS, dtype=f32)[None, :] + nrm((DEPTH, FOX_HEADS), 0.1),
        'w_branch_ml': nrm((DEPTH, ML_W, D_MODEL), ML_W ** -0.5),
        'w_branch_ssm': nrm((DEPTH, SSM_W, D_MODEL), SSM_W ** -0.5),
        'w_branch_fox': nrm((DEPTH, FOX_W, D_MODEL), FOX_W ** -0.5),
        'w_out': nrm((DEPTH, D_MODEL, D_MODEL), D_MODEL ** -0.5),
        'norm_ffn_g': gain((DEPTH, D_MODEL)),
        'peer_w_q': nrm((DEPTH, D_MODEL, PEER_HEADS * PEER_QDIM), D_MODEL ** -0.5),
        'peer_keys1': nrm((DEPTH, PEER_HEADS, PEER_NKEYS, PEER_QDIM // 2), (PEER_QDIM // 2) ** -0.5),
        'peer_keys2': nrm((DEPTH, PEER_HEADS, PEER_NKEYS, PEER_QDIM // 2), (PEER_QDIM // 2) ** -0.5),
        'peer_u': nrm((DEPTH, PEER_NEXPERTS, D_MODEL), D_MODEL ** -0.5),
        'peer_v': nrm((DEPTH, PEER_NEXPERTS, D_MODEL), (PEER_HEADS * PEER_TOPK) ** -0.5),
        'norm_ple_g': gain((DEPTH, D_MODEL)),
        'ple_w_gate': nrm((DEPTH, D_MODEL, D_MODEL), D_MODEL ** -0.5),
        'ple_w_proj': nrm((DEPTH, PLE_DIM, D_MODEL), PLE_DIM ** -0.5),
        'final_norm_g': gain((D_MODEL,)),
    }


def reference(x, p, norm_mix_g, w_in, ml_conv_w, ml_conv_b, ml_b_i, ml_b_f, ml_norm_g,
              ssm_conv_w, ssm_conv_b, ssm_dt_bias, ssm_a_log, ssm_d, ssm_norm_g,
              fox_q_norm_g, fox_k_norm_g, fox_b_f, w_branch_ml, w_branch_ssm, w_branch_fox, w_out,
              norm_ffn_g, peer_w_q, peer_keys1, peer_keys2, peer_u, peer_v,
              norm_ple_g, ple_w_gate, ple_w_proj, final_norm_g):
    for i in range(DEPTH):
        h = rmsnorm(x, norm_mix_g[i])
        x = x + hybrid_mixer(h, w_in[i], ml_conv_w[i], ml_conv_b[i], ml_b_i[i], ml_b_f[i], ml_norm_g[i],
                             ssm_conv_w[i], ssm_conv_b[i], ssm_dt_bias[i], ssm_a_log[i], ssm_d[i], ssm_norm_g[i],
                             fox_q_norm_g[i], fox_k_norm_g[i], fox_b_f[i],
                             w_branch_ml[i], w_branch_ssm[i], w_branch_fox[i], w_out[i])
        x = x + peer(rmsnorm(x, norm_ffn_g[i]), peer_w_q[i], peer_keys1[i], peer_keys2[i], peer_u[i], peer_v[i])
        x = x + jax.nn.sigmoid(rmsnorm(x, norm_ple_g[i]) @ ple_w_gate[i]) * (p[i] @ ple_w_proj[i])
    return rmsnorm(x, final_norm_g)
```

```python
import functools
import math

import jax
import jax.numpy as jnp
from jax import lax
from jax.experimental import pallas as pl
from jax.experimental.pallas import tpu as pltpu

F32 = jnp.float32
BF16 = jnp.bfloat16
EPS = 1e-6

CONV_WIDTH = 4
ML_HEADS = 8
ML_DQK = 64
ML_DV = 128
SSM_HEADS = 16
SSM_HEADDIM = 64
SSM_GROUPS = 2
SSM_STATE = 128
FOX_HEADS = 8
FOX_HEADDIM = 128
PEER_HEADS = 8
PEER_NKEYS = 128
PEER_QDIM = 256
PEER_TOPK = 16

LANES = 128
HALO = 8

COL_ML_I = 0
COL_ML_F = 8
COL_DT = 16
COL_FOX_F = 32

VMEM_LIMIT = 56 * 1024 * 1024


def _cparams(sem):
    return pltpu.CompilerParams(dimension_semantics=sem, vmem_limit_bytes=VMEM_LIMIT)


def _log_sigmoid(v):
    return jnp.minimum(v, 0.0) - jnp.log1p(jnp.exp(-jnp.abs(v)))


def _sigmoid(v):
    return 1.0 / (1.0 + jnp.exp(-v))


def _silu(v):
    return v * _sigmoid(v)


def _softplus(v):
    return jnp.maximum(v, 0.0) + jnp.log1p(jnp.exp(-jnp.abs(v)))


def _tril(n):
    r = lax.broadcasted_iota(jnp.int32, (n, n), 0)
    c = lax.broadcasted_iota(jnp.int32, (n, n), 1)
    return r >= c


def _cumsum_rows(v):
    n = v.shape[0]
    return jnp.dot(_tril(n).astype(F32), v, precision=lax.Precision.HIGHEST,
                   preferred_element_type=F32)


def _inproj_kernel(x_ref, g_ref, wm_ref, ws_ref, zm_ref, zs_ref, h_scr):
    @pl.when(pl.program_id(1) == 0)
    def _():
        x = x_ref[...]
        ms = jnp.mean(x * x, axis=-1, keepdims=True)
        h = (x * lax.rsqrt(ms + EPS) * g_ref[...]).astype(BF16)
        h_scr[...] = h
        zs_ref[...] = jnp.dot(h, ws_ref[...], preferred_element_type=F32)

    zm_ref[...] = jnp.dot(h_scr[...], wm_ref[...], preferred_element_type=F32).astype(zm_ref.dtype)


def _inproj(x2, g, w_main, w_small, tm=1024, tn=512):
    t, d = x2.shape
    n = w_main.shape[1]
    return pl.pallas_call(
        _inproj_kernel,
        grid=(t // tm, n // tn),
        in_specs=[
            pl.BlockSpec((tm, d), lambda i, j: (i, 0)),
            pl.BlockSpec((1, d), lambda i, j: (0, 0)),
            pl.BlockSpec((d, tn), lambda i, j: (0, j)),
            pl.BlockSpec((d, LANES), lambda i, j: (0, 0)),
        ],
        out_specs=[
            pl.BlockSpec((tm, tn), lambda i, j: (i, j)),
            pl.BlockSpec((tm, LANES), lambda i, j: (i, 0)),
        ],
        out_shape=[jax.ShapeDtypeStruct((t, n), BF16), jax.ShapeDtypeStruct((t, LANES), F32)],
        scratch_shapes=[pltpu.VMEM((tm, d), BF16)],
        compiler_params=_cparams(("parallel", "arbitrary")),
        name="inproj",
    )(x2, g, w_main, w_small)


def _conv_chunk(buf_ref, w_ref, b_ref, start, length):
    acc = None
    for j in range(CONV_WIDTH):
        off = HALO - (CONV_WIDTH - 1) + j + start
        term = buf_ref[off:off + length, :] * w_ref[j:j + 1, :]
        acc = term if acc is None else acc + term
    return acc + b_ref[...]


def _mlstm_kernel(qk_ref, v_ref, o_ref, gs_ref, cw_ref, cb_ref, gb_ref, ng_ref, y_ref,
                  buf, c_scr, m_scr, *, ts, chunk):
    nh, dk, dv = ML_HEADS, ML_DQK, ML_DV
    qkw = nh * dk

    @pl.when(pl.program_id(1) == 0)
    def _():
        buf[0:HALO, :] = jnp.zeros((HALO, buf.shape[1]), F32)
        c_scr[...] = jnp.zeros(c_scr.shape, F32)
        m_scr[...] = jnp.zeros(m_scr.shape, F32)

    buf[HALO:HALO + ts, :] = qk_ref[...].astype(F32)
    causal = _tril(chunk)
    lane = lax.broadcasted_iota(jnp.int32, (chunk, LANES), 1)
    ones_col = (lane == 0).astype(BF16)

    for c in range(ts // chunk):
        r0 = c * chunk
        act = _silu(_conv_chunk(buf, cw_ref, cb_ref, r0, chunk))
        q_all = (act[:, :qkw] * (dk ** -0.5)).astype(BF16)
        kt_all = act[:, qkw:].T
        gates = gs_ref[r0:r0 + chunk, :] + gb_ref[...]
        bcum = _cumsum_rows(_log_sigmoid(gates))
        gates_t = gates.T
        bcum_t = bcum.T
        for h in range(nh):
            i_col = gates[:, COL_ML_I + h:COL_ML_I + h + 1]
            i_row = gates_t[COL_ML_I + h:COL_ML_I + h + 1, :]
            b_col = bcum[:, COL_ML_F + h:COL_ML_F + h + 1]
            b_row = bcum_t[COL_ML_F + h:COL_ML_F + h + 1, :]
            g_tot = b_col[chunk - 1:chunk, :]
            m_prev = m_scr[h:h + 1, 0:1]
            dmat = jnp.where(causal, b_col - b_row + i_row, -jnp.inf)
            inter = b_col + m_prev
            m_t = jnp.maximum(inter, jnp.max(dmat, axis=-1, keepdims=True))
            qh = q_all[:, h * dk:(h + 1) * dk]
            kt = kt_all[h * dk:(h + 1) * dk, :]
            s = jnp.dot(qh, kt.astype(BF16), preferred_element_type=F32)
            sc = (s * jnp.exp(dmat - m_t)).astype(BF16)
            w_inter = jnp.exp(inter - m_t)
            vaug = jnp.concatenate([v_ref[r0:r0 + chunk, h * dv:(h + 1) * dv], ones_col], axis=1)
            cmem = c_scr[h]
            tot = (jnp.dot(sc, vaug, preferred_element_type=F32)
                   + w_inter * jnp.dot(qh, cmem.astype(BF16), preferred_element_type=F32))
            num = tot[:, :dv]
            den = tot[:, dv:dv + 1]
            hv = num / jnp.maximum(jnp.abs(den), jnp.exp(-m_t))
            hn = hv * lax.rsqrt(jnp.mean(hv * hv, axis=-1, keepdims=True) + EPS)
            hn = hn * ng_ref[:, h * dv:(h + 1) * dv]
            og = _sigmoid(o_ref[r0:r0 + chunk, h * dv:(h + 1) * dv].astype(F32))
            y_ref[r0:r0 + chunk, h * dv:(h + 1) * dv] = (og * hn).astype(y_ref.dtype)
            a_row = g_tot - b_row + i_row
            m_new = jnp.maximum(g_tot + m_prev, jnp.max(a_row, axis=-1, keepdims=True))
            wa_row = jnp.exp(a_row - m_new)
            decay = jnp.exp(g_tot + m_prev - m_new)
            kw = (kt * wa_row).astype(BF16)
            c_scr[h] = decay * cmem + jnp.dot(kw, vaug, preferred_element_type=F32)
            m_scr[h:h + 1, :] = jnp.broadcast_to(m_new, (1, LANES))

    buf[0:HALO, :] = buf[ts:ts + HALO, :]


def _mlstm(zm, zs, conv_w, conv_b, gate_bias, norm_g, *, bsz, seq, ts=256, chunk=128):
    t = zm.shape[0]
    w = ML_HEADS * ML_DV
    nblk = seq // ts
    row = lambda b, s: b * nblk + s
    return pl.pallas_call(
        functools.partial(_mlstm_kernel, ts=ts, chunk=chunk),
        grid=(bsz, nblk),
        in_specs=[
            pl.BlockSpec((ts, w), lambda b, s: (row(b, s), 0)),
            pl.BlockSpec((ts, w), lambda b, s: (row(b, s), 1)),
            pl.BlockSpec((ts, w), lambda b, s: (row(b, s), 2)),
            pl.BlockSpec((ts, LANES), lambda b, s: (row(b, s), 0)),
            pl.BlockSpec(conv_w.shape, lambda b, s: (0, 0)),
            pl.BlockSpec(conv_b.shape, lambda b, s: (0, 0)),
            pl.BlockSpec(gate_bias.shape, lambda b, s: (0, 0)),
            pl.BlockSpec(norm_g.shape, lambda b, s: (0, 0)),
        ],
        out_specs=pl.BlockSpec((ts, w), lambda b, s: (row(b, s), 0)),
        out_shape=jax.ShapeDtypeStruct((t, w), BF16),
        scratch_shapes=[
            pltpu.VMEM((HALO + ts, w), F32),
            pltpu.VMEM((ML_HEADS, ML_DQK, 2 * ML_DV), F32),
            pltpu.VMEM((ML_HEADS, LANES), F32),
        ],
        compiler_params=_cparams(("parallel", "arbitrary")),
        name="mlstm",
    )(zm, zm, zm, zs, conv_w, conv_b, gate_bias, norm_g)


def _ssd_kernel(z_ref, x_ref, bc_ref, gs_ref, cwx_ref, cbx_ref, cwbc_ref, cbbc_ref, gb_ref,
                alog_ref, d_ref, ng_ref, y_ref, bufx, bufbc, st_scr, *, ts, chunk):
    ng, nst, p = SSM_GROUPS, SSM_STATE, SSM_HEADDIM
    hg = SSM_HEADS // ng
    gw = hg * p
    pairs_per_group = gw // LANES

    @pl.when(pl.program_id(1) == 0)
    def _():
        bufx[0:HALO, :] = jnp.zeros((HALO, bufx.shape[1]), F32)
        bufbc[0:HALO, :] = jnp.zeros((HALO, bufbc.shape[1]), F32)
        st_scr[...] = jnp.zeros(st_scr.shape, F32)

    bufx[HALO:HALO + ts, :] = x_ref[...].astype(F32)
    bufbc[HALO:HALO + ts, :] = bc_ref[...].astype(F32)
    causal = _tril(chunk)
    low_half = lax.broadcasted_iota(jnp.int32, (chunk, LANES), 1) < p
    a_row_all = -jnp.exp(alog_ref[...])

    for c in range(ts // chunk):
        r0 = c * chunk
        xa = _silu(_conv_chunk(bufx, cwx_ref, cbx_ref, r0, chunk))
        bca = _silu(_conv_chunk(bufbc, cwbc_ref, cbbc_ref, r0, chunk))
        dt = _softplus(gs_ref[r0:r0 + chunk, :] + gb_ref[...])
        acum = _cumsum_rows(dt * a_row_all)
        acum_t = acum.T
        y_parts = []
        for g in range(ng):
            bg = bca[:, g * nst:(g + 1) * nst]
            cg = bca[:, ng * nst + g * nst:ng * nst + (g + 1) * nst].astype(BF16)
            bg_t = bg.T
            cbt = jnp.dot(cg, bg_t.astype(BF16), preferred_element_type=F32)
            state = st_scr[g]
            inter = jnp.dot(cg, state.astype(BF16), preferred_element_type=F32)
            xs_scaled, last_parts = [], []
            for pp in range(pairs_per_group):
                pidx = g * pairs_per_group + pp
                ha = 2 * pidx
                ca, cb = COL_DT + ha, COL_DT + ha + 1
                sl = slice(pidx * LANES, (pidx + 1) * LANES)
                xa_p = xa[:, sl]
                dt_pair = jnp.where(low_half, dt[:, ca:ca + 1], dt[:, cb:cb + 1])
                ac_pair = jnp.where(low_half, acum[:, ca:ca + 1], acum[:, cb:cb + 1])
                xs_p = xa_p * dt_pair
                xs_b = xs_p.astype(BF16)
                lm_a = jnp.exp(jnp.where(causal, acum[:, ca:ca + 1] - acum_t[ca:ca + 1, :], -jnp.inf))
                lm_b = jnp.exp(jnp.where(causal, acum[:, cb:cb + 1] - acum_t[cb:cb + 1, :], -jnp.inf))
                ya = jnp.dot((cbt * lm_a).astype(BF16), xs_b, preferred_element_type=F32)
                yb = jnp.dot((cbt * lm_b).astype(BF16), xs_b, preferred_element_type=F32)
                y_p = (jnp.where(low_half, ya, yb)
                       + inter[:, pp * LANES:(pp + 1) * LANES] * jnp.exp(ac_pair)
                       + d_ref[:, sl] * xa_p)
                y_parts.append(y_p)
                last = ac_pair[chunk - 1:chunk, :]
                xs_scaled.append((xs_p * jnp.exp(last - ac_pair)).astype(BF16))
                last_parts.append(last)
            xs_g = jnp.concatenate(xs_scaled, axis=1)
            last_g = jnp.concatenate(last_parts, axis=1)
            st_scr[g] = jnp.exp(last_g) * state + jnp.dot(bg_t.astype(BF16), xs_g,
                                                          preferred_element_type=F32)
        ys = jnp.concatenate(y_parts, axis=1)
        ys = ys * _silu(z_ref[r0:r0 + chunk, :].astype(F32))
        outs = []
        for g in range(ng):
            yg = ys[:, g * gw:(g + 1) * gw]
            yn = yg * lax.rsqrt(jnp.mean(yg * yg, axis=-1, keepdims=True) + EPS)
            outs.append(yn * ng_ref[:, g * gw:(g + 1) * gw])
        y_ref[r0:r0 + chunk, :] = jnp.concatenate(outs, axis=1).astype(y_ref.dtype)

    bufx[0:HALO, :] = bufx[ts:ts + HALO, :]
    bufbc[0:HALO, :] = bufbc[ts:ts + HALO, :]


def _ssd(zm, zs, cwx, cbx, cwbc, cbbc, gate_bias, alog_row, d_row, norm_g, *, bsz, seq,
         col_z, col_x, col_bc, ts=256, chunk=128):
    t = zm.shape[0]
    w = SSM_HEADS * SSM_HEADDIM
    bcw = 2 * SSM_GROUPS * SSM_STATE
    nblk = seq // ts
    row = lambda b, s: b * nblk + s
    full = lambda a: pl.BlockSpec(a.shape, lambda b, s: (0, 0))
    return pl.pallas_call(
        functools.partial(_ssd_kernel, ts=ts, chunk=chunk),
        grid=(bsz, nblk),
        in_specs=[
            pl.BlockSpec((ts, w), lambda b, s: (row(b, s), col_z // w)),
            pl.BlockSpec((ts, w), lambda b, s: (row(b, s), col_x // w)),
            pl.BlockSpec((ts, bcw), lambda b, s: (row(b, s), col_bc // bcw)),
            pl.BlockSpec((ts, LANES), lambda b, s: (row(b, s), 0)),
            full(cwx), full(cbx), full(cwbc), full(cbbc), full(gate_bias),
            full(alog_row), full(d_row), full(norm_g),
        ],
        out_specs=pl.BlockSpec((ts, w), lambda b, s: (row(b, s), 0)),
        out_shape=jax.ShapeDtypeStruct((t, w), BF16),
        scratch_shapes=[
            pltpu.VMEM((HALO + ts, w), F32),
            pltpu.VMEM((HALO + ts, bcw), F32),
            pltpu.VMEM((SSM_GROUPS, SSM_STATE, w // SSM_GROUPS), F32),
        ],
        compiler_params=_cparams(("parallel", "arbitrary")),
        name="ssd",
    )(zm, zm, zm, zs, cwx, cbx, cwbc, cbbc, gate_bias, alog_row, d_row, norm_g)


def _fox_prep_kernel(q_ref, k_ref, gs_ref, gb_ref, gq_ref, gk_ref, qn_ref, kn_ref, ft_ref,
                     carry, *, ts):
    d = FOX_HEADDIM

    @pl.when(pl.program_id(1) == 0)
    def _():
        carry[...] = jnp.zeros(carry.shape, F32)

    for h in range(FOX_HEADS):
        sl = slice(h * d, (h + 1) * d)
        qh = q_ref[:, sl].astype(F32)
        qn = qh * lax.rsqrt(jnp.mean(qh * qh, axis=-1, keepdims=True) + EPS) * gq_ref[...]
        qn_ref[:, sl] = qn.astype(qn_ref.dtype)
        kh = k_ref[:, sl].astype(F32)
        kn = kh * lax.rsqrt(jnp.mean(kh * kh, axis=-1, keepdims=True) + EPS) * gk_ref[...]
        kn_ref[:, sl] = (kn * (d ** -0.5)).astype(kn_ref.dtype)

    lf = _log_sigmoid(gs_ref[...] + gb_ref[...])
    fc = _cumsum_rows(lf) + carry[0:1, :]
    carry[0:1, :] = fc[ts - 1:ts, :]
    ft_ref[...] = fc.T[COL_FOX_F:COL_FOX_F + FOX_HEADS, :]


def _fox_prep(zm, zs, gate_bias, gq, gk, *, bsz, seq, col_q, col_k, ts=256):
    t = zm.shape[0]
    w = FOX_HEADS * FOX_HEADDIM
    nblk = seq // ts
    row = lambda b, s: b * nblk + s
    full = lambda a: pl.BlockSpec(a.shape, lambda b, s: (0, 0))
    return pl.pallas_call(
        functools.partial(_fox_prep_kernel, ts=ts),
        grid=(bsz, nblk),
        in_specs=[
            pl.BlockSpec((ts, w), lambda b, s: (row(b, s), col_q // w)),
            pl.BlockSpec((ts, w), lambda b, s: (row(b, s), col_k // w)),
            pl.BlockSpec((ts, LANES), lambda b, s: (row(b, s), 0)),
            full(gate_bias), full(gq), full(gk),
        ],
        out_specs=[
            pl.BlockSpec((ts, w), lambda b, s: (row(b, s), 0)),
            pl.BlockSpec((ts, w), lambda b, s: (row(b, s), 0)),
            pl.BlockSpec((None, FOX_HEADS, ts), lambda b, s: (b, 0, s)),
        ],
        out_shape=[jax.ShapeDtypeStruct((t, w), BF16), jax.ShapeDtypeStruct((t, w), BF16),
                   jax.ShapeDtypeStruct((bsz, FOX_HEADS, seq), F32)],
        scratch_shapes=[pltpu.VMEM((8, LANES), F32)],
        compiler_params=_cparams(("parallel", "arbitrary")),
        name="fox_prep",
    )(zm, zm, zs, gate_bias, gq, gk)


def _fox_attn_kernel(qi_ref, kj_ref, q_ref, k_ref, v_ref, f_ref, o_ref, m_scr, l_scr, acc_scr,
                     *, tq, tk, kc):
    pr = pl.program_id(2)
    qi = qi_ref[pr]
    kj = kj_ref[pr]
    h = pl.program_id(1)

    @pl.when(kj == 0)
    def _():
        m_scr[...] = jnp.full(m_scr.shape, -jnp.inf, F32)
        l_scr[...] = jnp.zeros(l_scr.shape, F32)
        acc_scr[...] = jnp.zeros(acc_scr.shape, F32)

    def body(masked):
        q = q_ref[...]
        for c in range(tk // kc):
            kk = k_ref[c * kc:(c + 1) * kc, :]
            fk = f_ref[pl.ds(h, 1), c * kc:(c + 1) * kc]
            s = lax.dot_general(q, kk, (((1,), (1,)), ((), ())), preferred_element_type=F32) - fk
            if masked:
                rows = lax.broadcasted_iota(jnp.int32, (tq, kc), 0)
                cols = lax.broadcasted_iota(jnp.int32, (tq, kc), 1) + c * kc
                s = jnp.where(cols <= rows, s, -jnp.inf)
            m_old = m_scr[...]
            m_new = jnp.maximum(m_old, jnp.max(s, axis=-1, keepdims=True))
            alpha = jnp.exp(m_old - m_new)
            p = jnp.exp(s - m_new)
            l_scr[...] = alpha * l_scr[...] + jnp.sum(p, axis=-1, keepdims=True)
            acc_scr[...] = alpha * acc_scr[...] + jnp.dot(
                p.astype(BF16), v_ref[c * kc:(c + 1) * kc, :], preferred_element_type=F32)
            m_scr[...] = m_new

    @pl.when(kj < qi)
    def _():
        body(False)

    @pl.when(kj == qi)
    def _():
        body(True)
        o_ref[...] = (acc_scr[...] / l_scr[...]).astype(o_ref.dtype)


def _fox_attn(qn, kn, zm, ft, *, bsz, seq, col_v, tq=1024, kc=512):
    t = qn.shape[0]
    d = FOX_HEADDIM
    tq = min(tq, seq)
    kc = min(kc, tq)
    tk = tq
    nq = seq // tq
    qi_tab, kj_tab = [], []
    for i in range(nq):
        for j in range(i + 1):
            qi_tab.append(i)
            kj_tab.append(j)
    qi_tab = jnp.asarray(qi_tab, jnp.int32)
    kj_tab = jnp.asarray(kj_tab, jnp.int32)
    vblk = col_v // d
    grid_spec = pltpu.PrefetchScalarGridSpec(
        num_scalar_prefetch=2,
        grid=(bsz, FOX_HEADS, qi_tab.shape[0]),
        in_specs=[
            pl.BlockSpec((tq, d), lambda b, h, p, qi, kj: (b * nq + qi[p], h)),
            pl.BlockSpec((tk, d), lambda b, h, p, qi, kj: (b * nq + kj[p], h)),
            pl.BlockSpec((tk, d), lambda b, h, p, qi, kj: (b * nq + kj[p], vblk + h)),
            pl.BlockSpec((None, FOX_HEADS, tk), lambda b, h, p, qi, kj: (b, 0, kj[p])),
        ],
        out_specs=pl.BlockSpec((tq, d), lambda b, h, p, qi, kj: (b * nq + qi[p], h)),
        scratch_shapes=[pltpu.VMEM((tq, 1), F32), pltpu.VMEM((tq, 1), F32), pltpu.VMEM((tq, d), F32)],
    )
    return pl.pallas_call(
        functools.partial(_fox_attn_kernel, tq=tq, tk=tk, kc=kc),
        grid_spec=grid_spec,
        out_shape=jax.ShapeDtypeStruct((t, FOX_HEADS * d), BF16),
        compiler_params=_cparams(("parallel", "parallel", "arbitrary")),
        name="fox_attn",
    )(qi_tab, kj_tab, qn, kn, zm, ft)


def _merge_kernel(x_ref, yml_ref, yssm_ref, yfox_ref, gml_ref, gssm_ref, gfox_ref,
                  pml_ref, pssm_ref, pfox_ref, wout_ref, o_ref):
    def branch(y_ref, g_ref, p_ref):
        proj = jnp.dot(y_ref[...], p_ref[...], preferred_element_type=F32)
        return _sigmoid(g_ref[...].astype(F32)) * proj

    merged = (branch(yml_ref, gml_ref, pml_ref) + branch(yssm_ref, gssm_ref, pssm_ref)
              + branch(yfox_ref, gfox_ref, pfox_ref))
    o_ref[...] = x_ref[...] + jnp.dot(merged.astype(BF16), wout_ref[...], preferred_element_type=F32)


def _merge(x2, y_ml, y_ssm, y_fox, zm, p_ml, p_ssm, p_fox, w_out, *, col_g, tm=512):
    t, d = x2.shape
    rowblk = lambda c: pl.BlockSpec((tm, d), lambda i: (i, c))
    wfull = lambda a: pl.BlockSpec(a.shape, lambda i: (0, 0))
    gblk = col_g // d
    return pl.pallas_call(
        _merge_kernel,
        grid=(t // tm,),
        in_specs=[rowblk(0), rowblk(0), rowblk(0), rowblk(0),
                  rowblk(gblk), rowblk(gblk + 1), rowblk(gblk + 2),
                  wfull(p_ml), wfull(p_ssm), wfull(p_fox), wfull(w_out)],
        out_specs=rowblk(0),
        out_shape=jax.ShapeDtypeStruct((t, d), F32),
        compiler_params=_cparams(("parallel",)),
        name="merge",
    )(x2, y_ml, y_ssm, y_fox, zm, zm, zm, p_ml, p_ssm, p_fox, w_out)


def _erf(v):
    return lax.erf(v)


def _gelu(v):
    return 0.5 * v * (1.0 + _erf(v * (2.0 ** -0.5)))


def _top_values(scores, k):
    vals = []
    cur = scores
    for it in range(k):
        m = jnp.max(cur, axis=0, keepdims=True)
        vals.append(m)
        if it + 1 < k:
            cur = jnp.where(cur == m, -jnp.inf, cur)
    return vals


def _peer_kernel(x_ref, g_ref, wq_ref, k1_ref, k2_ref, u_ref, vt_ref, o_ref,
                 h_scr, q_scr, s1_scr, e1_scr, s2_scr, e2_scr, tau_scr, acc_scr, *, tb, eb):
    j = pl.program_id(1)
    nk, topk, nhead = PEER_NKEYS, PEER_TOPK, PEER_HEADS
    half = PEER_QDIM // 2

    @pl.when(j == 0)
    def _():
        x = x_ref[...]
        hn = (x * lax.rsqrt(jnp.mean(x * x, axis=-1, keepdims=True) + EPS) * g_ref[...]).astype(BF16)
        h_scr[...] = hn
        q = jnp.dot(hn, wq_ref[...], preferred_element_type=F32).astype(BF16)
        for c in range(q.shape[1] // half):
            q_scr[c] = q[:, c * half:(c + 1) * half]
        acc_scr[...] = jnp.zeros(acc_scr.shape, F32)

        def head_body(h, carry):
            nt = (((1,), (1,)), ((), ()))
            s1 = lax.dot_general(k1_ref[h], q_scr[2 * h], nt, preferred_element_type=F32)
            s2 = lax.dot_general(k2_ref[h], q_scr[2 * h + 1], nt, preferred_element_type=F32)
            v1 = _top_values(s1, topk)
            v2 = _top_values(s2, topk)
            v2m = jnp.concatenate(v2, axis=0)
            cand = jnp.concatenate([v1[a] + v2m for a in range(topk)], axis=0)
            tv = _top_values(cand, topk)
            mx = tv[0]
            zsum = jnp.zeros_like(mx)
            for a in range(topk):
                zsum = zsum + jnp.exp(tv[a] - mx)
            s1_scr[h] = s1
            s2_scr[h] = s2
            e1_scr[h] = jnp.exp(s1 - v1[0]) / zsum
            e2_scr[h] = jnp.exp(s2 - v2[0])
            tau_scr[pl.ds(h, 1), :] = tv[topk - 1]
            return carry

        lax.fori_loop(0, nhead, head_body, 0)

    act_t = lax.dot_general(u_ref[...], h_scr[...], (((1,), (1,)), ((), ())),
                            preferred_element_type=F32)
    act_t = _gelu(act_t)
    w_parts = []
    for r in range(eb // nk):
        i1 = j * (eb // nk) + r

        def w_body(h, acc):
            s1_row = s1_scr[h, pl.ds(i1, 1), :]
            e1_row = e1_scr[h, pl.ds(i1, 1), :]
            sel = (s1_row + s2_scr[h]) >= tau_scr[pl.ds(h, 1), :]
            return acc + jnp.where(sel, e1_row * e2_scr[h], 0.0)

        w_parts.append(lax.fori_loop(0, nhead, w_body, jnp.zeros((nk, tb), F32)))
    gated = (act_t * jnp.concatenate(w_parts, axis=0)).astype(BF16)
    acc_scr[...] += jnp.dot(vt_ref[...], gated, preferred_element_type=F32)

    @pl.when(j == pl.num_programs(1) - 1)
    def _():
        o_ref[...] = x_ref[...] + acc_scr[...].T


def _peer(x2, g, w_q, keys1, keys2, u, v_t, *, tb=512, eb=256):
    t, d = x2.shape
    ne = u.shape[0]
    nq = w_q.shape[1]
    return pl.pallas_call(
        functools.partial(_peer_kernel, tb=tb, eb=eb),
        grid=(t // tb, ne // eb),
        in_specs=[
            pl.BlockSpec((tb, d), lambda i, j: (i, 0)),
            pl.BlockSpec((1, d), lambda i, j: (0, 0)),
            pl.BlockSpec((d, nq), lambda i, j: (0, 0)),
            pl.BlockSpec(keys1.shape, lambda i, j: (0, 0, 0)),
            pl.BlockSpec(keys2.shape, lambda i, j: (0, 0, 0)),
            pl.BlockSpec((eb, d), lambda i, j: (j, 0)),
            pl.BlockSpec((d, eb), lambda i, j: (0, j)),
        ],
        out_specs=pl.BlockSpec((tb, d), lambda i, j: (i, 0)),
        out_shape=jax.ShapeDtypeStruct((t, d), F32),
        scratch_shapes=[
            pltpu.VMEM((tb, d), BF16),
            pltpu.VMEM((2 * PEER_HEADS, tb, PEER_QDIM // 2), BF16),
            pltpu.VMEM((PEER_HEADS, PEER_NKEYS, tb), F32),
            pltpu.VMEM((PEER_HEADS, PEER_NKEYS, tb), F32),
            pltpu.VMEM((PEER_HEADS, PEER_NKEYS, tb), F32),
            pltpu.VMEM((PEER_HEADS, PEER_NKEYS, tb), F32),
            pltpu.VMEM((PEER_HEADS, tb), F32),
            pltpu.VMEM((d, tb), F32),
        ],
        compiler_params=_cparams(("parallel", "arbitrary")),
        name="peer",
    )(x2, g, w_q, keys1, keys2, u, v_t)


def _ple_kernel(x_ref, p_ref, g_ref, wg_ref, wp_ref, fg_ref, o_ref, *, final):
    x = x_ref[...]
    hn = (x * lax.rsqrt(jnp.mean(x * x, axis=-1, keepdims=True) + EPS) * g_ref[...]).astype(BF16)
    gate = _sigmoid(jnp.dot(hn, wg_ref[...], preferred_element_type=F32))
    proj = jnp.dot(p_ref[...].astype(BF16), wp_ref[...], preferred_element_type=F32)
    y = x + gate * proj
    if final:
        y = y * lax.rsqrt(jnp.mean(y * y, axis=-1, keepdims=True) + EPS) * fg_ref[...]
    o_ref[...] = y


def _ple(x2, p2, g, w_gate, w_proj, final_g, *, final, tm=512):
    t, d = x2.shape
    pd = p2.shape[1]
    full = lambda a: pl.BlockSpec(a.shape, lambda i: (0, 0))
    return pl.pallas_call(
        functools.partial(_ple_kernel, final=final),
        grid=(t // tm,),
        in_specs=[pl.BlockSpec((tm, d), lambda i: (i, 0)), pl.BlockSpec((tm, pd), lambda i: (i, 0)),
                  full(g), full(w_gate), full(w_proj), full(final_g)],
        out_specs=pl.BlockSpec((tm, d), lambda i: (i, 0)),
        out_shape=jax.ShapeDtypeStruct((t, d), F32),
        compiler_params=_cparams(("parallel",)),
        name="ple",
    )(x2, p2, g, w_gate, w_proj, final_g)


def _row(v):
    return v.reshape(1, -1).astype(F32)


def _gate_bias_row(ml_b_i, ml_b_f, dt_bias, fox_b_f):
    r = jnp.zeros((LANES,), F32)
    r = r.at[COL_ML_I:COL_ML_I + ML_HEADS].set(ml_b_i.astype(F32))
    r = r.at[COL_ML_F:COL_ML_F + ML_HEADS].set(ml_b_f.astype(F32))
    r = r.at[COL_DT:COL_DT + SSM_HEADS].set(dt_bias.astype(F32))
    r = r.at[COL_FOX_F:COL_FOX_F + FOX_HEADS].set(fox_b_f.astype(F32))
    return r.reshape(1, LANES)


def _pad_rows(w, rows):
    return jnp.concatenate([w, jnp.zeros((rows - w.shape[0],) + w.shape[1:], w.dtype)], axis=0)


def kernel(x, p, norm_mix_g, w_in, ml_conv_w, ml_conv_b, ml_b_i, ml_b_f, ml_norm_g, ssm_conv_w, ssm_conv_b, ssm_dt_bias, ssm_a_log, ssm_d, ssm_norm_g, fox_q_norm_g, fox_k_norm_g, fox_b_f, w_branch_ml, w_branch_ssm, w_branch_fox, w_out, norm_ffn_g, peer_w_q, peer_keys1, peer_keys2, peer_u, peer_v, norm_ple_g, ple_w_gate, ple_w_proj, final_norm_g):
    bsz, seq, d = x.shape
    depth = w_in.shape[0]
    t = bsz * seq
    ml_qk = ML_HEADS * ML_DQK
    ml_w = ML_HEADS * ML_DV
    ssm_w = SSM_HEADS * SSM_HEADDIM
    ssm_bc = SSM_GROUPS * SSM_STATE
    fox_w = FOX_HEADS * FOX_HEADDIM
    splits = (ml_qk, ml_qk, ml_w, ml_w, ML_HEADS, ML_HEADS, ssm_w, ssm_w, ssm_bc, ssm_bc, SSM_HEADS,
              fox_w, fox_w, fox_w, FOX_HEADS, d, d, d)
    offs = [0]
    for s_ in splits:
        offs.append(offs[-1] + s_)
    (o_mlq, o_mlk, o_mlv, o_mlo, o_mli, o_mlf, o_sz, o_sx, o_sb, o_sc, o_sdt,
     o_fq, o_fk, o_fv, o_ff, o_gml, o_gssm, o_gfox) = offs[:-1]

    main_segments = [(o_mlq, 2 * ml_qk), (o_mlv, ml_w), (o_mlo, ml_w), (o_sz, ssm_w), (o_sx, ssm_w),
                     (o_fq, fox_w), (o_fk, fox_w), (o_fv, fox_w), (o_gml, d), (o_gssm, d), (o_gfox, d),
                     (o_sb, 2 * ssm_bc)]
    seg_off = [0]
    for _, wd in main_segments:
        seg_off.append(seg_off[-1] + wd)
    (c_mlqk, c_mlv, c_mlo, c_sz, c_sx, c_fq, c_fk, c_fv, c_gml, c_gssm, c_gfox, c_bc) = seg_off[:-1]
    assert (c_mlqk, c_mlv, c_mlo) == (0, ml_w, 2 * ml_w)

    x2 = x.reshape(t, d)
    for i in range(depth):
        w = w_in[i]
        w_main = jnp.concatenate([w[:, o:o + wd] for o, wd in main_segments], axis=1).astype(BF16)
        small = jnp.zeros((d, LANES), w.dtype)
        small = small.at[:, COL_ML_I:COL_ML_I + ML_HEADS].set(w[:, o_mli:o_mli + ML_HEADS])
        small = small.at[:, COL_ML_F:COL_ML_F + ML_HEADS].set(w[:, o_mlf:o_mlf + ML_HEADS])
        small = small.at[:, COL_DT:COL_DT + SSM_HEADS].set(w[:, o_sdt:o_sdt + SSM_HEADS])
        small = small.at[:, COL_FOX_F:COL_FOX_F + FOX_HEADS].set(w[:, o_ff:o_ff + FOX_HEADS])
        w_small = small.astype(BF16)
        gate_bias = _gate_bias_row(ml_b_i[i], ml_b_f[i], ssm_dt_bias[i], fox_b_f[i])

        zm, zs = _inproj(x2, _row(norm_mix_g[i]), w_main, w_small)

        y_ml = _mlstm(zm, zs, _pad_rows(ml_conv_w[i].astype(F32), 8), _row(ml_conv_b[i]), gate_bias,
                      _row(ml_norm_g[i]), bsz=bsz, seq=seq)

        scw = ssm_conv_w[i].astype(F32)
        scb = ssm_conv_b[i].astype(F32)
        alog_row = jnp.zeros((LANES,), F32).at[COL_DT:COL_DT + SSM_HEADS].set(
            ssm_a_log[i].astype(F32)).reshape(1, LANES)
        d_row = jnp.repeat(ssm_d[i].astype(F32), SSM_HEADDIM).reshape(1, ssm_w)
        y_ssm = _ssd(zm, zs, _pad_rows(scw[:, :ssm_w], 8), _row(scb[:ssm_w]),
                     _pad_rows(scw[:, ssm_w:], 8), _row(scb[ssm_w:]), gate_bias, alog_row, d_row,
                     _row(ssm_norm_g[i]), bsz=bsz, seq=seq, col_z=c_sz, col_x=c_sx, col_bc=c_bc)

        qn, kn, ft = _fox_prep(zm, zs, gate_bias, _row(fox_q_norm_g[i]), _row(fox_k_norm_g[i]),
                               bsz=bsz, seq=seq, col_q=c_fq, col_k=c_fk)
        y_fox = _fox_attn(qn, kn, zm, ft, bsz=bsz, seq=seq, col_v=c_fv)

        x2 = _merge(x2, y_ml, y_ssm, y_fox, zm, w_branch_ml[i].astype(BF16), w_branch_ssm[i].astype(BF16),
                    w_branch_fox[i].astype(BF16), w_out[i].astype(BF16), col_g=c_gml)

        x2 = _peer(x2, _row(norm_ffn_g[i]), peer_w_q[i].astype(BF16), peer_keys1[i].astype(BF16),
                   peer_keys2[i].astype(BF16), peer_u[i].astype(BF16), peer_v[i].T.astype(BF16))

        x2 = _ple(x2, p[i].reshape(t, -1), _row(norm_ple_g[i]), ple_w_gate[i].astype(BF16),
                  ple_w_proj[i].astype(BF16), _row(final_norm_g), final=(i == depth - 1))
    return x2.reshape(bsz, seq, d)
```

```python
import functools
import math

import jax
import jax.numpy as jnp
from jax import lax
from jax.experimental import pallas as pl
from jax.experimental.pallas import tpu as pltpu

F32 = jnp.float32
BF16 = jnp.bfloat16
EPS = 1e-6
LOG2E = math.log2(math.e)

CONV_WIDTH = 4
ML_HEADS = 8
ML_DQK = 64
ML_DV = 128
SSM_HEADS = 16
SSM_HEADDIM = 64
SSM_GROUPS = 2
SSM_STATE = 128
FOX_HEADS = 8
FOX_HEADDIM = 128
PEER_HEADS = 8
PEER_NKEYS = 128
PEER_QDIM = 256
PEER_TOPK = 16

LANES = 128
HALO = 8

COL_ML_I = 0
COL_ML_F = 8
COL_DT = 16
COL_FOX_F = 32

VMEM_LIMIT = 56 * 1024 * 1024


def _cparams(sem, flags=None):
    return pltpu.CompilerParams(dimension_semantics=sem, vmem_limit_bytes=VMEM_LIMIT, flags=flags)


def _log_sigmoid(v):
    return jnp.minimum(v, 0.0) - jnp.log1p(jnp.exp(-jnp.abs(v)))


def _sigmoid(v):
    return 1.0 / (1.0 + jnp.exp(-v))


def _silu(v):
    return v * _sigmoid(v)


def _softplus(v):
    return jnp.maximum(v, 0.0) + jnp.log1p(jnp.exp(-jnp.abs(v)))


def _tril(n):
    r = lax.broadcasted_iota(jnp.int32, (n, n), 0)
    c = lax.broadcasted_iota(jnp.int32, (n, n), 1)
    return r >= c


def _cumsum_rows(v):
    n = v.shape[0]
    return jnp.dot(_tril(n).astype(F32), v, precision=lax.Precision.HIGHEST,
                   preferred_element_type=F32)


def _inproj_kernel(x_ref, g_ref, wm_ref, ws_ref, zm_ref, zs_ref, h_scr):
    @pl.when(pl.program_id(1) == 0)
    def _():
        x = x_ref[...]
        ms = jnp.mean(x * x, axis=-1, keepdims=True)
        h = (x * lax.rsqrt(ms + EPS) * g_ref[...]).astype(BF16)
        h_scr[...] = h
        zs_ref[...] = jnp.dot(h, ws_ref[...], preferred_element_type=F32)

    zm_ref[...] = jnp.dot(h_scr[...], wm_ref[...], preferred_element_type=F32).astype(zm_ref.dtype)


def _inproj(x2, g, w_main, w_small, tm=1024, tn=512):
    t, d = x2.shape
    n = w_main.shape[1]
    return pl.pallas_call(
        _inproj_kernel,
        grid=(t // tm, n // tn),
        in_specs=[
            pl.BlockSpec((tm, d), lambda i, j: (i, 0)),
            pl.BlockSpec((1, d), lambda i, j: (0, 0)),
            pl.BlockSpec((d, tn), lambda i, j: (0, j)),
            pl.BlockSpec((d, LANES), lambda i, j: (0, 0)),
        ],
        out_specs=[
            pl.BlockSpec((tm, tn), lambda i, j: (i, j)),
            pl.BlockSpec((tm, LANES), lambda i, j: (i, 0)),
        ],
        out_shape=[jax.ShapeDtypeStruct((t, n), BF16), jax.ShapeDtypeStruct((t, LANES), F32)],
        scratch_shapes=[pltpu.VMEM((tm, d), BF16)],
        compiler_params=_cparams(("parallel", "arbitrary")),
        name="inproj",
    )(x2, g, w_main, w_small)


def _conv_chunk(buf_ref, w_ref, b_ref, start, length):
    acc = None
    for j in range(CONV_WIDTH):
        off = HALO - (CONV_WIDTH - 1) + j + start
        term = buf_ref[off:off + length, :] * w_ref[j:j + 1, :]
        acc = term if acc is None else acc + term
    return acc + b_ref[...]


def _mlstm_kernel(qk_ref, v_ref, o_ref, gs_ref, cw_ref, cb_ref, gb_ref, ng_ref, y_ref,
                  buf, c_scr, m_scr, *, ts, chunk):
    nh, dk, dv = ML_HEADS, ML_DQK, ML_DV
    qkw = nh * dk

    @pl.when(pl.program_id(1) == 0)
    def _():
        buf[0:HALO, :] = jnp.zeros((HALO, buf.shape[1]), F32)
        c_scr[...] = jnp.zeros(c_scr.shape, F32)
        m_scr[...] = jnp.zeros(m_scr.shape, F32)

    buf[HALO:HALO + ts, :] = qk_ref[...].astype(F32)
    causal = _tril(chunk)
    lane = lax.broadcasted_iota(jnp.int32, (chunk, LANES), 1)
    ones_col = (lane == 0).astype(BF16)

    for c in range(ts // chunk):
        r0 = c * chunk
        act = _silu(_conv_chunk(buf, cw_ref, cb_ref, r0, chunk))
        q_all = (act[:, :qkw] * (dk ** -0.5)).astype(BF16)
        kt_all = act[:, qkw:].T
        gates = gs_ref[r0:r0 + chunk, :] + gb_ref[...]
        bcum = _cumsum_rows(_log_sigmoid(gates))
        gates_t = gates.T
        bcum_t = bcum.T
        for h in range(nh):
            i_col = gates[:, COL_ML_I + h:COL_ML_I + h + 1]
            i_row = gates_t[COL_ML_I + h:COL_ML_I + h + 1, :]
            b_col = bcum[:, COL_ML_F + h:COL_ML_F + h + 1]
            b_row = bcum_t[COL_ML_F + h:COL_ML_F + h + 1, :]
            g_tot = b_col[chunk - 1:chunk, :]
            m_prev = m_scr[h:h + 1, 0:1]
            dmat = jnp.where(causal, b_col - b_row + i_row, -jnp.inf)
            inter = b_col + m_prev
            m_t = jnp.maximum(inter, jnp.max(dmat, axis=-1, keepdims=True))
            qh = q_all[:, h * dk:(h + 1) * dk]
            kt = kt_all[h * dk:(h + 1) * dk, :]
            s = jnp.dot(qh, kt.astype(BF16), preferred_element_type=F32)
            sc = (s * jnp.exp(dmat - m_t)).astype(BF16)
            w_inter = jnp.exp(inter - m_t)
            vaug = jnp.concatenate([v_ref[r0:r0 + chunk, h * dv:(h + 1) * dv], ones_col], axis=1)
            cmem = c_scr[h]
            tot = (jnp.dot(sc, vaug, preferred_element_type=F32)
                   + w_inter * jnp.dot(qh, cmem.astype(BF16), preferred_element_type=F32))
            num = tot[:, :dv]
            den = tot[:, dv:dv + 1]
            hv = num / jnp.maximum(jnp.abs(den), jnp.exp(-m_t))
            hn = hv * lax.rsqrt(jnp.mean(hv * hv, axis=-1, keepdims=True) + EPS)
            hn = hn * ng_ref[:, h * dv:(h + 1) * dv]
            og = _sigmoid(o_ref[r0:r0 + chunk, h * dv:(h + 1) * dv].astype(F32))
            y_ref[r0:r0 + chunk, h * dv:(h + 1) * dv] = (og * hn).astype(y_ref.dtype)
            a_row = g_tot - b_row + i_row
            m_new = jnp.maximum(g_tot + m_prev, jnp.max(a_row, axis=-1, keepdims=True))
            wa_row = jnp.exp(a_row - m_new)
            decay = jnp.exp(g_tot + m_prev - m_new)
            kw = (kt * wa_row).astype(BF16)
            c_scr[h] = decay * cmem + jnp.dot(kw, vaug, preferred_element_type=F32)
            m_scr[h:h + 1, :] = jnp.broadcast_to(m_new, (1, LANES))

    buf[0:HALO, :] = buf[ts:ts + HALO, :]


def _mlstm(zm, zs, conv_w, conv_b, gate_bias, norm_g, *, bsz, seq, ts=256, chunk=128):
    t = zm.shape[0]
    w = ML_HEADS * ML_DV
    nblk = seq // ts
    row = lambda b, s: b * nblk + s
    return pl.pallas_call(
        functools.partial(_mlstm_kernel, ts=ts, chunk=chunk),
        grid=(bsz, nblk),
        in_specs=[
            pl.BlockSpec((ts, w), lambda b, s: (row(b, s), 0)),
            pl.BlockSpec((ts, w), lambda b, s: (row(b, s), 1)),
            pl.BlockSpec((ts, w), lambda b, s: (row(b, s), 2)),
            pl.BlockSpec((ts, LANES), lambda b, s: (row(b, s), 0)),
            pl.BlockSpec(conv_w.shape, lambda b, s: (0, 0)),
            pl.BlockSpec(conv_b.shape, lambda b, s: (0, 0)),
            pl.BlockSpec(gate_bias.shape, lambda b, s: (0, 0)),
            pl.BlockSpec(norm_g.shape, lambda b, s: (0, 0)),
        ],
        out_specs=pl.BlockSpec((ts, w), lambda b, s: (row(b, s), 0)),
        out_shape=jax.ShapeDtypeStruct((t, w), BF16),
        scratch_shapes=[
            pltpu.VMEM((HALO + ts, w), F32),
            pltpu.VMEM((ML_HEADS, ML_DQK, 2 * ML_DV), F32),
            pltpu.VMEM((ML_HEADS, LANES), F32),
        ],
        compiler_params=_cparams(("parallel", "arbitrary")),
        name="mlstm",
    )(zm, zm, zm, zs, conv_w, conv_b, gate_bias, norm_g)


def _ssd_kernel(z_ref, x_ref, bc_ref, gs_ref, cwx_ref, cbx_ref, cwbc_ref, cbbc_ref, gb_ref,
                alog_ref, d_ref, ng_ref, y_ref, bufx, bufbc, st_scr, *, ts, chunk):
    ng, nst, p = SSM_GROUPS, SSM_STATE, SSM_HEADDIM
    hg = SSM_HEADS // ng
    gw = hg * p
    pairs_per_group = gw // LANES

    @pl.when(pl.program_id(1) == 0)
    def _():
        bufx[0:HALO, :] = jnp.zeros((HALO, bufx.shape[1]), F32)
        bufbc[0:HALO, :] = jnp.zeros((HALO, bufbc.shape[1]), F32)
        st_scr[...] = jnp.zeros(st_scr.shape, F32)

    bufx[HALO:HALO + ts, :] = x_ref[...].astype(F32)
    bufbc[HALO:HALO + ts, :] = bc_ref[...].astype(F32)
    causal = _tril(chunk)
    low_half = lax.broadcasted_iota(jnp.int32, (chunk, LANES), 1) < p
    a_row_all = -jnp.exp(alog_ref[...])

    for c in range(ts // chunk):
        r0 = c * chunk
        xa = _silu(_conv_chunk(bufx, cwx_ref, cbx_ref, r0, chunk))
        bca = _silu(_conv_chunk(bufbc, cwbc_ref, cbbc_ref, r0, chunk))
        dt = _softplus(gs_ref[r0:r0 + chunk, :] + gb_ref[...])
        acum = _cumsum_rows(dt * a_row_all)
        acum_t = acum.T
        y_parts = []
        for g in range(ng):
            bg = bca[:, g * nst:(g + 1) * nst]
            cg = bca[:, ng * nst + g * nst:ng * nst + (g + 1) * nst].astype(BF16)
            bg_t = bg.T
            cbt = jnp.dot(cg, bg_t.astype(BF16), preferred_element_type=F32)
            state = st_scr[g]
            inter = jnp.dot(cg, state.astype(BF16), preferred_element_type=F32)
            xs_scaled, last_parts = [], []
            for pp in range(pairs_per_group):
                pidx = g * pairs_per_group + pp
                ha = 2 * pidx
                ca, cb = COL_DT + ha, COL_DT + ha + 1
                sl = slice(pidx * LANES, (pidx + 1) * LANES)
                xa_p = xa[:, sl]
                dt_pair = jnp.where(low_half, dt[:, ca:ca + 1], dt[:, cb:cb + 1])
                ac_pair = jnp.where(low_half, acum[:, ca:ca + 1], acum[:, cb:cb + 1])
                xs_p = xa_p * dt_pair
                xs_b = xs_p.astype(BF16)
                lm_a = jnp.exp(jnp.where(causal, acum[:, ca:ca + 1] - acum_t[ca:ca + 1, :], -jnp.inf))
                lm_b = jnp.exp(jnp.where(causal, acum[:, cb:cb + 1] - acum_t[cb:cb + 1, :], -jnp.inf))
                ya = jnp.dot((cbt * lm_a).astype(BF16), xs_b, preferred_element_type=F32)
                yb = jnp.dot((cbt * lm_b).astype(BF16), xs_b, preferred_element_type=F32)
                y_p = (jnp.where(low_half, ya, yb)
                       + inter[:, pp * LANES:(pp + 1) * LANES] * jnp.exp(ac_pair)
                       + d_ref[:, sl] * xa_p)
                y_parts.append(y_p)
                last = ac_pair[chunk - 1:chunk, :]
                xs_scaled.append((xs_p * jnp.exp(last - ac_pair)).astype(BF16))
                last_parts.append(last)
            xs_g = jnp.concatenate(xs_scaled, axis=1)
            last_g = jnp.concatenate(last_parts, axis=1)
            st_scr[g] = jnp.exp(last_g) * state + jnp.dot(bg_t.astype(BF16), xs_g,
                                                          preferred_element_type=F32)
        ys = jnp.concatenate(y_parts, axis=1)
        ys = ys * _silu(z_ref[r0:r0 + chunk, :].astype(F32))
        outs = []
        for g in range(ng):
            yg = ys[:, g * gw:(g + 1) * gw]
            yn = yg * lax.rsqrt(jnp.mean(yg * yg, axis=-1, keepdims=True) + EPS)
            outs.append(yn * ng_ref[:, g * gw:(g + 1) * gw])
        y_ref[r0:r0 + chunk, :] = jnp.concatenate(outs, axis=1).astype(y_ref.dtype)

    bufx[0:HALO, :] = bufx[ts:ts + HALO, :]
    bufbc[0:HALO, :] = bufbc[ts:ts + HALO, :]


def _ssd(zm, zs, cwx, cbx, cwbc, cbbc, gate_bias, alog_row, d_row, norm_g, *, bsz, seq,
         col_z, col_x, col_bc, ts=256, chunk=128):
    t = zm.shape[0]
    w = SSM_HEADS * SSM_HEADDIM
    bcw = 2 * SSM_GROUPS * SSM_STATE
    nblk = seq // ts
    row = lambda b, s: b * nblk + s
    full = lambda a: pl.BlockSpec(a.shape, lambda b, s: (0, 0))
    return pl.pallas_call(
        functools.partial(_ssd_kernel, ts=ts, chunk=chunk),
        grid=(bsz, nblk),
        in_specs=[
            pl.BlockSpec((ts, w), lambda b, s: (row(b, s), col_z // w)),
            pl.BlockSpec((ts, w), lambda b, s: (row(b, s), col_x // w)),
            pl.BlockSpec((ts, bcw), lambda b, s: (row(b, s), col_bc // bcw)),
            pl.BlockSpec((ts, LANES), lambda b, s: (row(b, s), 0)),
            full(cwx), full(cbx), full(cwbc), full(cbbc), full(gate_bias),
            full(alog_row), full(d_row), full(norm_g),
        ],
        out_specs=pl.BlockSpec((ts, w), lambda b, s: (row(b, s), 0)),
        out_shape=jax.ShapeDtypeStruct((t, w), BF16),
        scratch_shapes=[
            pltpu.VMEM((HALO + ts, w), F32),
            pltpu.VMEM((HALO + ts, bcw), F32),
            pltpu.VMEM((SSM_GROUPS, SSM_STATE, w // SSM_GROUPS), F32),
        ],
        compiler_params=_cparams(("parallel", "arbitrary")),
        name="ssd",
    )(zm, zm, zm, zs, cwx, cbx, cwbc, cbbc, gate_bias, alog_row, d_row, norm_g)


def _fox_prep_kernel(q_ref, k_ref, gs_ref, gb_ref, gq_ref, gk_ref, qn_ref, kn_ref, ft_ref,
                     carry, *, ts):
    d = FOX_HEADDIM

    @pl.when(pl.program_id(1) == 0)
    def _():
        carry[...] = jnp.zeros(carry.shape, F32)

    for h in range(FOX_HEADS):
        sl = slice(h * d, (h + 1) * d)
        qh = q_ref[:, sl].astype(F32)
        qn = qh * lax.rsqrt(jnp.mean(qh * qh, axis=-1, keepdims=True) + EPS) * gq_ref[...]
        qn_ref[:, sl] = qn.astype(qn_ref.dtype)
        kh = k_ref[:, sl].astype(F32)
        kn = kh * lax.rsqrt(jnp.mean(kh * kh, axis=-1, keepdims=True) + EPS) * gk_ref[...]
        kn_ref[:, sl] = (kn * (d ** -0.5 * LOG2E)).astype(kn_ref.dtype)

    lf = _log_sigmoid(gs_ref[...] + gb_ref[...])
    fc = _cumsum_rows(lf) + carry[0:1, :]
    carry[0:1, :] = fc[ts - 1:ts, :]
    ft_ref[...] = fc.T[COL_FOX_F:COL_FOX_F + FOX_HEADS, :] * LOG2E


def _fox_prep(zm, zs, gate_bias, gq, gk, *, bsz, seq, col_q, col_k, ts=256):
    t = zm.shape[0]
    w = FOX_HEADS * FOX_HEADDIM
    nblk = seq // ts
    row = lambda b, s: b * nblk + s
    full = lambda a: pl.BlockSpec(a.shape, lambda b, s: (0, 0))
    return pl.pallas_call(
        functools.partial(_fox_prep_kernel, ts=ts),
        grid=(bsz, nblk),
        in_specs=[
            pl.BlockSpec((ts, w), lambda b, s: (row(b, s), col_q // w)),
            pl.BlockSpec((ts, w), lambda b, s: (row(b, s), col_k // w)),
            pl.BlockSpec((ts, LANES), lambda b, s: (row(b, s), 0)),
            full(gate_bias), full(gq), full(gk),
        ],
        out_specs=[
            pl.BlockSpec((ts, w), lambda b, s: (row(b, s), 0)),
            pl.BlockSpec((ts, w), lambda b, s: (row(b, s), 0)),
            pl.BlockSpec((None, FOX_HEADS, ts), lambda b, s: (b, 0, s)),
        ],
        out_shape=[jax.ShapeDtypeStruct((t, w), BF16), jax.ShapeDtypeStruct((t, w), BF16),
                   jax.ShapeDtypeStruct((bsz, FOX_HEADS, seq), F32)],
        scratch_shapes=[pltpu.VMEM((8, LANES), F32)],
        compiler_params=_cparams(("parallel", "arbitrary")),
        name="fox_prep",
    )(zm, zm, zs, gate_bias, gq, gk)


def _fox_attn_kernel(qi_ref, kj_ref, q_ref, k_ref, v_ref, f_ref, o_ref, m_scr, acc_scr, vaug_scr,
                     *, tq, tk, rb):
    pr = pl.program_id(2)
    qi = qi_ref[pr]
    kj = kj_ref[pr]
    h = pl.program_id(1)
    d = FOX_HEADDIM

    @pl.when(kj == 0)
    def _():
        m_scr[...] = jnp.full(m_scr.shape, -jnp.inf, F32)
        acc_scr[...] = jnp.zeros(acc_scr.shape, F32)

    vaug_scr[:, :d] = v_ref[...]
    vaug_scr[:, d:] = (lax.broadcasted_iota(jnp.int32, (tk, d), 1) == 0).astype(BF16)
    fk_all = f_ref[pl.ds(h, 1), :]

    def body(masked):
        for r in range(tq // rb):
            rows = slice(r * rb, (r + 1) * rb)
            ncol = (r + 1) * rb if masked else tk
            q = q_ref[rows, :]
            s = lax.dot_general(q, k_ref[:ncol, :], (((1,), (1,)), ((), ())),
                                preferred_element_type=F32) - fk_all[:, :ncol]
            if masked:
                rr = lax.broadcasted_iota(jnp.int32, (rb, ncol), 0) + r * rb
                cc = lax.broadcasted_iota(jnp.int32, (rb, ncol), 1)
                s = jnp.where(cc <= rr, s, -jnp.inf)
            m_old = m_scr[rows, :]
            m_new = jnp.maximum(m_old, jnp.max(s, axis=-1, keepdims=True))
            alpha = jnp.exp2(m_old - m_new)
            p = jnp.exp2(s - m_new[:, :1]).astype(BF16)
            pv = jnp.dot(p, vaug_scr[:ncol, :], preferred_element_type=F32)
            acc_scr[rows, :] = jnp.concatenate([alpha, alpha], axis=1) * acc_scr[rows, :] + pv
            m_scr[rows, :] = m_new

    @pl.when(kj < qi)
    def _():
        body(False)

    @pl.when(kj == qi)
    def _():
        body(True)
        acc = acc_scr[...]
        o_ref[...] = (acc[:, :d] / acc[:, d:d + 1]).astype(o_ref.dtype)


def _fox_attn(qn, kn, zm, ft, *, bsz, seq, col_v, tq=1024, rb=256):
    t = qn.shape[0]
    d = FOX_HEADDIM
    tq = min(tq, seq)
    rb = min(rb, tq)
    tk = tq
    nq = seq // tq
    qi_tab, kj_tab = [], []
    for i in range(nq):
        for j in range(i + 1):
            qi_tab.append(i)
            kj_tab.append(j)
    qi_tab = jnp.asarray(qi_tab, jnp.int32)
    kj_tab = jnp.asarray(kj_tab, jnp.int32)
    vblk = col_v // d
    grid_spec = pltpu.PrefetchScalarGridSpec(
        num_scalar_prefetch=2,
        grid=(bsz, FOX_HEADS, qi_tab.shape[0]),
        in_specs=[
            pl.BlockSpec((tq, d), lambda b, h, p, qi, kj: (b * nq + qi[p], h)),
            pl.BlockSpec((tk, d), lambda b, h, p, qi, kj: (b * nq + kj[p], h)),
            pl.BlockSpec((tk, d), lambda b, h, p, qi, kj: (b * nq + kj[p], vblk + h)),
            pl.BlockSpec((None, FOX_HEADS, tk), lambda b, h, p, qi, kj: (b, 0, kj[p])),
        ],
        out_specs=pl.BlockSpec((tq, d), lambda b, h, p, qi, kj: (b * nq + qi[p], h)),
        scratch_shapes=[pltpu.VMEM((tq, LANES), F32), pltpu.VMEM((tq, 2 * d), F32),
                        pltpu.VMEM((tk, 2 * d), BF16)],
    )
    return pl.pallas_call(
        functools.partial(_fox_attn_kernel, tq=tq, tk=tk, rb=rb),
        grid_spec=grid_spec,
        out_shape=jax.ShapeDtypeStruct((t, FOX_HEADS * d), BF16),
        compiler_params=_cparams(("parallel", "parallel", "arbitrary")),
        name="fox_attn",
    )(qi_tab, kj_tab, qn, kn, zm, ft)


def _merge_kernel(x_ref, yml_ref, yssm_ref, yfox_ref, gml_ref, gssm_ref, gfox_ref,
                  pml_ref, pssm_ref, pfox_ref, wout_ref, o_ref):
    def branch(y_ref, g_ref, p_ref):
        proj = jnp.dot(y_ref[...], p_ref[...], preferred_element_type=F32)
        return _sigmoid(g_ref[...].astype(F32)) * proj

    merged = (branch(yml_ref, gml_ref, pml_ref) + branch(yssm_ref, gssm_ref, pssm_ref)
              + branch(yfox_ref, gfox_ref, pfox_ref))
    o_ref[...] = x_ref[...] + jnp.dot(merged.astype(BF16), wout_ref[...], preferred_element_type=F32)


def _merge(x2, y_ml, y_ssm, y_fox, zm, p_ml, p_ssm, p_fox, w_out, *, col_g, tm=512):
    t, d = x2.shape
    rowblk = lambda c: pl.BlockSpec((tm, d), lambda i: (i, c))
    wfull = lambda a: pl.BlockSpec(a.shape, lambda i: (0, 0))
    gblk = col_g // d
    return pl.pallas_call(
        _merge_kernel,
        grid=(t // tm,),
        in_specs=[rowblk(0), rowblk(0), rowblk(0), rowblk(0),
                  rowblk(gblk), rowblk(gblk + 1), rowblk(gblk + 2),
                  wfull(p_ml), wfull(p_ssm), wfull(p_fox), wfull(w_out)],
        out_specs=rowblk(0),
        out_shape=jax.ShapeDtypeStruct((t, d), F32),
        compiler_params=_cparams(("parallel",)),
        name="merge",
    )(x2, y_ml, y_ssm, y_fox, zm, zm, zm, p_ml, p_ssm, p_fox, w_out)


def _erf(v):
    return lax.erf(v)


def _gelu(v):
    return 0.5 * v * (1.0 + _erf(v * (2.0 ** -0.5)))


def _top_values(scores, k):
    vals = []
    cur = scores
    for it in range(k):
        m = jnp.max(cur, axis=0, keepdims=True)
        vals.append(m)
        if it + 1 < k:
            cur = jnp.where(cur == m, -jnp.inf, cur)
    return vals


def _candidate_sums(v1, v2, k):
    sub = 8
    v1m = jnp.concatenate(v1, axis=0)
    v2m = jnp.concatenate(v2, axis=0)
    row = lax.broadcasted_iota(jnp.int32, (sub, v1m.shape[1]), 0)
    groups = [v1[0] + v2m]
    a = 1
    while k // (a + 1) > 1:
        nb = k // (a + 1)
        assert nb <= sub
        grp = v1[a] + v2m[:sub]
        groups.append(grp if nb == sub else jnp.where(row < nb, grp, -jnp.inf))
        a += 1
    assert a % sub == 0
    groups.append(v1m[a:] + v2[0])
    return jnp.concatenate(groups, axis=0)


def _peer_kernel(x_ref, g_ref, wq_ref, k1_ref, k2_ref, u_ref, vt_ref, o_ref,
                 ht_scr, q_scr, s1_scr, e1_scr, s2_scr, e2_scr, tau_scr, acc_scr, gated_scr, w_scr,
                 rows_scr, *, tb, eb):
    j = pl.program_id(1)
    nk, topk, nhead = PEER_NKEYS, PEER_TOPK, PEER_HEADS
    half = PEER_QDIM // 2

    @pl.when(j == 0)
    def _():
        x = x_ref[...]
        hn32 = x * lax.rsqrt(jnp.mean(x * x, axis=-1, keepdims=True) + EPS) * g_ref[...]
        ht_scr[...] = hn32.T.astype(BF16)
        q = jnp.dot(hn32.astype(BF16), wq_ref[...], preferred_element_type=F32).astype(BF16)
        for c in range(q.shape[1] // half):
            q_scr[c] = q[:, c * half:(c + 1) * half]
        acc_scr[...] = jnp.zeros(acc_scr.shape, F32)
        gated_scr[0] = jnp.zeros(gated_scr.shape[1:], BF16)
        w_scr[1] = jnp.zeros(w_scr.shape[1:], F32)

        def head_body(h, carry):
            nt = (((1,), (1,)), ((), ()))
            s1 = lax.dot_general(k1_ref[h], q_scr[2 * h], nt, preferred_element_type=F32)
            s2 = lax.dot_general(k2_ref[h], q_scr[2 * h + 1], nt, preferred_element_type=F32)
            v1 = _top_values(s1, topk)
            v2 = _top_values(s2, topk)
            tv = _top_values(_candidate_sums(v1, v2, topk), topk)
            mx = tv[0]
            zsum = jnp.zeros_like(mx)
            for a in range(topk):
                zsum = zsum + jnp.exp(tv[a] - mx)
            s1_scr[h] = s1
            s2_scr[h] = s2
            e1_scr[h] = jnp.exp(s1 - v1[0]) / zsum
            e2_scr[h] = jnp.exp(s2 - v2[0])
            tau_scr[pl.ds(h, 1), :] = tv[topk - 1]
            return carry

        lax.fori_loop(0, nhead, head_body, 0)

    n_r = eb // nk
    n_e = pl.num_programs(1) - 2
    par = j % 2
    hrows = nk // 2

    acc_scr[...] += jnp.dot(vt_ref[...], gated_scr[par], preferred_element_type=F32)

    act = _gelu(jnp.dot(u_ref[...], ht_scr[...], preferred_element_type=F32))
    gated_scr[1 - par] = (act * w_scr[1 - par]).astype(BF16)

    ja = jnp.minimum(j, n_e - 1)
    for r in range(n_r):
        i1 = ja * n_r + r
        for h in range(nhead):
            rows_scr[2 * h:2 * h + 1, :] = s1_scr[h, pl.ds(i1, 1), :]
            rows_scr[2 * h + 1:2 * h + 2, :] = e1_scr[h, pl.ds(i1, 1), :]
        for lc in range(tb // LANES):
            ls = slice(lc * LANES, (lc + 1) * LANES)
            for hf in range(2):
                rows = slice(hf * hrows, (hf + 1) * hrows)
                wsum = None
                for h in range(nhead):
                    sel = (rows_scr[2 * h:2 * h + 1, ls] + s2_scr[h, rows, ls]) >= tau_scr[h:h + 1, ls]
                    term = jnp.where(sel, rows_scr[2 * h + 1:2 * h + 2, ls] * e2_scr[h, rows, ls], 0.0)
                    wsum = term if wsum is None else wsum + term
                w_scr[par, r * nk + hf * hrows:r * nk + (hf + 1) * hrows, ls] = wsum

    @pl.when(j == n_e + 1)
    def _():
        o_ref[...] = x_ref[...] + acc_scr[...].T


def _peer(x2, g, w_q, keys1, keys2, u, v_t, *, tb=512, eb=512):
    t, d = x2.shape
    ne = u.shape[0]
    nq = w_q.shape[1]
    n_e = ne // eb
    return pl.pallas_call(
        functools.partial(_peer_kernel, tb=tb, eb=eb),
        grid=(t // tb, n_e + 2),
        in_specs=[
            pl.BlockSpec((tb, d), lambda i, j: (i, 0)),
            pl.BlockSpec((1, d), lambda i, j: (0, 0)),
            pl.BlockSpec((d, nq), lambda i, j: (0, 0)),
            pl.BlockSpec(keys1.shape, lambda i, j: (0, 0, 0)),
            pl.BlockSpec(keys2.shape, lambda i, j: (0, 0, 0)),
            pl.BlockSpec((eb, d), lambda i, j: (jnp.clip(j - 1, 0, n_e - 1), 0)),
            pl.BlockSpec((d, eb), lambda i, j: (0, jnp.clip(j - 2, 0, n_e - 1))),
        ],
        out_specs=pl.BlockSpec((tb, d), lambda i, j: (i, 0)),
        out_shape=jax.ShapeDtypeStruct((t, d), F32),
        scratch_shapes=[
            pltpu.VMEM((d, tb), BF16),
            pltpu.VMEM((2 * PEER_HEADS, tb, PEER_QDIM // 2), BF16),
            pltpu.VMEM((PEER_HEADS, PEER_NKEYS, tb), F32),
            pltpu.VMEM((PEER_HEADS, PEER_NKEYS, tb), F32),
            pltpu.VMEM((PEER_HEADS, PEER_NKEYS, tb), F32),
            pltpu.VMEM((PEER_HEADS, PEER_NKEYS, tb), F32),
            pltpu.VMEM((PEER_HEADS, tb), F32),
            pltpu.VMEM((d, tb), F32),
            pltpu.VMEM((2, eb, tb), BF16),
            pltpu.VMEM((2, eb, tb), F32),
            pltpu.VMEM((2 * PEER_HEADS, tb), F32),
        ],
        compiler_params=_cparams(("parallel", "arbitrary")),
        name="peer",
    )(x2, g, w_q, keys1, keys2, u, v_t)


def _ple_kernel(x_ref, p_ref, g_ref, wg_ref, wp_ref, fg_ref, o_ref, *, final):
    x = x_ref[...]
    hn = (x * lax.rsqrt(jnp.mean(x * x, axis=-1, keepdims=True) + EPS) * g_ref[...]).astype(BF16)
    gate = _sigmoid(jnp.dot(hn, wg_ref[...], preferred_element_type=F32))
    proj = jnp.dot(p_ref[...].astype(BF16), wp_ref[...], preferred_element_type=F32)
    y = x + gate * proj
    if final:
        y = y * lax.rsqrt(jnp.mean(y * y, axis=-1, keepdims=True) + EPS) * fg_ref[...]
    o_ref[...] = y


def _ple(x2, p2, g, w_gate, w_proj, final_g, *, final, tm=512):
    t, d = x2.shape
    pd = p2.shape[1]
    full = lambda a: pl.BlockSpec(a.shape, lambda i: (0, 0))
    return pl.pallas_call(
        functools.partial(_ple_kernel, final=final),
        grid=(t // tm,),
        in_specs=[pl.BlockSpec((tm, d), lambda i: (i, 0)), pl.BlockSpec((tm, pd), lambda i: (i, 0)),
                  full(g), full(w_gate), full(w_proj), full(final_g)],
        out_specs=pl.BlockSpec((tm, d), lambda i: (i, 0)),
        out_shape=jax.ShapeDtypeStruct((t, d), F32),
        compiler_params=_cparams(("parallel",)),
        name="ple",
    )(x2, p2, g, w_gate, w_proj, final_g)


def _row(v):
    return v.reshape(1, -1).astype(F32)


def _gate_bias_row(ml_b_i, ml_b_f, dt_bias, fox_b_f):
    r = jnp.zeros((LANES,), F32)
    r = r.at[COL_ML_I:COL_ML_I + ML_HEADS].set(ml_b_i.astype(F32))
    r = r.at[COL_ML_F:COL_ML_F + ML_HEADS].set(ml_b_f.astype(F32))
    r = r.at[COL_DT:COL_DT + SSM_HEADS].set(dt_bias.astype(F32))
    r = r.at[COL_FOX_F:COL_FOX_F + FOX_HEADS].set(fox_b_f.astype(F32))
    return r.reshape(1, LANES)


def _pad_rows(w, rows):
    return jnp.concatenate([w, jnp.zeros((rows - w.shape[0],) + w.shape[1:], w.dtype)], axis=0)


def kernel(x, p, norm_mix_g, w_in, ml_conv_w, ml_conv_b, ml_b_i, ml_b_f, ml_norm_g, ssm_conv_w, ssm_conv_b, ssm_dt_bias, ssm_a_log, ssm_d, ssm_norm_g, fox_q_norm_g, fox_k_norm_g, fox_b_f, w_branch_ml, w_branch_ssm, w_branch_fox, w_out, norm_ffn_g, peer_w_q, peer_keys1, peer_keys2, peer_u, peer_v, norm_ple_g, ple_w_gate, ple_w_proj, final_norm_g):
    bsz, seq, d = x.shape
    depth = w_in.shape[0]
    t = bsz * seq
    ml_qk = ML_HEADS * ML_DQK
    ml_w = ML_HEADS * ML_DV
    ssm_w = SSM_HEADS * SSM_HEADDIM
    ssm_bc = SSM_GROUPS * SSM_STATE
    fox_w = FOX_HEADS * FOX_HEADDIM
    splits = (ml_qk, ml_qk, ml_w, ml_w, ML_HEADS, ML_HEADS, ssm_w, ssm_w, ssm_bc, ssm_bc, SSM_HEADS,
              fox_w, fox_w, fox_w, FOX_HEADS, d, d, d)
    offs = [0]
    for s_ in splits:
        offs.append(offs[-1] + s_)
    (o_mlq, o_mlk, o_mlv, o_mlo, o_mli, o_mlf, o_sz, o_sx, o_sb, o_sc, o_sdt,
     o_fq, o_fk, o_fv, o_ff, o_gml, o_gssm, o_gfox) = offs[:-1]

    main_segments = [(o_mlq, 2 * ml_qk), (o_mlv, ml_w), (o_mlo, ml_w), (o_sz, ssm_w), (o_sx, ssm_w),
                     (o_fq, fox_w), (o_fk, fox_w), (o_fv, fox_w), (o_gml, d), (o_gssm, d), (o_gfox, d),
                     (o_sb, 2 * ssm_bc)]
    seg_off = [0]
    for _, wd in main_segments:
        seg_off.append(seg_off[-1] + wd)
    (c_mlqk, c_mlv, c_mlo, c_sz, c_sx, c_fq, c_fk, c_fv, c_gml, c_gssm, c_gfox, c_bc) = seg_off[:-1]
    assert (c_mlqk, c_mlv, c_mlo) == (0, ml_w, 2 * ml_w)

    x2 = x.reshape(t, d)
    for i in range(depth):
        w = w_in[i]
        w_main = jnp.concatenate([w[:, o:o + wd] for o, wd in main_segments], axis=1).astype(BF16)
        small = jnp.zeros((d, LANES), w.dtype)
        small = small.at[:, COL_ML_I:COL_ML_I + ML_HEADS].set(w[:, o_mli:o_mli + ML_HEADS])
        small = small.at[:, COL_ML_F:COL_ML_F + ML_HEADS].set(w[:, o_mlf:o_mlf + ML_HEADS])
        small = small.at[:, COL_DT:COL_DT + SSM_HEADS].set(w[:, o_sdt:o_sdt + SSM_HEADS])
        small = small.at[:, COL_FOX_F:COL_FOX_F + FOX_HEADS].set(w[:, o_ff:o_ff + FOX_HEADS])
        w_small = small.astype(BF16)
        gate_bias = _gate_bias_row(ml_b_i[i], ml_b_f[i], ssm_dt_bias[i], fox_b_f[i])

        zm, zs = _inproj(x2, _row(norm_mix_g[i]), w_main, w_small)

        y_ml = _mlstm(zm, zs, _pad_rows(ml_conv_w[i].astype(F32), 8), _row(ml_conv_b[i]), gate_bias,
                      _row(ml_norm_g[i]), bsz=bsz, seq=seq)

        scw = ssm_conv_w[i].astype(F32)
        scb = ssm_conv_b[i].astype(F32)
        alog_row = jnp.zeros((LANES,), F32).at[COL_DT:COL_DT + SSM_HEADS].set(
            ssm_a_log[i].astype(F32)).reshape(1, LANES)
        d_row = jnp.repeat(ssm_d[i].astype(F32), SSM_HEADDIM).reshape(1, ssm_w)
        y_ssm = _ssd(zm, zs, _pad_rows(scw[:, :ssm_w], 8), _row(scb[:ssm_w]),
                     _pad_rows(scw[:, ssm_w:], 8), _row(scb[ssm_w:]), gate_bias, alog_row, d_row,
                     _row(ssm_norm_g[i]), bsz=bsz, seq=seq, col_z=c_sz, col_x=c_sx, col_bc=c_bc)

        qn, kn, ft = _fox_prep(zm, zs, gate_bias, _row(fox_q_norm_g[i]), _row(fox_k_norm_g[i]),
                               bsz=bsz, seq=seq, col_q=c_fq, col_k=c_fk)
        y_fox = _fox_attn(qn, kn, zm, ft, bsz=bsz, seq=seq, col_v=c_fv)

        x2 = _merge(x2, y_ml, y_ssm, y_fox, zm, w_branch_ml[i].astype(BF16), w_branch_ssm[i].astype(BF16),
                    w_branch_fox[i].astype(BF16), w_out[i].astype(BF16), col_g=c_gml)

        x2 = _peer(x2, _row(norm_ffn_g[i]), peer_w_q[i].astype(BF16), peer_keys1[i].astype(BF16),
                   peer_keys2[i].astype(BF16), peer_u[i].astype(BF16), peer_v[i].T.astype(BF16))

        x2 = _ple(x2, p[i].reshape(t, -1), _row(norm_ple_g[i]), ple_w_gate[i].astype(BF16),
                  ple_w_proj[i].astype(BF16), _row(final_norm_g), final=(i == depth - 1))
    return x2.reshape(bsz, seq, d)
```

```python
import functools
import math

import jax
import jax.numpy as jnp
from jax import lax
from jax.experimental import pallas as pl
from jax.experimental.pallas import tpu as pltpu

F32 = jnp.float32
BF16 = jnp.bfloat16
EPS = 1e-6
LOG2E = math.log2(math.e)

CONV_WIDTH = 4
ML_HEADS = 8
ML_DQK = 64
ML_DV = 128
SSM_HEADS = 16
SSM_HEADDIM = 64
SSM_GROUPS = 2
SSM_STATE = 128
FOX_HEADS = 8
FOX_HEADDIM = 128
PEER_HEADS = 8
PEER_NKEYS = 128
PEER_QDIM = 256
PEER_TOPK = 16

LANES = 128
HALO = 8

COL_ML_I = 0
COL_ML_F = 8
COL_DT = 16
COL_FOX_F = 32

VMEM_LIMIT = 56 * 1024 * 1024


def _cparams(sem, flags=None):
    return pltpu.CompilerParams(dimension_semantics=sem, vmem_limit_bytes=VMEM_LIMIT, flags=flags)


def _log_sigmoid(v):
    return jnp.minimum(v, 0.0) - jnp.log1p(jnp.exp(-jnp.abs(v)))


def _sigmoid(v):
    return 1.0 / (1.0 + jnp.exp(-v))


def _silu(v):
    return v * _sigmoid(v)


def _softplus(v):
    return jnp.maximum(v, 0.0) + jnp.log1p(jnp.exp(-jnp.abs(v)))


def _tril(n):
    r = lax.broadcasted_iota(jnp.int32, (n, n), 0)
    c = lax.broadcasted_iota(jnp.int32, (n, n), 1)
    return r >= c


def _cumsum_rows(v):
    n = v.shape[0]
    return jnp.dot(_tril(n).astype(F32), v, precision=lax.Precision.HIGHEST,
                   preferred_element_type=F32)


def _inproj_kernel(x_ref, g_ref, wm_ref, ws_ref, zm_ref, zs_ref, h_scr):
    @pl.when(pl.program_id(1) == 0)
    def _():
        x = x_ref[...]
        ms = jnp.mean(x * x, axis=-1, keepdims=True)
        h = (x * lax.rsqrt(ms + EPS) * g_ref[...]).astype(BF16)
        h_scr[...] = h
        zs_ref[...] = jnp.dot(h, ws_ref[...], preferred_element_type=F32)

    zm_ref[...] = jnp.dot(h_scr[...], wm_ref[...], preferred_element_type=F32).astype(zm_ref.dtype)


def _inproj(x2, g, w_main, w_small, tm=2048, tn=512):
    t, d = x2.shape
    tm = min(tm, t)
    n = w_main.shape[1]
    return pl.pallas_call(
        _inproj_kernel,
        grid=(t // tm, n // tn),
        in_specs=[
            pl.BlockSpec((tm, d), lambda i, j: (i, 0)),
            pl.BlockSpec((1, d), lambda i, j: (0, 0)),
            pl.BlockSpec((d, tn), lambda i, j: (0, j)),
            pl.BlockSpec((d, LANES), lambda i, j: (0, 0)),
        ],
        out_specs=[
            pl.BlockSpec((tm, tn), lambda i, j: (i, j)),
            pl.BlockSpec((tm, LANES), lambda i, j: (i, 0)),
        ],
        out_shape=[jax.ShapeDtypeStruct((t, n), BF16), jax.ShapeDtypeStruct((t, LANES), F32)],
        scratch_shapes=[pltpu.VMEM((tm, d), BF16)],
        compiler_params=_cparams(("parallel", "arbitrary")),
        name="inproj",
    )(x2, g, w_main, w_small)


def _conv_chunk(buf_ref, w_ref, b_ref, start, length):
    acc = None
    for j in range(CONV_WIDTH):
        off = HALO - (CONV_WIDTH - 1) + j + start
        term = buf_ref[off:off + length, :] * w_ref[j:j + 1, :]
        acc = term if acc is None else acc + term
    return acc + b_ref[...]


def _mlstm_kernel(qk_ref, v_ref, o_ref, gs_ref, cw_ref, cb_ref, gb_ref, ng_ref, y_ref,
                  buf, c_scr, m_scr, *, ts, chunk):
    nh, dk, dv = ML_HEADS, ML_DQK, ML_DV
    qkw = nh * dk

    @pl.when(pl.program_id(1) == 0)
    def _():
        buf[0:HALO, :] = jnp.zeros((HALO, buf.shape[1]), F32)
        c_scr[...] = jnp.zeros(c_scr.shape, F32)
        m_scr[...] = jnp.zeros(m_scr.shape, F32)

    buf[HALO:HALO + ts, :] = qk_ref[...].astype(F32)
    causal = _tril(chunk)
    lane = lax.broadcasted_iota(jnp.int32, (chunk, LANES), 1)
    ones_col = (lane == 0).astype(BF16)

    for c in range(ts // chunk):
        r0 = c * chunk
        act = _silu(_conv_chunk(buf, cw_ref, cb_ref, r0, chunk))
        q_all = (act[:, :qkw] * (dk ** -0.5)).astype(BF16)
        kt_all = act[:, qkw:].T
        gates = gs_ref[r0:r0 + chunk, :] + gb_ref[...]
        bcum = _cumsum_rows(_log_sigmoid(gates))
        gates_t = gates.T
        bcum_t = bcum.T
        for h in range(nh):
            i_col = gates[:, COL_ML_I + h:COL_ML_I + h + 1]
            i_row = gates_t[COL_ML_I + h:COL_ML_I + h + 1, :]
            b_col = bcum[:, COL_ML_F + h:COL_ML_F + h + 1]
            b_row = bcum_t[COL_ML_F + h:COL_ML_F + h + 1, :]
            g_tot = b_col[chunk - 1:chunk, :]
            m_prev = m_scr[h:h + 1, 0:1]
            dmat = jnp.where(causal, b_col - b_row + i_row, -jnp.inf)
            inter = b_col + m_prev
            m_t = jnp.maximum(inter, jnp.max(dmat, axis=-1, keepdims=True))
            qh = q_all[:, h * dk:(h + 1) * dk]
            kt = kt_all[h * dk:(h + 1) * dk, :]
            s = jnp.dot(qh, kt.astype(BF16), preferred_element_type=F32)
            sc = (s * jnp.exp(dmat - m_t)).astype(BF16)
            w_inter = jnp.exp(inter - m_t)
            vaug = jnp.concatenate([v_ref[r0:r0 + chunk, h * dv:(h + 1) * dv], ones_col], axis=1)
            cmem = c_scr[h]
            tot = (jnp.dot(sc, vaug, preferred_element_type=F32)
                   + w_inter * jnp.dot(qh, cmem.astype(BF16), preferred_element_type=F32))
            num = tot[:, :dv]
            den = tot[:, dv:dv + 1]
            hv = num / jnp.maximum(jnp.abs(den), jnp.exp(-m_t))
            hn = hv * lax.rsqrt(jnp.mean(hv * hv, axis=-1, keepdims=True) + EPS)
            hn = hn * ng_ref[:, h * dv:(h + 1) * dv]
            og = _sigmoid(o_ref[r0:r0 + chunk, h * dv:(h + 1) * dv].astype(F32))
            y_ref[r0:r0 + chunk, h * dv:(h + 1) * dv] = (og * hn).astype(y_ref.dtype)
            a_row = g_tot - b_row + i_row
            m_new = jnp.maximum(g_tot + m_prev, jnp.max(a_row, axis=-1, keepdims=True))
            wa_row = jnp.exp(a_row - m_new)
            decay = jnp.exp(g_tot + m_prev - m_new)
            kw = (kt * wa_row).astype(BF16)
            c_scr[h] = decay * cmem + jnp.dot(kw, vaug, preferred_element_type=F32)
            m_scr[h:h + 1, :] = jnp.broadcast_to(m_new, (1, LANES))

    buf[0:HALO, :] = buf[ts:ts + HALO, :]


def _mlstm(zm, zs, conv_w, conv_b, gate_bias, norm_g, *, bsz, seq, ts=256, chunk=128):
    t = zm.shape[0]
    w = ML_HEADS * ML_DV
    nblk = seq // ts
    row = lambda b, s: b * nblk + s
    return pl.pallas_call(
        functools.partial(_mlstm_kernel, ts=ts, chunk=chunk),
        grid=(bsz, nblk),
        in_specs=[
            pl.BlockSpec((ts, w), lambda b, s: (row(b, s), 0)),
            pl.BlockSpec((ts, w), lambda b, s: (row(b, s), 1)),
            pl.BlockSpec((ts, w), lambda b, s: (row(b, s), 2)),
            pl.BlockSpec((ts, LANES), lambda b, s: (row(b, s), 0)),
            pl.BlockSpec(conv_w.shape, lambda b, s: (0, 0)),
            pl.BlockSpec(conv_b.shape, lambda b, s: (0, 0)),
            pl.BlockSpec(gate_bias.shape, lambda b, s: (0, 0)),
            pl.BlockSpec(norm_g.shape, lambda b, s: (0, 0)),
        ],
        out_specs=pl.BlockSpec((ts, w), lambda b, s: (row(b, s), 0)),
        out_shape=jax.ShapeDtypeStruct((t, w), BF16),
        scratch_shapes=[
            pltpu.VMEM((HALO + ts, w), F32),
            pltpu.VMEM((ML_HEADS, ML_DQK, 2 * ML_DV), F32),
            pltpu.VMEM((ML_HEADS, LANES), F32),
        ],
        compiler_params=_cparams(("parallel", "arbitrary")),
        name="mlstm",
    )(zm, zm, zm, zs, conv_w, conv_b, gate_bias, norm_g)


def _ssd_kernel(z_ref, x_ref, bc_ref, gs_ref, cwx_ref, cbx_ref, cwbc_ref, cbbc_ref, gb_ref,
                alog_ref, d_ref, ng_ref, y_ref, bufx, bufbc, st_scr, *, ts, chunk):
    ng, nst, p = SSM_GROUPS, SSM_STATE, SSM_HEADDIM
    hg = SSM_HEADS // ng
    gw = hg * p
    pairs_per_group = gw // LANES

    @pl.when(pl.program_id(1) == 0)
    def _():
        bufx[0:HALO, :] = jnp.zeros((HALO, bufx.shape[1]), F32)
        bufbc[0:HALO, :] = jnp.zeros((HALO, bufbc.shape[1]), F32)
        st_scr[...] = jnp.zeros(st_scr.shape, F32)

    bufx[HALO:HALO + ts, :] = x_ref[...].astype(F32)
    bufbc[HALO:HALO + ts, :] = bc_ref[...].astype(F32)
    causal = _tril(chunk)
    low_half = lax.broadcasted_iota(jnp.int32, (chunk, LANES), 1) < p
    a_row_all = -jnp.exp(alog_ref[...])

    for c in range(ts // chunk):
        r0 = c * chunk
        xa = _silu(_conv_chunk(bufx, cwx_ref, cbx_ref, r0, chunk))
        bca = _silu(_conv_chunk(bufbc, cwbc_ref, cbbc_ref, r0, chunk))
        dt = _softplus(gs_ref[r0:r0 + chunk, :] + gb_ref[...])
        acum = _cumsum_rows(dt * a_row_all)
        acum_t = acum.T
        y_parts = []
        for g in range(ng):
            bg = bca[:, g * nst:(g + 1) * nst]
            cg = bca[:, ng * nst + g * nst:ng * nst + (g + 1) * nst].astype(BF16)
            bg_t = bg.T
            cbt = jnp.dot(cg, bg_t.astype(BF16), preferred_element_type=F32)
            state = st_scr[g]
            inter = jnp.dot(cg, state.astype(BF16), preferred_element_type=F32)
            xs_scaled, last_parts = [], []
            for pp in range(pairs_per_group):
                pidx = g * pairs_per_group + pp
                ha = 2 * pidx
                ca, cb = COL_DT + ha, COL_DT + ha + 1
                sl = slice(pidx * LANES, (pidx + 1) * LANES)
                xa_p = xa[:, sl]
                dt_pair = jnp.where(low_half, dt[:, ca:ca + 1], dt[:, cb:cb + 1])
                ac_pair = jnp.where(low_half, acum[:, ca:ca + 1], acum[:, cb:cb + 1])
                xs_p = xa_p * dt_pair
                xs_b = xs_p.astype(BF16)
                lm_a = jnp.exp(jnp.where(causal, acum[:, ca:ca + 1] - acum_t[ca:ca + 1, :], -jnp.inf))
                lm_b = jnp.exp(jnp.where(causal, acum[:, cb:cb + 1] - acum_t[cb:cb + 1, :], -jnp.inf))
                ya = jnp.dot((cbt * lm_a).astype(BF16), xs_b, preferred_element_type=F32)
                yb = jnp.dot((cbt * lm_b).astype(BF16), xs_b, preferred_element_type=F32)
                y_p = (jnp.where(low_half, ya, yb)
                       + inter[:, pp * LANES:(pp + 1) * LANES] * jnp.exp(ac_pair)
                       + d_ref[:, sl] * xa_p)
                y_parts.append(y_p)
                last = ac_pair[chunk - 1:chunk, :]
                xs_scaled.append((xs_p * jnp.exp(last - ac_pair)).astype(BF16))
                last_parts.append(last)
            xs_g = jnp.concatenate(xs_scaled, axis=1)
            last_g = jnp.concatenate(last_parts, axis=1)
            st_scr[g] = jnp.exp(last_g) * state + jnp.dot(bg_t.astype(BF16), xs_g,
                                                          preferred_element_type=F32)
        ys = jnp.concatenate(y_parts, axis=1)
        ys = ys * _silu(z_ref[r0:r0 + chunk, :].astype(F32))
        outs = []
        for g in range(ng):
            yg = ys[:, g * gw:(g + 1) * gw]
            yn = yg * lax.rsqrt(jnp.mean(yg * yg, axis=-1, keepdims=True) + EPS)
            outs.append(yn * ng_ref[:, g * gw:(g + 1) * gw])
        y_ref[r0:r0 + chunk, :] = jnp.concatenate(outs, axis=1).astype(y_ref.dtype)

    bufx[0:HALO, :] = bufx[ts:ts + HALO, :]
    bufbc[0:HALO, :] = bufbc[ts:ts + HALO, :]


def _ssd(zm, zs, cwx, cbx, cwbc, cbbc, gate_bias, alog_row, d_row, norm_g, *, bsz, seq,
         col_z, col_x, col_bc, ts=256, chunk=128):
    t = zm.shape[0]
    w = SSM_HEADS * SSM_HEADDIM
    bcw = 2 * SSM_GROUPS * SSM_STATE
    nblk = seq // ts
    row = lambda b, s: b * nblk + s
    full = lambda a: pl.BlockSpec(a.shape, lambda b, s: (0, 0))
    return pl.pallas_call(
        functools.partial(_ssd_kernel, ts=ts, chunk=chunk),
        grid=(bsz, nblk),
        in_specs=[
            pl.BlockSpec((ts, w), lambda b, s: (row(b, s), col_z // w)),
            pl.BlockSpec((ts, w), lambda b, s: (row(b, s), col_x // w)),
            pl.BlockSpec((ts, bcw), lambda b, s: (row(b, s), col_bc // bcw)),
            pl.BlockSpec((ts, LANES), lambda b, s: (row(b, s), 0)),
            full(cwx), full(cbx), full(cwbc), full(cbbc), full(gate_bias),
            full(alog_row), full(d_row), full(norm_g),
        ],
        out_specs=pl.BlockSpec((ts, w), lambda b, s: (row(b, s), 0)),
        out_shape=jax.ShapeDtypeStruct((t, w), BF16),
        scratch_shapes=[
            pltpu.VMEM((HALO + ts, w), F32),
            pltpu.VMEM((HALO + ts, bcw), F32),
            pltpu.VMEM((SSM_GROUPS, SSM_STATE, w // SSM_GROUPS), F32),
        ],
        compiler_params=_cparams(("parallel", "arbitrary")),
        name="ssd",
    )(zm, zm, zm, zs, cwx, cbx, cwbc, cbbc, gate_bias, alog_row, d_row, norm_g)


def _fox_prep_kernel(q_ref, k_ref, gs_ref, gb_ref, gq_ref, gk_ref, qn_ref, kn_ref, ft_ref,
                     carry, *, ts):
    d = FOX_HEADDIM

    @pl.when(pl.program_id(1) == 0)
    def _():
        carry[...] = jnp.zeros(carry.shape, F32)

    for h in range(FOX_HEADS):
        sl = slice(h * d, (h + 1) * d)
        qh = q_ref[:, sl].astype(F32)
        qn = qh * lax.rsqrt(jnp.mean(qh * qh, axis=-1, keepdims=True) + EPS) * gq_ref[...]
        qn_ref[:, sl] = qn.astype(qn_ref.dtype)
        kh = k_ref[:, sl].astype(F32)
        kn = kh * lax.rsqrt(jnp.mean(kh * kh, axis=-1, keepdims=True) + EPS) * gk_ref[...]
        kn_ref[:, sl] = (kn * (d ** -0.5 * LOG2E)).astype(kn_ref.dtype)

    lf = _log_sigmoid(gs_ref[...] + gb_ref[...])
    fc = _cumsum_rows(lf) + carry[0:1, :]
    carry[0:1, :] = fc[ts - 1:ts, :]
    ft_ref[...] = fc.T[COL_FOX_F:COL_FOX_F + FOX_HEADS, :] * LOG2E


def _fox_prep(zm, zs, gate_bias, gq, gk, *, bsz, seq, col_q, col_k, ts=256):
    t = zm.shape[0]
    w = FOX_HEADS * FOX_HEADDIM
    nblk = seq // ts
    row = lambda b, s: b * nblk + s
    full = lambda a: pl.BlockSpec(a.shape, lambda b, s: (0, 0))
    return pl.pallas_call(
        functools.partial(_fox_prep_kernel, ts=ts),
        grid=(bsz, nblk),
        in_specs=[
            pl.BlockSpec((ts, w), lambda b, s: (row(b, s), col_q // w)),
            pl.BlockSpec((ts, w), lambda b, s: (row(b, s), col_k // w)),
            pl.BlockSpec((ts, LANES), lambda b, s: (row(b, s), 0)),
            full(gate_bias), full(gq), full(gk),
        ],
        out_specs=[
            pl.BlockSpec((ts, w), lambda b, s: (row(b, s), 0)),
            pl.BlockSpec((ts, w), lambda b, s: (row(b, s), 0)),
            pl.BlockSpec((None, FOX_HEADS, ts), lambda b, s: (b, 0, s)),
        ],
        out_shape=[jax.ShapeDtypeStruct((t, w), BF16), jax.ShapeDtypeStruct((t, w), BF16),
                   jax.ShapeDtypeStruct((bsz, FOX_HEADS, seq), F32)],
        scratch_shapes=[pltpu.VMEM((8, LANES), F32)],
        compiler_params=_cparams(("parallel", "arbitrary")),
        name="fox_prep",
    )(zm, zm, zs, gate_bias, gq, gk)


def _fox_attn_kernel(qi_ref, kj_ref, q_ref, k_ref, v_ref, f_ref, o_ref, m_scr, acc_scr, vaug_scr,
                     *, tq, tk, rb):
    pr = pl.program_id(2)
    qi = qi_ref[pr]
    kj = kj_ref[pr]
    h = pl.program_id(1)
    d = FOX_HEADDIM

    @pl.when(kj == 0)
    def _():
        m_scr[...] = jnp.full(m_scr.shape, -jnp.inf, F32)
        acc_scr[...] = jnp.zeros(acc_scr.shape, F32)

    vaug_scr[:, :d] = v_ref[...]
    vaug_scr[:, d:] = (lax.broadcasted_iota(jnp.int32, (tk, d), 1) == 0).astype(BF16)
    fk_all = f_ref[pl.ds(h, 1), :]

    def body(masked):
        for r in range(tq // rb):
            rows = slice(r * rb, (r + 1) * rb)
            ncol = (r + 1) * rb if masked else tk
            q = q_ref[rows, :]
            s = lax.dot_general(q, k_ref[:ncol, :], (((1,), (1,)), ((), ())),
                                preferred_element_type=F32) - fk_all[:, :ncol]
            if masked:
                rr = lax.broadcasted_iota(jnp.int32, (rb, ncol), 0) + r * rb
                cc = lax.broadcasted_iota(jnp.int32, (rb, ncol), 1)
                s = jnp.where(cc <= rr, s, -jnp.inf)
            m_old = m_scr[rows, :]
            m_new = jnp.maximum(m_old, jnp.max(s, axis=-1, keepdims=True))
            alpha = jnp.exp2(m_old - m_new)
            p = jnp.exp2(s - m_new[:, :1]).astype(BF16)
            pv = jnp.dot(p, vaug_scr[:ncol, :], preferred_element_type=F32)
            acc_scr[rows, :] = jnp.concatenate([alpha, alpha], axis=1) * acc_scr[rows, :] + pv
            m_scr[rows, :] = m_new

    @pl.when(kj < qi)
    def _():
        body(False)

    @pl.when(kj == qi)
    def _():
        body(True)
        acc = acc_scr[...]
        o_ref[...] = (acc[:, :d] / acc[:, d:d + 1]).astype(o_ref.dtype)


def _fox_attn(qn, kn, zm, ft, *, bsz, seq, col_v, tq=1024, rb=256):
    t = qn.shape[0]
    d = FOX_HEADDIM
    tq = min(tq, seq)
    rb = min(rb, tq)
    tk = tq
    nq = seq // tq
    qi_tab, kj_tab = [], []
    for i in range(nq):
        for j in range(i + 1):
            qi_tab.append(i)
            kj_tab.append(j)
    qi_tab = jnp.asarray(qi_tab, jnp.int32)
    kj_tab = jnp.asarray(kj_tab, jnp.int32)
    vblk = col_v // d
    grid_spec = pltpu.PrefetchScalarGridSpec(
        num_scalar_prefetch=2,
        grid=(bsz, FOX_HEADS, qi_tab.shape[0]),
        in_specs=[
            pl.BlockSpec((tq, d), lambda b, h, p, qi, kj: (b * nq + qi[p], h)),
            pl.BlockSpec((tk, d), lambda b, h, p, qi, kj: (b * nq + kj[p], h)),
            pl.BlockSpec((tk, d), lambda b, h, p, qi, kj: (b * nq + kj[p], vblk + h)),
            pl.BlockSpec((None, FOX_HEADS, tk), lambda b, h, p, qi, kj: (b, 0, kj[p])),
        ],
        out_specs=pl.BlockSpec((tq, d), lambda b, h, p, qi, kj: (b * nq + qi[p], h)),
        scratch_shapes=[pltpu.VMEM((tq, LANES), F32), pltpu.VMEM((tq, 2 * d), F32),
                        pltpu.VMEM((tk, 2 * d), BF16)],
    )
    return pl.pallas_call(
        functools.partial(_fox_attn_kernel, tq=tq, tk=tk, rb=rb),
        grid_spec=grid_spec,
        out_shape=jax.ShapeDtypeStruct((t, FOX_HEADS * d), BF16),
        compiler_params=_cparams(("parallel", "parallel", "arbitrary")),
        name="fox_attn",
    )(qi_tab, kj_tab, qn, kn, zm, ft)


def _merge_kernel(x_ref, yml_ref, yssm_ref, yfox_ref, gml_ref, gssm_ref, gfox_ref,
                  pml_ref, pssm_ref, pfox_ref, wout_ref, o_ref):
    def branch(y_ref, g_ref, p_ref):
        proj = jnp.dot(y_ref[...], p_ref[...], preferred_element_type=F32)
        return _sigmoid(g_ref[...].astype(F32)) * proj

    merged = (branch(yml_ref, gml_ref, pml_ref) + branch(yssm_ref, gssm_ref, pssm_ref)
              + branch(yfox_ref, gfox_ref, pfox_ref))
    o_ref[...] = x_ref[...] + jnp.dot(merged.astype(BF16), wout_ref[...], preferred_element_type=F32)


def _merge(x2, y_ml, y_ssm, y_fox, zm, p_ml, p_ssm, p_fox, w_out, *, col_g, tm=512):
    t, d = x2.shape
    rowblk = lambda c: pl.BlockSpec((tm, d), lambda i: (i, c))
    wfull = lambda a: pl.BlockSpec(a.shape, lambda i: (0, 0))
    gblk = col_g // d
    return pl.pallas_call(
        _merge_kernel,
        grid=(t // tm,),
        in_specs=[rowblk(0), rowblk(0), rowblk(0), rowblk(0),
                  rowblk(gblk), rowblk(gblk + 1), rowblk(gblk + 2),
                  wfull(p_ml), wfull(p_ssm), wfull(p_fox), wfull(w_out)],
        out_specs=rowblk(0),
        out_shape=jax.ShapeDtypeStruct((t, d), F32),
        compiler_params=_cparams(("parallel",)),
        name="merge",
    )(x2, y_ml, y_ssm, y_fox, zm, zm, zm, p_ml, p_ssm, p_fox, w_out)


def _erf(v):
    return lax.erf(v)


def _gelu(v):
    return 0.5 * v * (1.0 + _erf(v * (2.0 ** -0.5)))


def _top_values(scores, k, ranked=False):
    vals = []
    cur = scores
    rank = jnp.full(scores.shape, float(k), F32) if ranked else None
    for it in range(k):
        m = jnp.max(cur, axis=0, keepdims=True)
        vals.append(m)
        hit = cur == m
        if ranked:
            rank = jnp.where(hit, float(it), rank)
        if it + 1 < k:
            cur = jnp.where(hit, -jnp.inf, cur)
    return (vals, rank) if ranked else vals


def _candidate_sums(v1, v2, k):
    sub = 8
    v1m = jnp.concatenate(v1, axis=0)
    v2m = jnp.concatenate(v2, axis=0)
    row = lax.broadcasted_iota(jnp.int32, (sub, v1m.shape[1]), 0)
    groups = [v1[0] + v2m]
    a = 1
    while k // (a + 1) > 1:
        nb = k // (a + 1)
        assert nb <= sub
        grp = v1[a] + v2m[:sub]
        groups.append(grp if nb == sub else jnp.where(row < nb, grp, -jnp.inf))
        a += 1
    assert a % sub == 0
    groups.append(v1m[a:] + v2[0])
    return jnp.concatenate(groups, axis=0)


def _peer_kernel(x_ref, g_ref, wq_ref, k1_ref, k2_ref, u_ref, vt_ref, o_ref,
                 ht_scr, q_scr, cnt_scr, e1_scr, rank_scr, e2_scr, acc_scr, g0_scr, g1_scr,
                 w0_scr, w1_scr, *, tb, eb):
    j = pl.program_id(1)
    nk, topk, nhead = PEER_NKEYS, PEER_TOPK, PEER_HEADS
    half = PEER_QDIM // 2
    n_lc = tb // LANES
    pk = 16

    @pl.when(j == 0)
    def _():
        x = x_ref[...]
        hn32 = x * lax.rsqrt(jnp.mean(x * x, axis=-1, keepdims=True) + EPS) * g_ref[...]
        ht_scr[...] = hn32.T.astype(BF16)
        q = jnp.dot(hn32.astype(BF16), wq_ref[...], preferred_element_type=F32).astype(BF16)
        for c in range(q.shape[1] // half):
            q_scr[c] = q[:, c * half:(c + 1) * half]
        acc_scr[...] = jnp.zeros(acc_scr.shape, F32)
        g0_scr[...] = jnp.zeros(g0_scr.shape, BF16)
        w1_scr[...] = jnp.zeros(w1_scr.shape, BF16)

        def head_body(h, carry):
            nt = (((1,), (1,)), ((), ()))
            s1 = lax.dot_general(k1_ref[h], q_scr[2 * h], nt, preferred_element_type=F32)
            s2 = lax.dot_general(k2_ref[h], q_scr[2 * h + 1], nt, preferred_element_type=F32)
            for lc in range(n_lc):
                ls = slice(lc * LANES, (lc + 1) * LANES)
                s1c = s1[:, ls]
                s2c = s2[:, ls]
                v1 = _top_values(s1c, topk)
                v2, rank2 = _top_values(s2c, topk, ranked=True)
                tv = _top_values(_candidate_sums(v1, v2, topk), topk)
                tau, mx = tv[topk - 1], tv[0]
                zsum = jnp.zeros_like(mx)
                for a in range(topk):
                    zsum = zsum + jnp.exp(tv[a] - mx)
                v1m = jnp.concatenate(v1, axis=0)
                npart = jnp.zeros_like(v1m)
                for b in range(topk):
                    npart = npart + jnp.where(v1m + v2[b] >= tau, 1.0, 0.0)
                cnt = jnp.zeros_like(s1c)
                for a in range(topk):
                    cnt = jnp.where(s1c == v1[a], npart[a:a + 1, :], cnt)
                cnt_scr[h, lc] = cnt
                e1_scr[h, lc] = jnp.exp(s1c - v1[0]) / zsum
                rank_scr[h, lc] = rank2.astype(BF16).reshape(nk // pk, pk, LANES)
                e2_scr[h, lc] = jnp.exp(s2c - v2[0]).astype(BF16).reshape(nk // pk, pk, LANES)
            return carry

        lax.fori_loop(0, nhead, head_body, 0)

    n_r = eb // nk
    n_e = pl.num_programs(1) - 2
    ja = jnp.minimum(j, n_e - 1)

    def stage_a(w_wr, lc, r_range):
        for r in r_range:
            i1 = ja * n_r + r
            wsum = None
            for h in range(nhead):
                cnt = jnp.broadcast_to(cnt_scr[h, lc, pl.ds(i1, 1), :], (pk, LANES)).astype(BF16)
                e1 = jnp.broadcast_to(e1_scr[h, lc, pl.ds(i1, 1), :], (pk, LANES)).astype(BF16)
                term = jnp.where(rank_scr[h, lc] < cnt[None], e2_scr[h, lc] * e1[None],
                                 jnp.zeros((), BF16))
                wsum = term if wsum is None else wsum + term
            w_wr[lc, r * (nk // pk):(r + 1) * (nk // pk)] = wsum

    def stage_b(w_rd, g_wr, n, m):
        ts_ = slice(n * 2 * LANES, (n + 1) * 2 * LANES)
        es = slice(m * (eb // 2), (m + 1) * (eb // 2))
        ps = slice(m * (eb // 2 // pk), (m + 1) * (eb // 2 // pk))
        act = _gelu(jnp.dot(u_ref[es, :], ht_scr[:, ts_], preferred_element_type=F32))
        w_prev = jnp.concatenate([w_rd[2 * n + c, ps].reshape(eb // 2, LANES) for c in range(2)], axis=1)
        g_wr[es, ts_] = (act * w_prev.astype(F32)).astype(BF16)

    def stage_c(g_rd, n, m):
        ts_ = slice(n * 2 * LANES, (n + 1) * 2 * LANES)
        ds_ = slice(m * (acc_scr.shape[0] // 2), (m + 1) * (acc_scr.shape[0] // 2))
        acc_scr[ds_, ts_] += jnp.dot(vt_ref[ds_, :], g_rd[:, ts_], preferred_element_type=F32)

    def step(w_wr, w_rd, g_wr, g_rd):
        for n in range(n_lc // 2):
            for m in range(2):
                stage_a(w_wr, 2 * n + m, range(0, n_r // 2))
                stage_b(w_rd, g_wr, n, m)
                stage_a(w_wr, 2 * n + m, range(n_r // 2, n_r))
                stage_c(g_rd, n, m)

    @pl.when(j % 2 == 0)
    def _():
        step(w0_scr, w1_scr, g1_scr, g0_scr)

    @pl.when(j % 2 == 1)
    def _():
        step(w1_scr, w0_scr, g0_scr, g1_scr)

    @pl.when(j == n_e + 1)
    def _():
        o_ref[...] = x_ref[...] + acc_scr[...].T


def _peer(x2, g, w_q, keys1, keys2, u, v_t, *, tb=512, eb=512):
    t, d = x2.shape
    ne = u.shape[0]
    nq = w_q.shape[1]
    n_e = ne // eb
    n_lc = tb // LANES
    pk = 16
    return pl.pallas_call(
        functools.partial(_peer_kernel, tb=tb, eb=eb),
        grid=(t // tb, n_e + 2),
        in_specs=[
            pl.BlockSpec((tb, d), lambda i, j: (i, 0)),
            pl.BlockSpec((1, d), lambda i, j: (0, 0)),
            pl.BlockSpec((d, nq), lambda i, j: (0, 0)),
            pl.BlockSpec(keys1.shape, lambda i, j: (0, 0, 0)),
            pl.BlockSpec(keys2.shape, lambda i, j: (0, 0, 0)),
            pl.BlockSpec((eb, d), lambda i, j: (jnp.clip(j - 1, 0, n_e - 1), 0)),
            pl.BlockSpec((d, eb), lambda i, j: (0, jnp.clip(j - 2, 0, n_e - 1))),
        ],
        out_specs=pl.BlockSpec((tb, d), lambda i, j: (i, 0)),
        out_shape=jax.ShapeDtypeStruct((t, d), F32),
        scratch_shapes=[
            pltpu.VMEM((d, tb), BF16),
            pltpu.VMEM((2 * PEER_HEADS, tb, PEER_QDIM // 2), BF16),
            pltpu.VMEM((PEER_HEADS, n_lc, PEER_NKEYS, LANES), F32),
            pltpu.VMEM((PEER_HEADS, n_lc, PEER_NKEYS, LANES), F32),
            pltpu.VMEM((PEER_HEADS, n_lc, PEER_NKEYS // pk, pk, LANES), BF16),
            pltpu.VMEM((PEER_HEADS, n_lc, PEER_NKEYS // pk, pk, LANES), BF16),
            pltpu.VMEM((d, tb), F32),
            pltpu.VMEM((eb, tb), BF16),
            pltpu.VMEM((eb, tb), BF16),
            pltpu.VMEM((n_lc, eb // pk, pk, LANES), BF16),
            pltpu.VMEM((n_lc, eb // pk, pk, LANES), BF16),
        ],
        compiler_params=_cparams(("parallel", "arbitrary")),
        name="peer",
    )(x2, g, w_q, keys1, keys2, u, v_t)


def _ple_kernel(x_ref, p_ref, g_ref, wg_ref, wp_ref, fg_ref, o_ref, *, final):
    x = x_ref[...]
    hn = (x * lax.rsqrt(jnp.mean(x * x, axis=-1, keepdims=True) + EPS) * g_ref[...]).astype(BF16)
    gate = _sigmoid(jnp.dot(hn, wg_ref[...], preferred_element_type=F32))
    proj = jnp.dot(p_ref[...].astype(BF16), wp_ref[...], preferred_element_type=F32)
    y = x + gate * proj
    if final:
        y = y * lax.rsqrt(jnp.mean(y * y, axis=-1, keepdims=True) + EPS) * fg_ref[...]
    o_ref[...] = y


def _ple(x2, p2, g, w_gate, w_proj, final_g, *, final, tm=512):
    t, d = x2.shape
    pd = p2.shape[1]
    full = lambda a: pl.BlockSpec(a.shape, lambda i: (0, 0))
    return pl.pallas_call(
        functools.partial(_ple_kernel, final=final),
        grid=(t // tm,),
        in_specs=[pl.BlockSpec((tm, d), lambda i: (i, 0)), pl.BlockSpec((tm, pd), lambda i: (i, 0)),
                  full(g), full(w_gate), full(w_proj), full(final_g)],
        out_specs=pl.BlockSpec((tm, d), lambda i: (i, 0)),
        out_shape=jax.ShapeDtypeStruct((t, d), F32),
        compiler_params=_cparams(("parallel",)),
        name="ple",
    )(x2, p2, g, w_gate, w_proj, final_g)


def _row(v):
    return v.reshape(1, -1).astype(F32)


def _gate_bias_row(ml_b_i, ml_b_f, dt_bias, fox_b_f):
    r = jnp.zeros((LANES,), F32)
    r = r.at[COL_ML_I:COL_ML_I + ML_HEADS].set(ml_b_i.astype(F32))
    r = r.at[COL_ML_F:COL_ML_F + ML_HEADS].set(ml_b_f.astype(F32))
    r = r.at[COL_DT:COL_DT + SSM_HEADS].set(dt_bias.astype(F32))
    r = r.at[COL_FOX_F:COL_FOX_F + FOX_HEADS].set(fox_b_f.astype(F32))
    return r.reshape(1, LANES)


def _pad_rows(w, rows):
    return jnp.concatenate([w, jnp.zeros((rows - w.shape[0],) + w.shape[1:], w.dtype)], axis=0)


def kernel(x, p, norm_mix_g, w_in, ml_conv_w, ml_conv_b, ml_b_i, ml_b_f, ml_norm_g, ssm_conv_w, ssm_conv_b, ssm_dt_bias, ssm_a_log, ssm_d, ssm_norm_g, fox_q_norm_g, fox_k_norm_g, fox_b_f, w_branch_ml, w_branch_ssm, w_branch_fox, w_out, norm_ffn_g, peer_w_q, peer_keys1, peer_keys2, peer_u, peer_v, norm_ple_g, ple_w_gate, ple_w_proj, final_norm_g):
    bsz, seq, d = x.shape
    depth = w_in.shape[0]
    t = bsz * seq
    ml_qk = ML_HEADS * ML_DQK
    ml_w = ML_HEADS * ML_DV
    ssm_w = SSM_HEADS * SSM_HEADDIM
    ssm_bc = SSM_GROUPS * SSM_STATE
    fox_w = FOX_HEADS * FOX_HEADDIM
    splits = (ml_qk, ml_qk, ml_w, ml_w, ML_HEADS, ML_HEADS, ssm_w, ssm_w, ssm_bc, ssm_bc, SSM_HEADS,
              fox_w, fox_w, fox_w, FOX_HEADS, d, d, d)
    offs = [0]
    for s_ in splits:
        offs.append(offs[-1] + s_)
    (o_mlq, o_mlk, o_mlv, o_mlo, o_mli, o_mlf, o_sz, o_sx, o_sb, o_sc, o_sdt,
     o_fq, o_fk, o_fv, o_ff, o_gml, o_gssm, o_gfox) = offs[:-1]

    main_segments = [(o_mlq, 2 * ml_qk), (o_mlv, ml_w), (o_mlo, ml_w), (o_sz, ssm_w), (o_sx, ssm_w),
                     (o_fq, fox_w), (o_fk, fox_w), (o_fv, fox_w), (o_gml, d), (o_gssm, d), (o_gfox, d),
                     (o_sb, 2 * ssm_bc)]
    seg_off = [0]
    for _, wd in main_segments:
        seg_off.append(seg_off[-1] + wd)
    (c_mlqk, c_mlv, c_mlo, c_sz, c_sx, c_fq, c_fk, c_fv, c_gml, c_gssm, c_gfox, c_bc) = seg_off[:-1]
    assert (c_mlqk, c_mlv, c_mlo) == (0, ml_w, 2 * ml_w)

    x2 = x.reshape(t, d)
    for i in range(depth):
        w = w_in[i]
        w_main = jnp.concatenate([w[:, o:o + wd] for o, wd in main_segments], axis=1).astype(BF16)
        small = jnp.zeros((d, LANES), w.dtype)
        small = small.at[:, COL_ML_I:COL_ML_I + ML_HEADS].set(w[:, o_mli:o_mli + ML_HEADS])
        small = small.at[:, COL_ML_F:COL_ML_F + ML_HEADS].set(w[:, o_mlf:o_mlf + ML_HEADS])
        small = small.at[:, COL_DT:COL_DT + SSM_HEADS].set(w[:, o_sdt:o_sdt + SSM_HEADS])
        small = small.at[:, COL_FOX_F:COL_FOX_F + FOX_HEADS].set(w[:, o_ff:o_ff + FOX_HEADS])
        w_small = small.astype(BF16)
        gate_bias = _gate_bias_row(ml_b_i[i], ml_b_f[i], ssm_dt_bias[i], fox_b_f[i])

        zm, zs = _inproj(x2, _row(norm_mix_g[i]), w_main, w_small)

        y_ml = _mlstm(zm, zs, _pad_rows(ml_conv_w[i].astype(F32), 8), _row(ml_conv_b[i]), gate_bias,
                      _row(ml_norm_g[i]), bsz=bsz, seq=seq)

        scw = ssm_conv_w[i].astype(F32)
        scb = ssm_conv_b[i].astype(F32)
        alog_row = jnp.zeros((LANES,), F32).at[COL_DT:COL_DT + SSM_HEADS].set(
            ssm_a_log[i].astype(F32)).reshape(1, LANES)
        d_row = jnp.repeat(ssm_d[i].astype(F32), SSM_HEADDIM).reshape(1, ssm_w)
        y_ssm = _ssd(zm, zs, _pad_rows(scw[:, :ssm_w], 8), _row(scb[:ssm_w]),
                     _pad_rows(scw[:, ssm_w:], 8), _row(scb[ssm_w:]), gate_bias, alog_row, d_row,
                     _row(ssm_norm_g[i]), bsz=bsz, seq=seq, col_z=c_sz, col_x=c_sx, col_bc=c_bc)

        qn, kn, ft = _fox_prep(zm, zs, gate_bias, _row(fox_q_norm_g[i]), _row(fox_k_norm_g[i]),
                               bsz=bsz, seq=seq, col_q=c_fq, col_k=c_fk)
        y_fox = _fox_attn(qn, kn, zm, ft, bsz=bsz, seq=seq, col_v=c_fv)

        x2 = _merge(x2, y_ml, y_ssm, y_fox, zm, w_branch_ml[i].astype(BF16), w_branch_ssm[i].astype(BF16),
                    w_branch_fox[i].astype(BF16), w_out[i].astype(BF16), col_g=c_gml)

        x2 = _peer(x2, _row(norm_ffn_g[i]), peer_w_q[i].astype(BF16), peer_keys1[i].astype(BF16),
                   peer_keys2[i].astype(BF16), peer_u[i].astype(BF16), peer_v[i].T.astype(BF16))

        x2 = _ple(x2, p[i].reshape(t, -1), _row(norm_ple_g[i]), ple_w_gate[i].astype(BF16),
                  ple_w_proj[i].astype(BF16), _row(final_norm_g), final=(i == depth - 1))
    return x2.reshape(bsz, seq, d)
```

```python
import functools
import math

import jax
import jax.numpy as jnp
from jax import lax
from jax.experimental import pallas as pl
from jax.experimental.pallas import tpu as pltpu

F32 = jnp.float32
BF16 = jnp.bfloat16
EPS = 1e-6
LOG2E = math.log2(math.e)

CONV_WIDTH = 4
ML_HEADS = 8
ML_DQK = 64
ML_DV = 128
SSM_HEADS = 16
SSM_HEADDIM = 64
SSM_GROUPS = 2
SSM_STATE = 128
FOX_HEADS = 8
FOX_HEADDIM = 128
PEER_HEADS = 8
PEER_NKEYS = 128
PEER_QDIM = 256
PEER_TOPK = 16

LANES = 128
HALO = 8

COL_ML_I = 0
COL_ML_F = 8
COL_DT = 16
COL_FOX_F = 32

VMEM_LIMIT = 56 * 1024 * 1024


def _cparams(sem, flags=None):
    return pltpu.CompilerParams(dimension_semantics=sem, vmem_limit_bytes=VMEM_LIMIT, flags=flags)


def _log_sigmoid(v):
    return jnp.minimum(v, 0.0) - jnp.log1p(jnp.exp(-jnp.abs(v)))


def _sigmoid(v):
    return 1.0 / (1.0 + jnp.exp(-v))


def _silu(v):
    return v * _sigmoid(v)


def _softplus(v):
    return jnp.maximum(v, 0.0) + jnp.log1p(jnp.exp(-jnp.abs(v)))


def _tril(n):
    r = lax.broadcasted_iota(jnp.int32, (n, n), 0)
    c = lax.broadcasted_iota(jnp.int32, (n, n), 1)
    return r >= c


def _cumsum_rows(v):
    n = v.shape[0]
    return jnp.dot(_tril(n).astype(F32), v, precision=lax.Precision.HIGHEST,
                   preferred_element_type=F32)


def _inproj_kernel(x_ref, g_ref, wm_ref, ws_ref, zm_ref, zs_ref, h_scr):
    @pl.when(pl.program_id(1) == 0)
    def _():
        x = x_ref[...]
        ms = jnp.mean(x * x, axis=-1, keepdims=True)
        h = (x * lax.rsqrt(ms + EPS) * g_ref[...]).astype(BF16)
        h_scr[...] = h
        zs_ref[...] = jnp.dot(h, ws_ref[...], preferred_element_type=F32)

    zm_ref[...] = jnp.dot(h_scr[...], wm_ref[...], preferred_element_type=F32).astype(zm_ref.dtype)


def _inproj(x2, g, w_main, w_small, tm=2048, tn=512):
    t, d = x2.shape
    tm = min(tm, t)
    n = w_main.shape[1]
    return pl.pallas_call(
        _inproj_kernel,
        grid=(t // tm, n // tn),
        in_specs=[
            pl.BlockSpec((tm, d), lambda i, j: (i, 0)),
            pl.BlockSpec((1, d), lambda i, j: (0, 0)),
            pl.BlockSpec((d, tn), lambda i, j: (0, j)),
            pl.BlockSpec((d, LANES), lambda i, j: (0, 0)),
        ],
        out_specs=[
            pl.BlockSpec((tm, tn), lambda i, j: (i, j)),
            pl.BlockSpec((tm, LANES), lambda i, j: (i, 0)),
        ],
        out_shape=[jax.ShapeDtypeStruct((t, n), BF16), jax.ShapeDtypeStruct((t, LANES), F32)],
        scratch_shapes=[pltpu.VMEM((tm, d), BF16)],
        compiler_params=_cparams(("parallel", "arbitrary")),
        name="inproj",
    )(x2, g, w_main, w_small)


def _conv_chunk(buf_ref, w_ref, b_ref, start, length):
    acc = None
    for j in range(CONV_WIDTH):
        off = HALO - (CONV_WIDTH - 1) + j + start
        term = buf_ref[off:off + length, :] * w_ref[j:j + 1, :]
        acc = term if acc is None else acc + term
    return acc + b_ref[...]


def _mlstm_kernel(qk_ref, v_ref, o_ref, gs_ref, cw_ref, cb_ref, gb_ref, ng_ref, y_ref,
                  buf, c_scr, m_scr, *, ts, chunk):
    nh, dk, dv = ML_HEADS, ML_DQK, ML_DV
    qkw = nh * dk

    @pl.when(pl.program_id(1) == 0)
    def _():
        buf[0:HALO, :] = jnp.zeros((HALO, buf.shape[1]), F32)
        c_scr[...] = jnp.zeros(c_scr.shape, F32)
        m_scr[...] = jnp.zeros(m_scr.shape, F32)

    buf[HALO:HALO + ts, :] = qk_ref[...].astype(F32)
    causal = _tril(chunk)
    lane = lax.broadcasted_iota(jnp.int32, (chunk, LANES), 1)
    ones_col = (lane == 0).astype(BF16)

    for c in range(ts // chunk):
        r0 = c * chunk
        act = _silu(_conv_chunk(buf, cw_ref, cb_ref, r0, chunk))
        q_all = (act[:, :qkw] * (dk ** -0.5)).astype(BF16)
        kt_all = act[:, qkw:].T
        gates = gs_ref[r0:r0 + chunk, :] + gb_ref[...]
        bcum = _cumsum_rows(_log_sigmoid(gates))
        gates_t = gates.T
        bcum_t = bcum.T
        for h in range(nh):
            i_col = gates[:, COL_ML_I + h:COL_ML_I + h + 1]
            i_row = gates_t[COL_ML_I + h:COL_ML_I + h + 1, :]
            b_col = bcum[:, COL_ML_F + h:COL_ML_F + h + 1]
            b_row = bcum_t[COL_ML_F + h:COL_ML_F + h + 1, :]
            g_tot = b_col[chunk - 1:chunk, :]
            m_prev = m_scr[h:h + 1, 0:1]
            dmat = jnp.where(causal, b_col - b_row + i_row, -jnp.inf)
            inter = b_col + m_prev
            m_t = jnp.maximum(inter, jnp.max(dmat, axis=-1, keepdims=True))
            qh = q_all[:, h * dk:(h + 1) * dk]
            kt = kt_all[h * dk:(h + 1) * dk, :]
            s = jnp.dot(qh, kt.astype(BF16), preferred_element_type=F32)
            sc = (s * jnp.exp(dmat - m_t)).astype(BF16)
            w_inter = jnp.exp(inter - m_t)
            vaug = jnp.concatenate([v_ref[r0:r0 + chunk, h * dv:(h + 1) * dv], ones_col], axis=1)
            cmem = c_scr[h]
            tot = (jnp.dot(sc, vaug, preferred_element_type=F32)
                   + w_inter * jnp.dot(qh, cmem.astype(BF16), preferred_element_type=F32))
            num = tot[:, :dv]
            den = tot[:, dv:dv + 1]
            hv = num / jnp.maximum(jnp.abs(den), jnp.exp(-m_t))
            hn = hv * lax.rsqrt(jnp.mean(hv * hv, axis=-1, keepdims=True) + EPS)
            hn = hn * ng_ref[:, h * dv:(h + 1) * dv]
            og = _sigmoid(o_ref[r0:r0 + chunk, h * dv:(h + 1) * dv].astype(F32))
            y_ref[r0:r0 + chunk, h * dv:(h + 1) * dv] = (og * hn).astype(y_ref.dtype)
            a_row = g_tot - b_row + i_row
            m_new = jnp.maximum(g_tot + m_prev, jnp.max(a_row, axis=-1, keepdims=True))
            wa_row = jnp.exp(a_row - m_new)
            decay = jnp.exp(g_tot + m_prev - m_new)
            kw = (kt * wa_row).astype(BF16)
            c_scr[h] = decay * cmem + jnp.dot(kw, vaug, preferred_element_type=F32)
            m_scr[h:h + 1, :] = jnp.broadcast_to(m_new, (1, LANES))

    buf[0:HALO, :] = buf[ts:ts + HALO, :]


def _mlstm(zm, zs, conv_w, conv_b, gate_bias, norm_g, *, bsz, seq, ts=256, chunk=128):
    t = zm.shape[0]
    w = ML_HEADS * ML_DV
    nblk = seq // ts
    row = lambda b, s: b * nblk + s
    return pl.pallas_call(
        functools.partial(_mlstm_kernel, ts=ts, chunk=chunk),
        grid=(bsz, nblk),
        in_specs=[
            pl.BlockSpec((ts, w), lambda b, s: (row(b, s), 0)),
            pl.BlockSpec((ts, w), lambda b, s: (row(b, s), 1)),
            pl.BlockSpec((ts, w), lambda b, s: (row(b, s), 2)),
            pl.BlockSpec((ts, LANES), lambda b, s: (row(b, s), 0)),
            pl.BlockSpec(conv_w.shape, lambda b, s: (0, 0)),
            pl.BlockSpec(conv_b.shape, lambda b, s: (0, 0)),
            pl.BlockSpec(gate_bias.shape, lambda b, s: (0, 0)),
            pl.BlockSpec(norm_g.shape, lambda b, s: (0, 0)),
        ],
        out_specs=pl.BlockSpec((ts, w), lambda b, s: (row(b, s), 0)),
        out_shape=jax.ShapeDtypeStruct((t, w), BF16),
        scratch_shapes=[
            pltpu.VMEM((HALO + ts, w), F32),
            pltpu.VMEM((ML_HEADS, ML_DQK, 2 * ML_DV), F32),
            pltpu.VMEM((ML_HEADS, LANES), F32),
        ],
        compiler_params=_cparams(("parallel", "arbitrary")),
        name="mlstm",
    )(zm, zm, zm, zs, conv_w, conv_b, gate_bias, norm_g)


def _ssd_kernel(z_ref, x_ref, bc_ref, gs_ref, cwx_ref, cbx_ref, cwbc_ref, cbbc_ref, gb_ref,
                alog_ref, d_ref, ng_ref, y_ref, bufx, bufbc, st_scr, *, ts, chunk):
    ng, nst, p = SSM_GROUPS, SSM_STATE, SSM_HEADDIM
    hg = SSM_HEADS // ng
    gw = hg * p
    pairs_per_group = gw // LANES

    @pl.when(pl.program_id(1) == 0)
    def _():
        bufx[0:HALO, :] = jnp.zeros((HALO, bufx.shape[1]), F32)
        bufbc[0:HALO, :] = jnp.zeros((HALO, bufbc.shape[1]), F32)
        st_scr[...] = jnp.zeros(st_scr.shape, F32)

    bufx[HALO:HALO + ts, :] = x_ref[...].astype(F32)
    bufbc[HALO:HALO + ts, :] = bc_ref[...].astype(F32)
    causal = _tril(chunk)
    low_half = lax.broadcasted_iota(jnp.int32, (chunk, LANES), 1) < p
    a_row_all = -jnp.exp(alog_ref[...])

    for c in range(ts // chunk):
        r0 = c * chunk
        xa = _silu(_conv_chunk(bufx, cwx_ref, cbx_ref, r0, chunk))
        bca = _silu(_conv_chunk(bufbc, cwbc_ref, cbbc_ref, r0, chunk))
        dt = _softplus(gs_ref[r0:r0 + chunk, :] + gb_ref[...])
        acum = _cumsum_rows(dt * a_row_all)
        acum_t = acum.T
        y_parts = []
        for g in range(ng):
            bg = bca[:, g * nst:(g + 1) * nst]
            cg = bca[:, ng * nst + g * nst:ng * nst + (g + 1) * nst].astype(BF16)
            bg_t = bg.T
            cbt = jnp.dot(cg, bg_t.astype(BF16), preferred_element_type=F32)
            state = st_scr[g]
            inter = jnp.dot(cg, state.astype(BF16), preferred_element_type=F32)
            xs_scaled, last_parts = [], []
            for pp in range(pairs_per_group):
                pidx = g * pairs_per_group + pp
                ha = 2 * pidx
                ca, cb = COL_DT + ha, COL_DT + ha + 1
                sl = slice(pidx * LANES, (pidx + 1) * LANES)
                xa_p = xa[:, sl]
                dt_pair = jnp.where(low_half, dt[:, ca:ca + 1], dt[:, cb:cb + 1])
                ac_pair = jnp.where(low_half, acum[:, ca:ca + 1], acum[:, cb:cb + 1])
                xs_p = xa_p * dt_pair
                xs_b = xs_p.astype(BF16)
                lm_a = jnp.exp(jnp.where(causal, acum[:, ca:ca + 1] - acum_t[ca:ca + 1, :], -jnp.inf))
                lm_b = jnp.exp(jnp.where(causal, acum[:, cb:cb + 1] - acum_t[cb:cb + 1, :], -jnp.inf))
                ya = jnp.dot((cbt * lm_a).astype(BF16), xs_b, preferred_element_type=F32)
                yb = jnp.dot((cbt * lm_b).astype(BF16), xs_b, preferred_element_type=F32)
                y_p = (jnp.where(low_half, ya, yb)
                       + inter[:, pp * LANES:(pp + 1) * LANES] * jnp.exp(ac_pair)
                       + d_ref[:, sl] * xa_p)
                y_parts.append(y_p)
                last = ac_pair[chunk - 1:chunk, :]
                xs_scaled.append((xs_p * jnp.exp(last - ac_pair)).astype(BF16))
                last_parts.append(last)
            xs_g = jnp.concatenate(xs_scaled, axis=1)
            last_g = jnp.concatenate(last_parts, axis=1)
            st_scr[g] = jnp.exp(last_g) * state + jnp.dot(bg_t.astype(BF16), xs_g,
                                                          preferred_element_type=F32)
        ys = jnp.concatenate(y_parts, axis=1)
        ys = ys * _silu(z_ref[r0:r0 + chunk, :].astype(F32))
        outs = []
        for g in range(ng):
            yg = ys[:, g * gw:(g + 1) * gw]
            yn = yg * lax.rsqrt(jnp.mean(yg * yg, axis=-1, keepdims=True) + EPS)
            outs.append(yn * ng_ref[:, g * gw:(g + 1) * gw])
        y_ref[r0:r0 + chunk, :] = jnp.concatenate(outs, axis=1).astype(y_ref.dtype)

    bufx[0:HALO, :] = bufx[ts:ts + HALO, :]
    bufbc[0:HALO, :] = bufbc[ts:ts + HALO, :]


def _ssd(zm, zs, cwx, cbx, cwbc, cbbc, gate_bias, alog_row, d_row, norm_g, *, bsz, seq,
         col_z, col_x, col_bc, ts=256, chunk=128):
    t = zm.shape[0]
    w = SSM_HEADS * SSM_HEADDIM
    bcw = 2 * SSM_GROUPS * SSM_STATE
    nblk = seq // ts
    row = lambda b, s: b * nblk + s
    full = lambda a: pl.BlockSpec(a.shape, lambda b, s: (0, 0))
    return pl.pallas_call(
        functools.partial(_ssd_kernel, ts=ts, chunk=chunk),
        grid=(bsz, nblk),
        in_specs=[
            pl.BlockSpec((ts, w), lambda b, s: (row(b, s), col_z // w)),
            pl.BlockSpec((ts, w), lambda b, s: (row(b, s), col_x // w)),
            pl.BlockSpec((ts, bcw), lambda b, s: (row(b, s), col_bc // bcw)),
            pl.BlockSpec((ts, LANES), lambda b, s: (row(b, s), 0)),
            full(cwx), full(cbx), full(cwbc), full(cbbc), full(gate_bias),
            full(alog_row), full(d_row), full(norm_g),
        ],
        out_specs=pl.BlockSpec((ts, w), lambda b, s: (row(b, s), 0)),
        out_shape=jax.ShapeDtypeStruct((t, w), BF16),
        scratch_shapes=[
            pltpu.VMEM((HALO + ts, w), F32),
            pltpu.VMEM((HALO + ts, bcw), F32),
            pltpu.VMEM((SSM_GROUPS, SSM_STATE, w // SSM_GROUPS), F32),
        ],
        compiler_params=_cparams(("parallel", "arbitrary")),
        name="ssd",
    )(zm, zm, zm, zs, cwx, cbx, cwbc, cbbc, gate_bias, alog_row, d_row, norm_g)


def _fox_prep_kernel(q_ref, k_ref, gs_ref, gb_ref, gq_ref, gk_ref, qn_ref, kn_ref, ft_ref,
                     carry, *, ts):
    d = FOX_HEADDIM

    @pl.when(pl.program_id(1) == 0)
    def _():
        carry[...] = jnp.zeros(carry.shape, F32)

    for h in range(FOX_HEADS):
        sl = slice(h * d, (h + 1) * d)
        qh = q_ref[:, sl].astype(F32)
        qn = qh * lax.rsqrt(jnp.mean(qh * qh, axis=-1, keepdims=True) + EPS) * gq_ref[...]
        qn_ref[:, sl] = qn.astype(qn_ref.dtype)
        kh = k_ref[:, sl].astype(F32)
        kn = kh * lax.rsqrt(jnp.mean(kh * kh, axis=-1, keepdims=True) + EPS) * gk_ref[...]
        kn_ref[:, sl] = (kn * (d ** -0.5 * LOG2E)).astype(kn_ref.dtype)

    lf = _log_sigmoid(gs_ref[...] + gb_ref[...])
    fc = _cumsum_rows(lf) + carry[0:1, :]
    carry[0:1, :] = fc[ts - 1:ts, :]
    ft_ref[...] = fc.T[COL_FOX_F:COL_FOX_F + FOX_HEADS, :] * LOG2E


def _fox_prep(zm, zs, gate_bias, gq, gk, *, bsz, seq, col_q, col_k, ts=256):
    t = zm.shape[0]
    w = FOX_HEADS * FOX_HEADDIM
    nblk = seq // ts
    row = lambda b, s: b * nblk + s
    full = lambda a: pl.BlockSpec(a.shape, lambda b, s: (0, 0))
    return pl.pallas_call(
        functools.partial(_fox_prep_kernel, ts=ts),
        grid=(bsz, nblk),
        in_specs=[
            pl.BlockSpec((ts, w), lambda b, s: (row(b, s), col_q // w)),
            pl.BlockSpec((ts, w), lambda b, s: (row(b, s), col_k // w)),
            pl.BlockSpec((ts, LANES), lambda b, s: (row(b, s), 0)),
            full(gate_bias), full(gq), full(gk),
        ],
        out_specs=[
            pl.BlockSpec((ts, w), lambda b, s: (row(b, s), 0)),
            pl.BlockSpec((ts, w), lambda b, s: (row(b, s), 0)),
            pl.BlockSpec((None, FOX_HEADS, ts), lambda b, s: (b, 0, s)),
        ],
        out_shape=[jax.ShapeDtypeStruct((t, w), BF16), jax.ShapeDtypeStruct((t, w), BF16),
                   jax.ShapeDtypeStruct((bsz, FOX_HEADS, seq), F32)],
        scratch_shapes=[pltpu.VMEM((8, LANES), F32)],
        compiler_params=_cparams(("parallel", "arbitrary")),
        name="fox_prep",
    )(zm, zm, zs, gate_bias, gq, gk)


def _fox_attn_kernel(qi_ref, kj_ref, q_ref, k_ref, v_ref, f_ref, o_ref, m_scr, acc_scr, vaug_scr,
                     *, tq, tk, rb):
    pr = pl.program_id(2)
    qi = qi_ref[pr]
    kj = kj_ref[pr]
    h = pl.program_id(1)
    d = FOX_HEADDIM
    ratio = tq // tk
    per_tile = tk // rb

    @pl.when(kj == 0)
    def _():
        m_scr[...] = jnp.full(m_scr.shape, -jnp.inf, F32)
        acc_scr[...] = jnp.zeros(acc_scr.shape, F32)

    vaug_scr[:, :d] = v_ref[...]
    vaug_scr[:, d:] = (lax.broadcasted_iota(jnp.int32, (tk, d), 1) == 0).astype(BF16)
    fk_all = f_ref[pl.ds(h, 1), :]

    def row_block(r, diag_r):
        rows = slice(r * rb, (r + 1) * rb)
        ncol = tk if diag_r is None else (diag_r + 1) * rb
        s = lax.dot_general(q_ref[rows, :], k_ref[:ncol, :], (((1,), (1,)), ((), ())),
                            preferred_element_type=F32) - fk_all[:, :ncol]
        if diag_r is not None:
            rr = lax.broadcasted_iota(jnp.int32, (rb, ncol), 0) + diag_r * rb
            cc = lax.broadcasted_iota(jnp.int32, (rb, ncol), 1)
            s = jnp.where(cc <= rr, s, -jnp.inf)
        m_old = m_scr[rows, :]
        m_new = jnp.maximum(m_old, jnp.max(s, axis=-1, keepdims=True))
        alpha = jnp.exp2(m_old - m_new)
        p = jnp.exp2(s - m_new[:, :1]).astype(BF16)
        pv = jnp.dot(p, vaug_scr[:ncol, :], preferred_element_type=F32)
        acc_scr[rows, :] = jnp.concatenate([alpha, alpha], axis=1) * acc_scr[rows, :] + pv
        m_scr[rows, :] = m_new

    @pl.when(kj < qi * ratio)
    def _():
        for r in range(tq // rb):
            row_block(r, None)

    for c in range(ratio):
        @pl.when(kj == qi * ratio + c)
        def _(c=c):
            for r in range(c * per_tile, tq // rb):
                row_block(r, r - c * per_tile if r < (c + 1) * per_tile else None)

    @pl.when(kj == qi * ratio + ratio - 1)
    def _():
        acc = acc_scr[...]
        o_ref[...] = (acc[:, :d] / acc[:, d:d + 1]).astype(o_ref.dtype)


def _fox_attn(qn, kn, zm, ft, *, bsz, seq, col_v, tk=1024, ratio=2, rb=128):
    t = qn.shape[0]
    d = FOX_HEADDIM
    tk = min(tk, seq // ratio)
    tq = ratio * tk
    rb = min(rb, tk)
    nq = seq // tq
    nkt = seq // tk
    qi_tab, kj_tab = [], []
    for i in range(nq):
        for j in range(ratio * (i + 1)):
            qi_tab.append(i)
            kj_tab.append(j)
    qi_tab = jnp.asarray(qi_tab, jnp.int32)
    kj_tab = jnp.asarray(kj_tab, jnp.int32)
    vblk = col_v // d
    grid_spec = pltpu.PrefetchScalarGridSpec(
        num_scalar_prefetch=2,
        grid=(bsz, FOX_HEADS, qi_tab.shape[0]),
        in_specs=[
            pl.BlockSpec((tq, d), lambda b, h, p, qi, kj: (b * nq + qi[p], h)),
            pl.BlockSpec((tk, d), lambda b, h, p, qi, kj: (b * nkt + kj[p], h)),
            pl.BlockSpec((tk, d), lambda b, h, p, qi, kj: (b * nkt + kj[p], vblk + h)),
            pl.BlockSpec((None, FOX_HEADS, tk), lambda b, h, p, qi, kj: (b, 0, kj[p])),
        ],
        out_specs=pl.BlockSpec((tq, d), lambda b, h, p, qi, kj: (b * nq + qi[p], h)),
        scratch_shapes=[pltpu.VMEM((tq, LANES), F32), pltpu.VMEM((tq, 2 * d), F32),
                        pltpu.VMEM((tk, 2 * d), BF16)],
    )
    return pl.pallas_call(
        functools.partial(_fox_attn_kernel, tq=tq, tk=tk, rb=rb),
        grid_spec=grid_spec,
        out_shape=jax.ShapeDtypeStruct((t, FOX_HEADS * d), BF16),
        compiler_params=_cparams(("parallel", "parallel", "arbitrary")),
        name="fox_attn",
    )(qi_tab, kj_tab, qn, kn, zm, ft)


def _merge_kernel(x_ref, yml_ref, yssm_ref, yfox_ref, gml_ref, gssm_ref, gfox_ref,
                  pml_ref, pssm_ref, pfox_ref, wout_ref, o_ref):
    def branch(y_ref, g_ref, p_ref):
        proj = jnp.dot(y_ref[...], p_ref[...], preferred_element_type=F32)
        return _sigmoid(g_ref[...].astype(F32)) * proj

    merged = (branch(yml_ref, gml_ref, pml_ref) + branch(yssm_ref, gssm_ref, pssm_ref)
              + branch(yfox_ref, gfox_ref, pfox_ref))
    o_ref[...] = x_ref[...] + jnp.dot(merged.astype(BF16), wout_ref[...], preferred_element_type=F32)


def _merge(x2, y_ml, y_ssm, y_fox, zm, p_ml, p_ssm, p_fox, w_out, *, col_g, tm=512):
    t, d = x2.shape
    rowblk = lambda c: pl.BlockSpec((tm, d), lambda i: (i, c))
    wfull = lambda a: pl.BlockSpec(a.shape, lambda i: (0, 0))
    gblk = col_g // d
    return pl.pallas_call(
        _merge_kernel,
        grid=(t // tm,),
        in_specs=[rowblk(0), rowblk(0), rowblk(0), rowblk(0),
                  rowblk(gblk), rowblk(gblk + 1), rowblk(gblk + 2),
                  wfull(p_ml), wfull(p_ssm), wfull(p_fox), wfull(w_out)],
        out_specs=rowblk(0),
        out_shape=jax.ShapeDtypeStruct((t, d), F32),
        compiler_params=_cparams(("parallel",)),
        name="merge",
    )(x2, y_ml, y_ssm, y_fox, zm, zm, zm, p_ml, p_ssm, p_fox, w_out)


def _erf(v):
    return lax.erf(v)


def _gelu(v):
    return 0.5 * v * (1.0 + _erf(v * (2.0 ** -0.5)))


def _top_values(scores, k, ranked=False):
    vals = []
    cur = scores
    rank = jnp.full(scores.shape, float(k), F32) if ranked else None
    for it in range(k):
        m = jnp.max(cur, axis=0, keepdims=True)
        vals.append(m)
        hit = cur == m
        if ranked:
            rank = jnp.where(hit, float(it), rank)
        if it + 1 < k:
            cur = jnp.where(hit, -jnp.inf, cur)
    return (vals, rank) if ranked else vals


def _candidate_sums(v1, v2, k):
    sub = 8
    v1m = jnp.concatenate(v1, axis=0)
    v2m = jnp.concatenate(v2, axis=0)
    row = lax.broadcasted_iota(jnp.int32, (sub, v1m.shape[1]), 0)
    groups = [v1[0] + v2m]
    a = 1
    while k // (a + 1) > 1:
        nb = k // (a + 1)
        assert nb <= sub
        grp = v1[a] + v2m[:sub]
        groups.append(grp if nb == sub else jnp.where(row < nb, grp, -jnp.inf))
        a += 1
    assert a % sub == 0
    groups.append(v1m[a:] + v2[0])
    return jnp.concatenate(groups, axis=0)


def _peer_kernel(x_ref, g_ref, wq_ref, k1_ref, k2_ref, u_ref, vt_ref, o_ref,
                 ht_scr, q_scr, cnt_scr, e1_scr, rank_scr, e2_scr, acc_scr, g0_scr, g1_scr,
                 w0_scr, w1_scr, *, tb, eb):
    j = pl.program_id(1)
    nk, topk, nhead = PEER_NKEYS, PEER_TOPK, PEER_HEADS
    half = PEER_QDIM // 2
    n_lc = tb // LANES
    pk = 16

    @pl.when(j == 0)
    def _():
        x = x_ref[...]
        hn32 = x * lax.rsqrt(jnp.mean(x * x, axis=-1, keepdims=True) + EPS) * g_ref[...]
        ht_scr[...] = hn32.T.astype(BF16)
        q = jnp.dot(hn32.astype(BF16), wq_ref[...], preferred_element_type=F32).astype(BF16)
        for c in range(q.shape[1] // half):
            q_scr[c] = q[:, c * half:(c + 1) * half]
        acc_scr[...] = jnp.zeros(acc_scr.shape, F32)
        g0_scr[...] = jnp.zeros(g0_scr.shape, BF16)
        w1_scr[...] = jnp.zeros(w1_scr.shape, BF16)

        def head_body(h, carry):
            nt = (((1,), (1,)), ((), ()))
            s1 = lax.dot_general(k1_ref[h], q_scr[2 * h], nt, preferred_element_type=F32)
            s2 = lax.dot_general(k2_ref[h], q_scr[2 * h + 1], nt, preferred_element_type=F32)
            for lc in range(n_lc):
                ls = slice(lc * LANES, (lc + 1) * LANES)
                s1c = s1[:, ls]
                s2c = s2[:, ls]
                v1 = _top_values(s1c, topk)
                v2, rank2 = _top_values(s2c, topk, ranked=True)
                tv = _top_values(_candidate_sums(v1, v2, topk), topk)
                tau, mx = tv[topk - 1], tv[0]
                zsum = jnp.zeros_like(mx)
                for a in range(topk):
                    zsum = zsum + jnp.exp(tv[a] - mx)
                v1m = jnp.concatenate(v1, axis=0)
                npart = jnp.zeros_like(v1m)
                for b in range(topk):
                    npart = npart + jnp.where(v1m + v2[b] >= tau, 1.0, 0.0)
                cnt = jnp.zeros_like(s1c)
                for a in range(topk):
                    cnt = jnp.where(s1c == v1[a], npart[a:a + 1, :], cnt)
                cnt_scr[h, lc] = cnt
                e1_scr[h, lc] = jnp.exp(s1c - v1[0]) / zsum
                rank_scr[h, lc] = rank2.astype(BF16).reshape(nk // pk, pk, LANES)
                e2_scr[h, lc] = jnp.exp(s2c - v2[0]).astype(BF16).reshape(nk // pk, pk, LANES)
            return carry

        lax.fori_loop(0, nhead, head_body, 0)

    n_r = eb // nk
    n_e = pl.num_programs(1) - 2
    ja = jnp.minimum(j, n_e - 1)

    def stage_a(w_wr, lc, r_range):
        for r in r_range:
            i1 = ja * n_r + r
            wsum = None
            for h in range(nhead):
                cnt = jnp.broadcast_to(cnt_scr[h, lc, pl.ds(i1, 1), :], (pk, LANES)).astype(BF16)
                e1 = jnp.broadcast_to(e1_scr[h, lc, pl.ds(i1, 1), :], (pk, LANES)).astype(BF16)
                term = jnp.where(rank_scr[h, lc] < cnt[None], e2_scr[h, lc] * e1[None],
                                 jnp.zeros((), BF16))
                wsum = term if wsum is None else wsum + term
            w_wr[lc, r * (nk // pk):(r + 1) * (nk // pk)] = wsum

    def stage_b(w_rd, g_wr, n, m):
        ts_ = slice(n * 2 * LANES, (n + 1) * 2 * LANES)
        es = slice(m * (eb // 2), (m + 1) * (eb // 2))
        ps = slice(m * (eb // 2 // pk), (m + 1) * (eb // 2 // pk))
        act = _gelu(jnp.dot(u_ref[es, :], ht_scr[:, ts_], preferred_element_type=F32))
        w_prev = jnp.concatenate([w_rd[2 * n + c, ps].reshape(eb // 2, LANES) for c in range(2)], axis=1)
        g_wr[es, ts_] = (act * w_prev.astype(F32)).astype(BF16)

    def stage_c(g_rd, n, m):
        ts_ = slice(n * 2 * LANES, (n + 1) * 2 * LANES)
        ds_ = slice(m * (acc_scr.shape[0] // 2), (m + 1) * (acc_scr.shape[0] // 2))
        acc_scr[ds_, ts_] += jnp.dot(vt_ref[ds_, :], g_rd[:, ts_], preferred_element_type=F32)

    def step(w_wr, w_rd, g_wr, g_rd):
        for n in range(n_lc // 2):
            for m in range(2):
                stage_a(w_wr, 2 * n + m, range(0, n_r // 2))
                stage_b(w_rd, g_wr, n, m)
                stage_a(w_wr, 2 * n + m, range(n_r // 2, n_r))
                stage_c(g_rd, n, m)

    @pl.when(j % 2 == 0)
    def _():
        step(w0_scr, w1_scr, g1_scr, g0_scr)

    @pl.when(j % 2 == 1)
    def _():
        step(w1_scr, w0_scr, g0_scr, g1_scr)

    @pl.when(j == n_e + 1)
    def _():
        o_ref[...] = x_ref[...] + acc_scr[...].T


def _peer(x2, g, w_q, keys1, keys2, u, v, *, tb=1024, eb=512):
    t, d = x2.shape
    tb = min(tb, t)
    ne = u.shape[0]
    nq = w_q.shape[1]
    n_e = ne // eb
    n_lc = tb // LANES
    pk = 16
    v_t = v.reshape(n_e, eb, d).transpose(0, 2, 1).astype(BF16)
    once = pl.Buffered(1)
    return pl.pallas_call(
        functools.partial(_peer_kernel, tb=tb, eb=eb),
        grid=(t // tb, n_e + 2),
        in_specs=[
            pl.BlockSpec((tb, d), lambda i, j: (i, 0), pipeline_mode=once),
            pl.BlockSpec((1, d), lambda i, j: (0, 0)),
            pl.BlockSpec((d, nq), lambda i, j: (0, 0), pipeline_mode=once),
            pl.BlockSpec(keys1.shape, lambda i, j: (0, 0, 0)),
            pl.BlockSpec(keys2.shape, lambda i, j: (0, 0, 0)),
            pl.BlockSpec((eb, d), lambda i, j: (jnp.clip(j - 1, 0, n_e - 1), 0)),
            pl.BlockSpec((None, d, eb), lambda i, j: (jnp.clip(j - 2, 0, n_e - 1), 0, 0)),
        ],
        out_specs=pl.BlockSpec((tb, d), lambda i, j: (i, 0)),
        out_shape=jax.ShapeDtypeStruct((t, d), F32),
        scratch_shapes=[
            pltpu.VMEM((d, tb), BF16),
            pltpu.VMEM((2 * PEER_HEADS, tb, PEER_QDIM // 2), BF16),
            pltpu.VMEM((PEER_HEADS, n_lc, PEER_NKEYS, LANES), F32),
            pltpu.VMEM((PEER_HEADS, n_lc, PEER_NKEYS, LANES), F32),
            pltpu.VMEM((PEER_HEADS, n_lc, PEER_NKEYS // pk, pk, LANES), BF16),
            pltpu.VMEM((PEER_HEADS, n_lc, PEER_NKEYS // pk, pk, LANES), BF16),
            pltpu.VMEM((d, tb), F32),
            pltpu.VMEM((eb, tb), BF16),
            pltpu.VMEM((eb, tb), BF16),
            pltpu.VMEM((n_lc, eb // pk, pk, LANES), BF16),
            pltpu.VMEM((n_lc, eb // pk, pk, LANES), BF16),
        ],
        compiler_params=_cparams(("parallel", "arbitrary")),
        name="peer",
    )(x2, g, w_q, keys1, keys2, u, v_t)


def _ple_kernel(x_ref, p_ref, g_ref, wg_ref, wp_ref, fg_ref, o_ref, *, final):
    x = x_ref[...]
    hn = (x * lax.rsqrt(jnp.mean(x * x, axis=-1, keepdims=True) + EPS) * g_ref[...]).astype(BF16)
    gate = _sigmoid(jnp.dot(hn, wg_ref[...], preferred_element_type=F32))
    proj = jnp.dot(p_ref[...].astype(BF16), wp_ref[...], preferred_element_type=F32)
    y = x + gate * proj
    if final:
        y = y * lax.rsqrt(jnp.mean(y * y, axis=-1, keepdims=True) + EPS) * fg_ref[...]
    o_ref[...] = y


def _ple(x2, p2, g, w_gate, w_proj, final_g, *, final, tm=512):
    t, d = x2.shape
    pd = p2.shape[1]
    full = lambda a: pl.BlockSpec(a.shape, lambda i: (0, 0))
    return pl.pallas_call(
        functools.partial(_ple_kernel, final=final),
        grid=(t // tm,),
        in_specs=[pl.BlockSpec((tm, d), lambda i: (i, 0)), pl.BlockSpec((tm, pd), lambda i: (i, 0)),
                  full(g), full(w_gate), full(w_proj), full(final_g)],
        out_specs=pl.BlockSpec((tm, d), lambda i: (i, 0)),
        out_shape=jax.ShapeDtypeStruct((t, d), F32),
        compiler_params=_cparams(("parallel",)),
        name="ple",
    )(x2, p2, g, w_gate, w_proj, final_g)


def _row(v):
    return v.reshape(1, -1).astype(F32)


def _gate_bias_row(ml_b_i, ml_b_f, dt_bias, fox_b_f):
    r = jnp.zeros((LANES,), F32)
    r = r.at[COL_ML_I:COL_ML_I + ML_HEADS].set(ml_b_i.astype(F32))
    r = r.at[COL_ML_F:COL_ML_F + ML_HEADS].set(ml_b_f.astype(F32))
    r = r.at[COL_DT:COL_DT + SSM_HEADS].set(dt_bias.astype(F32))
    r = r.at[COL_FOX_F:COL_FOX_F + FOX_HEADS].set(fox_b_f.astype(F32))
    return r.reshape(1, LANES)


def _pad_rows(w, rows):
    return jnp.concatenate([w, jnp.zeros((rows - w.shape[0],) + w.shape[1:], w.dtype)], axis=0)


def kernel(x, p, norm_mix_g, w_in, ml_conv_w, ml_conv_b, ml_b_i, ml_b_f, ml_norm_g, ssm_conv_w, ssm_conv_b, ssm_dt_bias, ssm_a_log, ssm_d, ssm_norm_g, fox_q_norm_g, fox_k_norm_g, fox_b_f, w_branch_ml, w_branch_ssm, w_branch_fox, w_out, norm_ffn_g, peer_w_q, peer_keys1, peer_keys2, peer_u, peer_v, norm_ple_g, ple_w_gate, ple_w_proj, final_norm_g):
    bsz, seq, d = x.shape
    depth = w_in.shape[0]
    t = bsz * seq
    ml_qk = ML_HEADS * ML_DQK
    ml_w = ML_HEADS * ML_DV
    ssm_w = SSM_HEADS * SSM_HEADDIM
    ssm_bc = SSM_GROUPS * SSM_STATE
    fox_w = FOX_HEADS * FOX_HEADDIM
    splits = (ml_qk, ml_qk, ml_w, ml_w, ML_HEADS, ML_HEADS, ssm_w, ssm_w, ssm_bc, ssm_bc, SSM_HEADS,
              fox_w, fox_w, fox_w, FOX_HEADS, d, d, d)
    offs = [0]
    for s_ in splits:
        offs.append(offs[-1] + s_)
    (o_mlq, o_mlk, o_mlv, o_mlo, o_mli, o_mlf, o_sz, o_sx, o_sb, o_sc, o_sdt,
     o_fq, o_fk, o_fv, o_ff, o_gml, o_gssm, o_gfox) = offs[:-1]

    main_segments = [(o_mlq, 2 * ml_qk), (o_mlv, ml_w), (o_mlo, ml_w), (o_sz, ssm_w), (o_sx, ssm_w),
                     (o_fq, fox_w), (o_fk, fox_w), (o_fv, fox_w), (o_gml, d), (o_gssm, d), (o_gfox, d),
                     (o_sb, 2 * ssm_bc)]
    seg_off = [0]
    for _, wd in main_segments:
        seg_off.append(seg_off[-1] + wd)
    (c_mlqk, c_mlv, c_mlo, c_sz, c_sx, c_fq, c_fk, c_fv, c_gml, c_gssm, c_gfox, c_bc) = seg_off[:-1]
    assert (c_mlqk, c_mlv, c_mlo) == (0, ml_w, 2 * ml_w)

    x2 = x.reshape(t, d)
    for i in range(depth):
        w = w_in[i]
        w_main = jnp.concatenate([w[:, o:o + wd] for o, wd in main_segments], axis=1).astype(BF16)
        small = jnp.zeros((d, LANES), w.dtype)
        small = small.at[:, COL_ML_I:COL_ML_I + ML_HEADS].set(w[:, o_mli:o_mli + ML_HEADS])
        small = small.at[:, COL_ML_F:COL_ML_F + ML_HEADS].set(w[:, o_mlf:o_mlf + ML_HEADS])
        small = small.at[:, COL_DT:COL_DT + SSM_HEADS].set(w[:, o_sdt:o_sdt + SSM_HEADS])
        small = small.at[:, COL_FOX_F:COL_FOX_F + FOX_HEADS].set(w[:, o_ff:o_ff + FOX_HEADS])
        w_small = small.astype(BF16)
        gate_bias = _gate_bias_row(ml_b_i[i], ml_b_f[i], ssm_dt_bias[i], fox_b_f[i])

        zm, zs = _inproj(x2, _row(norm_mix_g[i]), w_main, w_small)

        y_ml = _mlstm(zm, zs, _pad_rows(ml_conv_w[i].astype(F32), 8), _row(ml_conv_b[i]), gate_bias,
                      _row(ml_norm_g[i]), bsz=bsz, seq=seq)

        scw = ssm_conv_w[i].astype(F32)
        scb = ssm_conv_b[i].astype(F32)
        alog_row = jnp.zeros((LANES,), F32).at[COL_DT:COL_DT + SSM_HEADS].set(
            ssm_a_log[i].astype(F32)).reshape(1, LANES)
        d_row = jnp.repeat(ssm_d[i].astype(F32), SSM_HEADDIM).reshape(1, ssm_w)
        y_ssm = _ssd(zm, zs, _pad_rows(scw[:, :ssm_w], 8), _row(scb[:ssm_w]),
                     _pad_rows(scw[:, ssm_w:], 8), _row(scb[ssm_w:]), gate_bias, alog_row, d_row,
                     _row(ssm_norm_g[i]), bsz=bsz, seq=seq, col_z=c_sz, col_x=c_sx, col_bc=c_bc)

        qn, kn, ft = _fox_prep(zm, zs, gate_bias, _row(fox_q_norm_g[i]), _row(fox_k_norm_g[i]),
                               bsz=bsz, seq=seq, col_q=c_fq, col_k=c_fk)
        y_fox = _fox_attn(qn, kn, zm, ft, bsz=bsz, seq=seq, col_v=c_fv)

        x2 = _merge(x2, y_ml, y_ssm, y_fox, zm, w_branch_ml[i].astype(BF16), w_branch_ssm[i].astype(BF16),
                    w_branch_fox[i].astype(BF16), w_out[i].astype(BF16), col_g=c_gml)

        x2 = _peer(x2, _row(norm_ffn_g[i]), peer_w_q[i].astype(BF16), peer_keys1[i].astype(BF16),
                   peer_keys2[i].astype(BF16), peer_u[i].astype(BF16), peer_v[i])

        x2 = _ple(x2, p[i].reshape(t, -1), _row(norm_ple_g[i]), ple_w_gate[i].astype(BF16),
                  ple_w_proj[i].astype(BF16), _row(final_norm_g), final=(i == depth - 1))
    return x2.reshape(bsz, seq, d)
```

```python
import functools
import math

import jax
import jax.numpy as jnp
from jax import lax
from jax.experimental import pallas as pl
from jax.experimental.pallas import tpu as pltpu

F32 = jnp.float32
BF16 = jnp.bfloat16
EPS = 1e-6
LOG2E = math.log2(math.e)

CONV_WIDTH = 4
ML_HEADS = 8
ML_DQK = 64
ML_DV = 128
SSM_HEADS = 16
SSM_HEADDIM = 64
SSM_GROUPS = 2
SSM_STATE = 128
FOX_HEADS = 8
FOX_HEADDIM = 128
PEER_HEADS = 8
PEER_NKEYS = 128
PEER_QDIM = 256
PEER_TOPK = 16

LANES = 128
SUB = 8
HALO = 8

COL_ML_I = 0
COL_ML_F = 8
COL_DT = 16
COL_FOX_F = 32

VMEM_LIMIT = 56 * 1024 * 1024


def _cparams(sem, flags=None):
    return pltpu.CompilerParams(dimension_semantics=sem, vmem_limit_bytes=VMEM_LIMIT, flags=flags)


def _log_sigmoid(v):
    return jnp.minimum(v, 0.0) - jnp.log1p(jnp.exp(-jnp.abs(v)))


def _sigmoid(v):
    return 1.0 / (1.0 + jnp.exp(-v))


def _silu(v):
    return v * _sigmoid(v)


def _softplus(v):
    return jnp.maximum(v, 0.0) + jnp.log1p(jnp.exp(-jnp.abs(v)))


def _tril(n):
    r = lax.broadcasted_iota(jnp.int32, (n, n), 0)
    c = lax.broadcasted_iota(jnp.int32, (n, n), 1)
    return r >= c


def _cumsum_rows(v):
    n = v.shape[0]
    return jnp.dot(_tril(n).astype(F32), v, precision=lax.Precision.HIGHEST,
                   preferred_element_type=F32)


def _inproj_kernel(x_ref, g_ref, wm_ref, ws_ref, zm_ref, zs_ref, h_scr):
    @pl.when(pl.program_id(1) == 0)
    def _():
        x = x_ref[...]
        ms = jnp.mean(x * x, axis=-1, keepdims=True)
        h = (x * lax.rsqrt(ms + EPS) * g_ref[...]).astype(BF16)
        h_scr[...] = h
        zs_ref[...] = jnp.dot(h, ws_ref[...], preferred_element_type=F32)

    zm_ref[...] = jnp.dot(h_scr[...], wm_ref[...], preferred_element_type=F32).astype(zm_ref.dtype)


def _inproj(x2, g, w_main, w_small, tm=2048, tn=512):
    t, d = x2.shape
    tm = min(tm, t)
    n = w_main.shape[1]
    return pl.pallas_call(
        _inproj_kernel,
        grid=(t // tm, n // tn),
        in_specs=[
            pl.BlockSpec((tm, d), lambda i, j: (i, 0)),
            pl.BlockSpec((1, d), lambda i, j: (0, 0)),
            pl.BlockSpec((d, tn), lambda i, j: (0, j)),
            pl.BlockSpec((d, LANES), lambda i, j: (0, 0)),
        ],
        out_specs=[
            pl.BlockSpec((tm, tn), lambda i, j: (i, j)),
            pl.BlockSpec((tm, LANES), lambda i, j: (i, 0)),
        ],
        out_shape=[jax.ShapeDtypeStruct((t, n), BF16), jax.ShapeDtypeStruct((t, LANES), F32)],
        scratch_shapes=[pltpu.VMEM((tm, d), BF16)],
        compiler_params=_cparams(("parallel", "arbitrary")),
        name="inproj",
    )(x2, g, w_main, w_small)


def _conv_chunk(buf_ref, w_ref, b_ref, start, length):
    acc = None
    for j in range(CONV_WIDTH):
        off = HALO - (CONV_WIDTH - 1) + j + start
        term = buf_ref[off:off + length, :] * w_ref[j:j + 1, :]
        acc = term if acc is None else acc + term
    return acc + b_ref[...]


def _mlstm_kernel(qk_ref, v_ref, o_ref, gs_ref, cw_ref, cb_ref, gb_ref, ng_ref, y_ref,
                  buf, c_scr, m_scr, *, ts, chunk):
    nh, dk, dv = ML_HEADS, ML_DQK, ML_DV
    qkw = nh * dk

    @pl.when(pl.program_id(1) == 0)
    def _():
        buf[0:HALO, :] = jnp.zeros((HALO, buf.shape[1]), F32)
        c_scr[...] = jnp.zeros(c_scr.shape, F32)
        m_scr[...] = jnp.zeros(m_scr.shape, F32)

    buf[HALO:HALO + ts, :] = qk_ref[...].astype(F32)
    causal = _tril(chunk)
    lane = lax.broadcasted_iota(jnp.int32, (chunk, LANES), 1)
    ones_col = (lane == 0).astype(BF16)

    for c in range(ts // chunk):
        r0 = c * chunk
        act = _silu(_conv_chunk(buf, cw_ref, cb_ref, r0, chunk))
        q_all = (act[:, :qkw] * (dk ** -0.5)).astype(BF16)
        kt_all = act[:, qkw:].T
        gates = gs_ref[r0:r0 + chunk, :] + gb_ref[...]
        bcum = _cumsum_rows(_log_sigmoid(gates))
        gates_t = gates.T
        bcum_t = bcum.T
        for h in range(nh):
            i_col = gates[:, COL_ML_I + h:COL_ML_I + h + 1]
            i_row = gates_t[COL_ML_I + h:COL_ML_I + h + 1, :]
            b_col = bcum[:, COL_ML_F + h:COL_ML_F + h + 1]
            b_row = bcum_t[COL_ML_F + h:COL_ML_F + h + 1, :]
            g_tot = b_col[chunk - 1:chunk, :]
            m_prev = m_scr[h:h + 1, 0:1]
            dmat = jnp.where(causal, b_col - b_row + i_row, -jnp.inf)
            inter = b_col + m_prev
            m_t = jnp.maximum(inter, jnp.max(dmat, axis=-1, keepdims=True))
            qh = q_all[:, h * dk:(h + 1) * dk]
            kt = kt_all[h * dk:(h + 1) * dk, :]
            s = jnp.dot(qh, kt.astype(BF16), preferred_element_type=F32)
            sc = (s * jnp.exp(dmat - m_t)).astype(BF16)
            w_inter = jnp.exp(inter - m_t)
            vaug = jnp.concatenate([v_ref[r0:r0 + chunk, h * dv:(h + 1) * dv], ones_col], axis=1)
            cmem = c_scr[h]
            tot = (jnp.dot(sc, vaug, preferred_element_type=F32)
                   + w_inter * jnp.dot(qh, cmem.astype(BF16), preferred_element_type=F32))
            num = tot[:, :dv]
            den = tot[:, dv:dv + 1]
            hv = num / jnp.maximum(jnp.abs(den), jnp.exp(-m_t))
            hn = hv * lax.rsqrt(jnp.mean(hv * hv, axis=-1, keepdims=True) + EPS)
            hn = hn * ng_ref[:, h * dv:(h + 1) * dv]
            og = _sigmoid(o_ref[r0:r0 + chunk, h * dv:(h + 1) * dv].astype(F32))
            y_ref[r0:r0 + chunk, h * dv:(h + 1) * dv] = (og * hn).astype(y_ref.dtype)
            a_row = g_tot - b_row + i_row
            m_new = jnp.maximum(g_tot + m_prev, jnp.max(a_row, axis=-1, keepdims=True))
            wa_row = jnp.exp(a_row - m_new)
            decay = jnp.exp(g_tot + m_prev - m_new)
            kw = (kt * wa_row).astype(BF16)
            c_scr[h] = decay * cmem + jnp.dot(kw, vaug, preferred_element_type=F32)
            m_scr[h:h + 1, :] = jnp.broadcast_to(m_new, (1, LANES))

    buf[0:HALO, :] = buf[ts:ts + HALO, :]


def _mlstm(zm, zs, conv_w, conv_b, gate_bias, norm_g, *, bsz, seq, ts=256, chunk=128):
    t = zm.shape[0]
    w = ML_HEADS * ML_DV
    nblk = seq // ts
    row = lambda b, s: b * nblk + s
    return pl.pallas_call(
        functools.partial(_mlstm_kernel, ts=ts, chunk=chunk),
        grid=(bsz, nblk),
        in_specs=[
            pl.BlockSpec((ts, w), lambda b, s: (row(b, s), 0)),
            pl.BlockSpec((ts, w), lambda b, s: (row(b, s), 1)),
            pl.BlockSpec((ts, w), lambda b, s: (row(b, s), 2)),
            pl.BlockSpec((ts, LANES), lambda b, s: (row(b, s), 0)),
            pl.BlockSpec(conv_w.shape, lambda b, s: (0, 0)),
            pl.BlockSpec(conv_b.shape, lambda b, s: (0, 0)),
            pl.BlockSpec(gate_bias.shape, lambda b, s: (0, 0)),
            pl.BlockSpec(norm_g.shape, lambda b, s: (0, 0)),
        ],
        out_specs=pl.BlockSpec((ts, w), lambda b, s: (row(b, s), 0)),
        out_shape=jax.ShapeDtypeStruct((t, w), BF16),
        scratch_shapes=[
            pltpu.VMEM((HALO + ts, w), F32),
            pltpu.VMEM((ML_HEADS, ML_DQK, 2 * ML_DV), F32),
            pltpu.VMEM((ML_HEADS, LANES), F32),
        ],
        compiler_params=_cparams(("parallel", "arbitrary")),
        name="mlstm",
    )(zm, zm, zm, zs, conv_w, conv_b, gate_bias, norm_g)


def _ssd_kernel(z_ref, x_ref, bc_ref, gs_ref, cwx_ref, cbx_ref, cwbc_ref, cbbc_ref, gb_ref,
                alog_ref, d_ref, ng_ref, y_ref, bufx, bufbc, st_scr, *, ts, chunk):
    ng, nst, p = SSM_GROUPS, SSM_STATE, SSM_HEADDIM
    hg = SSM_HEADS // ng
    gw = hg * p
    pairs_per_group = gw // LANES

    @pl.when(pl.program_id(1) == 0)
    def _():
        bufx[0:HALO, :] = jnp.zeros((HALO, bufx.shape[1]), F32)
        bufbc[0:HALO, :] = jnp.zeros((HALO, bufbc.shape[1]), F32)
        st_scr[...] = jnp.zeros(st_scr.shape, F32)

    bufx[HALO:HALO + ts, :] = x_ref[...].astype(F32)
    bufbc[HALO:HALO + ts, :] = bc_ref[...].astype(F32)
    causal = _tril(chunk)
    low_half = lax.broadcasted_iota(jnp.int32, (chunk, LANES), 1) < p
    a_row_all = -jnp.exp(alog_ref[...])

    for c in range(ts // chunk):
        r0 = c * chunk
        xa = _silu(_conv_chunk(bufx, cwx_ref, cbx_ref, r0, chunk))
        bca = _silu(_conv_chunk(bufbc, cwbc_ref, cbbc_ref, r0, chunk))
        dt = _softplus(gs_ref[r0:r0 + chunk, :] + gb_ref[...])
        acum = _cumsum_rows(dt * a_row_all)
        acum_t = acum.T
        y_parts = []
        for g in range(ng):
            bg = bca[:, g * nst:(g + 1) * nst]
            cg = bca[:, ng * nst + g * nst:ng * nst + (g + 1) * nst].astype(BF16)
            bg_t = bg.T
            cbt = jnp.dot(cg, bg_t.astype(BF16), preferred_element_type=F32)
            state = st_scr[g]
            inter = jnp.dot(cg, state.astype(BF16), preferred_element_type=F32)
            xs_scaled, last_parts = [], []
            for pp in range(pairs_per_group):
                pidx = g * pairs_per_group + pp
                ha = 2 * pidx
                ca, cb = COL_DT + ha, COL_DT + ha + 1
                sl = slice(pidx * LANES, (pidx + 1) * LANES)
                xa_p = xa[:, sl]
                dt_pair = jnp.where(low_half, dt[:, ca:ca + 1], dt[:, cb:cb + 1])
                ac_pair = jnp.where(low_half, acum[:, ca:ca + 1], acum[:, cb:cb + 1])
                xs_p = xa_p * dt_pair
                xs_b = xs_p.astype(BF16)
                lm_a = jnp.exp(jnp.where(causal, acum[:, ca:ca + 1] - acum_t[ca:ca + 1, :], -jnp.inf))
                lm_b = jnp.exp(jnp.where(causal, acum[:, cb:cb + 1] - acum_t[cb:cb + 1, :], -jnp.inf))
                ya = jnp.dot((cbt * lm_a).astype(BF16), xs_b, preferred_element_type=F32)
                yb = jnp.dot((cbt * lm_b).astype(BF16), xs_b, preferred_element_type=F32)
                y_p = (jnp.where(low_half, ya, yb)
                       + inter[:, pp * LANES:(pp + 1) * LANES] * jnp.exp(ac_pair)
                       + d_ref[:, sl] * xa_p)
                y_parts.append(y_p)
                last = ac_pair[chunk - 1:chunk, :]
                xs_scaled.append((xs_p * jnp.exp(last - ac_pair)).astype(BF16))
                last_parts.append(last)
            xs_g = jnp.concatenate(xs_scaled, axis=1)
            last_g = jnp.concatenate(last_parts, axis=1)
            st_scr[g] = jnp.exp(last_g) * state + jnp.dot(bg_t.astype(BF16), xs_g,
                                                          preferred_element_type=F32)
        ys = jnp.concatenate(y_parts, axis=1)
        ys = ys * _silu(z_ref[r0:r0 + chunk, :].astype(F32))
        outs = []
        for g in range(ng):
            yg = ys[:, g * gw:(g + 1) * gw]
            yn = yg * lax.rsqrt(jnp.mean(yg * yg, axis=-1, keepdims=True) + EPS)
            outs.append(yn * ng_ref[:, g * gw:(g + 1) * gw])
        y_ref[r0:r0 + chunk, :] = jnp.concatenate(outs, axis=1).astype(y_ref.dtype)

    bufx[0:HALO, :] = bufx[ts:ts + HALO, :]
    bufbc[0:HALO, :] = bufbc[ts:ts + HALO, :]


def _ssd(zm, zs, cwx, cbx, cwbc, cbbc, gate_bias, alog_row, d_row, norm_g, *, bsz, seq,
         col_z, col_x, col_bc, ts=256, chunk=128):
    t = zm.shape[0]
    w = SSM_HEADS * SSM_HEADDIM
    bcw = 2 * SSM_GROUPS * SSM_STATE
    nblk = seq // ts
    row = lambda b, s: b * nblk + s
    full = lambda a: pl.BlockSpec(a.shape, lambda b, s: (0, 0))
    return pl.pallas_call(
        functools.partial(_ssd_kernel, ts=ts, chunk=chunk),
        grid=(bsz, nblk),
        in_specs=[
            pl.BlockSpec((ts, w), lambda b, s: (row(b, s), col_z // w)),
            pl.BlockSpec((ts, w), lambda b, s: (row(b, s), col_x // w)),
            pl.BlockSpec((ts, bcw), lambda b, s: (row(b, s), col_bc // bcw)),
            pl.BlockSpec((ts, LANES), lambda b, s: (row(b, s), 0)),
            full(cwx), full(cbx), full(cwbc), full(cbbc), full(gate_bias),
            full(alog_row), full(d_row), full(norm_g),
        ],
        out_specs=pl.BlockSpec((ts, w), lambda b, s: (row(b, s), 0)),
        out_shape=jax.ShapeDtypeStruct((t, w), BF16),
        scratch_shapes=[
            pltpu.VMEM((HALO + ts, w), F32),
            pltpu.VMEM((HALO + ts, bcw), F32),
            pltpu.VMEM((SSM_GROUPS, SSM_STATE, w // SSM_GROUPS), F32),
        ],
        compiler_params=_cparams(("parallel", "arbitrary")),
        name="ssd",
    )(zm, zm, zm, zs, cwx, cbx, cwbc, cbbc, gate_bias, alog_row, d_row, norm_g)


def _fox_prep_kernel(q_ref, k_ref, gs_ref, gb_ref, gq_ref, gk_ref, qn_ref, kn_ref, ft_ref,
                     carry, *, ts):
    d = FOX_HEADDIM

    @pl.when(pl.program_id(1) == 0)
    def _():
        carry[...] = jnp.zeros(carry.shape, F32)

    for h in range(FOX_HEADS):
        sl = slice(h * d, (h + 1) * d)
        qh = q_ref[:, sl].astype(F32)
        qn = qh * lax.rsqrt(jnp.mean(qh * qh, axis=-1, keepdims=True) + EPS) * gq_ref[...]
        qn_ref[:, sl] = qn.astype(qn_ref.dtype)
        kh = k_ref[:, sl].astype(F32)
        kn = kh * lax.rsqrt(jnp.mean(kh * kh, axis=-1, keepdims=True) + EPS) * gk_ref[...]
        kn_ref[:, sl] = (kn * (d ** -0.5 * LOG2E)).astype(kn_ref.dtype)

    lf = _log_sigmoid(gs_ref[...] + gb_ref[...])
    fc = _cumsum_rows(lf) + carry[0:1, :]
    carry[0:1, :] = fc[ts - 1:ts, :]
    ft_ref[...] = fc.T[COL_FOX_F:COL_FOX_F + FOX_HEADS, :] * LOG2E


def _fox_prep(zm, zs, gate_bias, gq, gk, *, bsz, seq, col_q, col_k, ts=256):
    t = zm.shape[0]
    w = FOX_HEADS * FOX_HEADDIM
    nblk = seq // ts
    row = lambda b, s: b * nblk + s
    full = lambda a: pl.BlockSpec(a.shape, lambda b, s: (0, 0))
    return pl.pallas_call(
        functools.partial(_fox_prep_kernel, ts=ts),
        grid=(bsz, nblk),
        in_specs=[
            pl.BlockSpec((ts, w), lambda b, s: (row(b, s), col_q // w)),
            pl.BlockSpec((ts, w), lambda b, s: (row(b, s), col_k // w)),
            pl.BlockSpec((ts, LANES), lambda b, s: (row(b, s), 0)),
            full(gate_bias), full(gq), full(gk),
        ],
        out_specs=[
            pl.BlockSpec((ts, w), lambda b, s: (row(b, s), 0)),
            pl.BlockSpec((ts, w), lambda b, s: (row(b, s), 0)),
            pl.BlockSpec((None, FOX_HEADS, ts), lambda b, s: (b, 0, s)),
        ],
        out_shape=[jax.ShapeDtypeStruct((t, w), BF16), jax.ShapeDtypeStruct((t, w), BF16),
                   jax.ShapeDtypeStruct((bsz, FOX_HEADS, seq), F32)],
        scratch_shapes=[pltpu.VMEM((8, LANES), F32)],
        compiler_params=_cparams(("parallel", "arbitrary")),
        name="fox_prep",
    )(zm, zm, zs, gate_bias, gq, gk)


def _fox_attn_kernel(qi_ref, kj_ref, q_ref, k_ref, v_ref, f_ref, o_ref, m_scr, acc_scr, vaug_scr,
                     *, tq, tk, rb):
    pr = pl.program_id(2)
    qi = qi_ref[pr]
    kj = kj_ref[pr]
    h = pl.program_id(1)
    d = FOX_HEADDIM
    ratio = tq // tk
    per_tile = tk // rb

    @pl.when(kj == 0)
    def _():
        m_scr[...] = jnp.full(m_scr.shape, -jnp.inf, F32)
        acc_scr[...] = jnp.zeros(acc_scr.shape, F32)

    vaug_scr[:, :d] = v_ref[...]
    vaug_scr[:, d:] = (lax.broadcasted_iota(jnp.int32, (tk, d), 1) == 0).astype(BF16)
    fk_all = f_ref[pl.ds(h, 1), :]

    def row_block(r, diag_r):
        rows = slice(r * rb, (r + 1) * rb)
        ncol = tk if diag_r is None else (diag_r + 1) * rb
        s = lax.dot_general(q_ref[rows, :], k_ref[:ncol, :], (((1,), (1,)), ((), ())),
                            preferred_element_type=F32) - fk_all[:, :ncol]
        if diag_r is not None:
            rr = lax.broadcasted_iota(jnp.int32, (rb, ncol), 0) + diag_r * rb
            cc = lax.broadcasted_iota(jnp.int32, (rb, ncol), 1)
            s = jnp.where(cc <= rr, s, -jnp.inf)
        m_old = m_scr[rows, :]
        m_new = jnp.maximum(m_old, jnp.max(s, axis=-1, keepdims=True))
        alpha = jnp.exp2(m_old - m_new)
        p = jnp.exp2(s - m_new[:, :1]).astype(BF16)
        pv = jnp.dot(p, vaug_scr[:ncol, :], preferred_element_type=F32)
        acc_scr[rows, :] = jnp.concatenate([alpha, alpha], axis=1) * acc_scr[rows, :] + pv
        m_scr[rows, :] = m_new

    @pl.when(kj < qi * ratio)
    def _():
        for r in range(tq // rb):
            row_block(r, None)

    for c in range(ratio):
        @pl.when(kj == qi * ratio + c)
        def _(c=c):
            for r in range(c * per_tile, tq // rb):
                row_block(r, r - c * per_tile if r < (c + 1) * per_tile else None)

    @pl.when(kj == qi * ratio + ratio - 1)
    def _():
        acc = acc_scr[...]
        o_ref[...] = (acc[:, :d] / acc[:, d:d + 1]).astype(o_ref.dtype)


def _fox_attn(qn, kn, zm, ft, *, bsz, seq, col_v, tk=1024, ratio=2, rb=128):
    t = qn.shape[0]
    d = FOX_HEADDIM
    tk = min(tk, seq // ratio)
    tq = ratio * tk
    rb = min(rb, tk)
    nq = seq // tq
    nkt = seq // tk
    qi_tab, kj_tab = [], []
    for i in range(nq):
        for j in range(ratio * (i + 1)):
            qi_tab.append(i)
            kj_tab.append(j)
    qi_tab = jnp.asarray(qi_tab, jnp.int32)
    kj_tab = jnp.asarray(kj_tab, jnp.int32)
    vblk = col_v // d
    grid_spec = pltpu.PrefetchScalarGridSpec(
        num_scalar_prefetch=2,
        grid=(bsz, FOX_HEADS, qi_tab.shape[0]),
        in_specs=[
            pl.BlockSpec((tq, d), lambda b, h, p, qi, kj: (b * nq + qi[p], h)),
            pl.BlockSpec((tk, d), lambda b, h, p, qi, kj: (b * nkt + kj[p], h)),
            pl.BlockSpec((tk, d), lambda b, h, p, qi, kj: (b * nkt + kj[p], vblk + h)),
            pl.BlockSpec((None, FOX_HEADS, tk), lambda b, h, p, qi, kj: (b, 0, kj[p])),
        ],
        out_specs=pl.BlockSpec((tq, d), lambda b, h, p, qi, kj: (b * nq + qi[p], h)),
        scratch_shapes=[pltpu.VMEM((tq, LANES), F32), pltpu.VMEM((tq, 2 * d), F32),
                        pltpu.VMEM((tk, 2 * d), BF16)],
    )
    return pl.pallas_call(
        functools.partial(_fox_attn_kernel, tq=tq, tk=tk, rb=rb),
        grid_spec=grid_spec,
        out_shape=jax.ShapeDtypeStruct((t, FOX_HEADS * d), BF16),
        compiler_params=_cparams(("parallel", "parallel", "arbitrary")),
        name="fox_attn",
    )(qi_tab, kj_tab, qn, kn, zm, ft)


def _merge_kernel(x_ref, yml_ref, yssm_ref, yfox_ref, gml_ref, gssm_ref, gfox_ref,
                  pml_ref, pssm_ref, pfox_ref, wout_ref, o_ref):
    def branch(y_ref, g_ref, p_ref):
        proj = jnp.dot(y_ref[...], p_ref[...], preferred_element_type=F32)
        return _sigmoid(g_ref[...].astype(F32)) * proj

    merged = (branch(yml_ref, gml_ref, pml_ref) + branch(yssm_ref, gssm_ref, pssm_ref)
              + branch(yfox_ref, gfox_ref, pfox_ref))
    o_ref[...] = x_ref[...] + jnp.dot(merged.astype(BF16), wout_ref[...], preferred_element_type=F32)


def _merge(x2, y_ml, y_ssm, y_fox, zm, p_ml, p_ssm, p_fox, w_out, *, col_g, tm=512):
    t, d = x2.shape
    rowblk = lambda c: pl.BlockSpec((tm, d), lambda i: (i, c))
    wfull = lambda a: pl.BlockSpec(a.shape, lambda i: (0, 0))
    gblk = col_g // d
    return pl.pallas_call(
        _merge_kernel,
        grid=(t // tm,),
        in_specs=[rowblk(0), rowblk(0), rowblk(0), rowblk(0),
                  rowblk(gblk), rowblk(gblk + 1), rowblk(gblk + 2),
                  wfull(p_ml), wfull(p_ssm), wfull(p_fox), wfull(w_out)],
        out_specs=rowblk(0),
        out_shape=jax.ShapeDtypeStruct((t, d), F32),
        compiler_params=_cparams(("parallel",)),
        name="merge",
    )(x2, y_ml, y_ssm, y_fox, zm, zm, zm, p_ml, p_ssm, p_fox, w_out)


def _erf(v):
    return lax.erf(v)


def _gelu_x2(v):
    return v * (1.0 + _erf(v * (2.0 ** -0.5)))


def _sorted_top(tiles):
    n = len(tiles)
    a = list(tiles)

    def keep_larger_first(i, j):
        a[i], a[j] = jnp.maximum(a[i], a[j]), jnp.minimum(a[i], a[j])

    def merge(lo_stride):
        j = lo_stride
        while j >= 1:
            for i in range(n):
                if i ^ j > i:
                    keep_larger_first(i, i ^ j)
            j //= 2

    k = 2
    while k <= n:
        j = k // 2
        while j >= 1:
            for i in range(n):
                l = i ^ j
                if l > i:
                    if i & k == 0:
                        keep_larger_first(i, l)
                    else:
                        keep_larger_first(l, i)
            j //= 2
        k *= 2
    shift = SUB // 2
    while shift >= 1:
        b = [pltpu.roll(x, shift, 0) for x in a]
        a = [jnp.maximum(a[j], b[n - 1 - j]) for j in range(n)]
        merge(n // 2)
        shift //= 2
    return a


def _as_rows(vals):
    sub = lax.broadcasted_iota(jnp.int32, (SUB, LANES), 0)
    tiles = []
    for g in range(len(vals) // SUB):
        t = vals[SUB * g]
        for r in range(1, SUB):
            t = jnp.where(sub == r, vals[SUB * g + r], t)
        tiles.append(t)
    return tiles


def _candidate_tiles(v1, v2, v1_rows, v2_rows, k):
    sub = lax.broadcasted_iota(jnp.int32, (SUB, LANES), 0)
    tiles = [v1[0] + v2_rows[g] for g in range(k // SUB)]
    a = 1
    while k // (a + 1) > 1:
        nb = k // (a + 1)
        assert nb <= SUB
        t = v1[a] + v2_rows[0]
        tiles.append(t if nb == SUB else jnp.where(sub < nb, t, -jnp.inf))
        a += 1
    assert a % SUB == 0
    tiles.extend(v1_rows[g] + v2[0] for g in range(a // SUB, k // SUB))
    assert len(tiles) <= 16
    tiles.extend([jnp.full((SUB, LANES), -jnp.inf, F32)] * (16 - len(tiles)))
    return tiles


def _peer_kernel(x_ref, g_ref, wq_ref, k1_ref, k2_ref, u_ref, vt_ref, o_ref,
                 ht_scr, q_scr, theta_scr, e1_scr, s2_scr, e2_scr, acc_scr, g0_scr, g1_scr,
                 w0_scr, w1_scr, *, tb, eb):
    j = pl.program_id(1)
    nk, topk, nhead = PEER_NKEYS, PEER_TOPK, PEER_HEADS
    assert nk // SUB == topk
    half = PEER_QDIM // 2
    n_lc = tb // LANES
    pk = SUB

    @pl.when(j == 0)
    def _():
        x = x_ref[...]
        hn32 = x * lax.rsqrt(jnp.mean(x * x, axis=-1, keepdims=True) + EPS) * g_ref[...]
        ht_scr[...] = hn32.T.astype(BF16)
        q = jnp.dot(hn32.astype(BF16), wq_ref[...], preferred_element_type=F32).astype(BF16)
        for c in range(q.shape[1] // half):
            q_scr[c] = q[:, c * half:(c + 1) * half]
        acc_scr[...] = jnp.zeros(acc_scr.shape, F32)
        g0_scr[...] = jnp.zeros(g0_scr.shape, BF16)
        w1_scr[...] = jnp.zeros(w1_scr.shape, F32)

        def head_body(h, carry):
            nt = (((1,), (1,)), ((), ()))
            s1 = lax.dot_general(k1_ref[h], q_scr[2 * h], nt, preferred_element_type=F32)
            s2 = lax.dot_general(k2_ref[h], q_scr[2 * h + 1], nt, preferred_element_type=F32)
            for lc in range(n_lc):
                ls = slice(lc * LANES, (lc + 1) * LANES)
                s1c = s1[:, ls].reshape(nk // SUB, SUB, LANES)
                s2c = s2[:, ls].reshape(nk // SUB, SUB, LANES)
                v1 = _sorted_top([s1c[i] for i in range(nk // SUB)])
                v2 = _sorted_top([s2c[i] for i in range(nk // SUB)])
                v1_rows, v2_rows = _as_rows(v1), _as_rows(v2)
                tv = _sorted_top(_candidate_tiles(v1, v2, v1_rows, v2_rows, topk))
                tau, mx = tv[topk - 1], tv[0]
                zsum = jnp.zeros_like(mx)
                for a in range(topk):
                    zsum = zsum + jnp.exp(tv[a] - mx)
                th_rows = [jnp.full((SUB, LANES), jnp.inf, F32) for _ in v1_rows]
                for b in range(topk):
                    th_rows = [jnp.where(v1_rows[g] + v2[b] >= tau, v2[b], th_rows[g])
                               for g in range(len(v1_rows))]
                theta = jnp.full(s1c.shape, jnp.inf, F32)
                for a in range(topk):
                    th_a = jnp.broadcast_to(th_rows[a // SUB][a % SUB:a % SUB + 1, :], (SUB, LANES))
                    theta = jnp.where(s1c == v1[a][None], th_a[None], theta)
                theta_scr[h, lc] = theta.reshape(nk, LANES)
                e1_scr[h, lc] = (jnp.exp(s1c - v1[0][None]) * (0.5 / zsum)[None]).reshape(nk, LANES)
                s2_scr[h, lc] = s2c
                e2_scr[h, lc] = jnp.exp(s2c - v2[0][None])
            return carry

        lax.fori_loop(0, nhead, head_body, 0)

    n_r = eb // nk
    n_e = pl.num_programs(1) - 2
    ja = jnp.minimum(j, n_e - 1)

    def stage_a(w_wr, lc, r_range):
        for r in r_range:
            i1 = ja * n_r + r
            wsum = None
            for h in range(nhead):
                th = jnp.broadcast_to(theta_scr[h, lc, pl.ds(i1, 1), :], (pk, LANES))
                e1 = jnp.broadcast_to(e1_scr[h, lc, pl.ds(i1, 1), :], (pk, LANES))
                term = jnp.where(s2_scr[h, lc] >= th[None], e2_scr[h, lc] * e1[None], 0.0)
                wsum = term if wsum is None else wsum + term
            w_wr[lc, r * (nk // pk):(r + 1) * (nk // pk)] = wsum

    def stage_b(w_rd, g_wr, n, m):
        ts_ = slice(n * 2 * LANES, (n + 1) * 2 * LANES)
        es = slice(m * (eb // 2), (m + 1) * (eb // 2))
        ps = slice(m * (eb // 2 // pk), (m + 1) * (eb // 2 // pk))
        act = _gelu_x2(jnp.dot(u_ref[es, :], ht_scr[:, ts_], preferred_element_type=F32))
        w_prev = jnp.concatenate([w_rd[2 * n + c, ps].reshape(eb // 2, LANES) for c in range(2)], axis=1)
        g_wr[es, ts_] = (act * w_prev).astype(BF16)

    def stage_c(g_rd, n, m):
        ts_ = slice(n * 2 * LANES, (n + 1) * 2 * LANES)
        ds_ = slice(m * (acc_scr.shape[0] // 2), (m + 1) * (acc_scr.shape[0] // 2))
        acc_scr[ds_, ts_] += jnp.dot(vt_ref[ds_, :], g_rd[:, ts_], preferred_element_type=F32)

    def step(w_wr, w_rd, g_wr, g_rd):
        for n in range(n_lc // 2):
            for m in range(2):
                stage_b(w_rd, g_wr, n, m)
                stage_a(w_wr, 2 * n + m, range(0, n_r // 2))
                stage_c(g_rd, n, m)
                stage_a(w_wr, 2 * n + m, range(n_r // 2, n_r))

    @pl.when(j % 2 == 0)
    def _():
        step(w0_scr, w1_scr, g1_scr, g0_scr)

    @pl.when(j % 2 == 1)
    def _():
        step(w1_scr, w0_scr, g0_scr, g1_scr)

    @pl.when(j == n_e + 1)
    def _():
        o_ref[...] = x_ref[...] + acc_scr[...].T


def _peer(x2, g, w_q, keys1, keys2, u, v, *, tb=512, eb=512):
    t, d = x2.shape
    tb = min(tb, t)
    ne = u.shape[0]
    nq = w_q.shape[1]
    n_e = ne // eb
    n_lc = tb // LANES
    pk = SUB
    v_t = v.reshape(n_e, eb, d).transpose(0, 2, 1).astype(BF16)
    once = pl.Buffered(1)
    return pl.pallas_call(
        functools.partial(_peer_kernel, tb=tb, eb=eb),
        grid=(t // tb, n_e + 2),
        in_specs=[
            pl.BlockSpec((tb, d), lambda i, j: (i, 0), pipeline_mode=once),
            pl.BlockSpec((1, d), lambda i, j: (0, 0)),
            pl.BlockSpec((d, nq), lambda i, j: (0, 0), pipeline_mode=once),
            pl.BlockSpec(keys1.shape, lambda i, j: (0, 0, 0)),
            pl.BlockSpec(keys2.shape, lambda i, j: (0, 0, 0)),
            pl.BlockSpec((eb, d), lambda i, j: (jnp.clip(j - 1, 0, n_e - 1), 0)),
            pl.BlockSpec((None, d, eb), lambda i, j: (jnp.clip(j - 2, 0, n_e - 1), 0, 0)),
        ],
        out_specs=pl.BlockSpec((tb, d), lambda i, j: (i, 0)),
        out_shape=jax.ShapeDtypeStruct((t, d), F32),
        scratch_shapes=[
            pltpu.VMEM((d, tb), BF16),
            pltpu.VMEM((2 * PEER_HEADS, tb, PEER_QDIM // 2), BF16),
            pltpu.VMEM((PEER_HEADS, n_lc, PEER_NKEYS, LANES), F32),
            pltpu.VMEM((PEER_HEADS, n_lc, PEER_NKEYS, LANES), F32),
            pltpu.VMEM((PEER_HEADS, n_lc, PEER_NKEYS // pk, pk, LANES), F32),
            pltpu.VMEM((PEER_HEADS, n_lc, PEER_NKEYS // pk, pk, LANES), F32),
            pltpu.VMEM((d, tb), F32),
            pltpu.VMEM((eb, tb), BF16),
            pltpu.VMEM((eb, tb), BF16),
            pltpu.VMEM((n_lc, eb // pk, pk, LANES), F32),
            pltpu.VMEM((n_lc, eb // pk, pk, LANES), F32),
        ],
        compiler_params=_cparams(("parallel", "arbitrary")),
        name="peer",
    )(x2, g, w_q, keys1, keys2, u, v_t)


def _ple_kernel(x_ref, p_ref, g_ref, wg_ref, wp_ref, fg_ref, o_ref, *, final):
    x = x_ref[...]
    hn = (x * lax.rsqrt(jnp.mean(x * x, axis=-1, keepdims=True) + EPS) * g_ref[...]).astype(BF16)
    gate = _sigmoid(jnp.dot(hn, wg_ref[...], preferred_element_type=F32))
    proj = jnp.dot(p_ref[...].astype(BF16), wp_ref[...], preferred_element_type=F32)
    y = x + gate * proj
    if final:
        y = y * lax.rsqrt(jnp.mean(y * y, axis=-1, keepdims=True) + EPS) * fg_ref[...]
    o_ref[...] = y


def _ple(x2, p2, g, w_gate, w_proj, final_g, *, final, tm=512):
    t, d = x2.shape
    pd = p2.shape[1]
    full = lambda a: pl.BlockSpec(a.shape, lambda i: (0, 0))
    return pl.pallas_call(
        functools.partial(_ple_kernel, final=final),
        grid=(t // tm,),
        in_specs=[pl.BlockSpec((tm, d), lambda i: (i, 0)), pl.BlockSpec((tm, pd), lambda i: (i, 0)),
                  full(g), full(w_gate), full(w_proj), full(final_g)],
        out_specs=pl.BlockSpec((tm, d), lambda i: (i, 0)),
        out_shape=jax.ShapeDtypeStruct((t, d), F32),
        compiler_params=_cparams(("parallel",)),
        name="ple",
    )(x2, p2, g, w_gate, w_proj, final_g)


def _row(v):
    return v.reshape(1, -1).astype(F32)


def _gate_bias_row(ml_b_i, ml_b_f, dt_bias, fox_b_f):
    r = jnp.zeros((LANES,), F32)
    r = r.at[COL_ML_I:COL_ML_I + ML_HEADS].set(ml_b_i.astype(F32))
    r = r.at[COL_ML_F:COL_ML_F + ML_HEADS].set(ml_b_f.astype(F32))
    r = r.at[COL_DT:COL_DT + SSM_HEADS].set(dt_bias.astype(F32))
    r = r.at[COL_FOX_F:COL_FOX_F + FOX_HEADS].set(fox_b_f.astype(F32))
    return r.reshape(1, LANES)


def _pad_rows(w, rows):
    return jnp.concatenate([w, jnp.zeros((rows - w.shape[0],) + w.shape[1:], w.dtype)], axis=0)


def kernel(x, p, norm_mix_g, w_in, ml_conv_w, ml_conv_b, ml_b_i, ml_b_f, ml_norm_g, ssm_conv_w, ssm_conv_b, ssm_dt_bias, ssm_a_log, ssm_d, ssm_norm_g, fox_q_norm_g, fox_k_norm_g, fox_b_f, w_branch_ml, w_branch_ssm, w_branch_fox, w_out, norm_ffn_g, peer_w_q, peer_keys1, peer_keys2, peer_u, peer_v, norm_ple_g, ple_w_gate, ple_w_proj, final_norm_g):
    bsz, seq, d = x.shape
    depth = w_in.shape[0]
    t = bsz * seq
    ml_qk = ML_HEADS * ML_DQK
    ml_w = ML_HEADS * ML_DV
    ssm_w = SSM_HEADS * SSM_HEADDIM
    ssm_bc = SSM_GROUPS * SSM_STATE
    fox_w = FOX_HEADS * FOX_HEADDIM
    splits = (ml_qk, ml_qk, ml_w, ml_w, ML_HEADS, ML_HEADS, ssm_w, ssm_w, ssm_bc, ssm_bc, SSM_HEADS,
              fox_w, fox_w, fox_w, FOX_HEADS, d, d, d)
    offs = [0]
    for s_ in splits:
        offs.append(offs[-1] + s_)
    (o_mlq, o_mlk, o_mlv, o_mlo, o_mli, o_mlf, o_sz, o_sx, o_sb, o_sc, o_sdt,
     o_fq, o_fk, o_fv, o_ff, o_gml, o_gssm, o_gfox) = offs[:-1]

    main_segments = [(o_mlq, 2 * ml_qk), (o_mlv, ml_w), (o_mlo, ml_w), (o_sz, ssm_w), (o_sx, ssm_w),
                     (o_fq, fox_w), (o_fk, fox_w), (o_fv, fox_w), (o_gml, d), (o_gssm, d), (o_gfox, d),
                     (o_sb, 2 * ssm_bc)]
    seg_off = [0]
    for _, wd in main_segments:
        seg_off.append(seg_off[-1] + wd)
    (c_mlqk, c_mlv, c_mlo, c_sz, c_sx, c_fq, c_fk, c_fv, c_gml, c_gssm, c_gfox, c_bc) = seg_off[:-1]
    assert (c_mlqk, c_mlv, c_mlo) == (0, ml_w, 2 * ml_w)

    x2 = x.reshape(t, d)
    for i in range(depth):
        w = w_in[i]
        w_main = jnp.concatenate([w[:, o:o + wd] for o, wd in main_segments], axis=1).astype(BF16)
        small = jnp.zeros((d, LANES), w.dtype)
        small = small.at[:, COL_ML_I:COL_ML_I + ML_HEADS].set(w[:, o_mli:o_mli + ML_HEADS])
        small = small.at[:, COL_ML_F:COL_ML_F + ML_HEADS].set(w[:, o_mlf:o_mlf + ML_HEADS])
        small = small.at[:, COL_DT:COL_DT + SSM_HEADS].set(w[:, o_sdt:o_sdt + SSM_HEADS])
        small = small.at[:, COL_FOX_F:COL_FOX_F + FOX_HEADS].set(w[:, o_ff:o_ff + FOX_HEADS])
        w_small = small.astype(BF16)
        gate_bias = _gate_bias_row(ml_b_i[i], ml_b_f[i], ssm_dt_bias[i], fox_b_f[i])

        zm, zs = _inproj(x2, _row(norm_mix_g[i]), w_main, w_small)

        y_ml = _mlstm(zm, zs, _pad_rows(ml_conv_w[i].astype(F32), 8), _row(ml_conv_b[i]), gate_bias,
                      _row(ml_norm_g[i]), bsz=bsz, seq=seq)

        scw = ssm_conv_w[i].astype(F32)
        scb = ssm_conv_b[i].astype(F32)
        alog_row = jnp.zeros((LANES,), F32).at[COL_DT:COL_DT + SSM_HEADS].set(
            ssm_a_log[i].astype(F32)).reshape(1, LANES)
        d_row = jnp.repeat(ssm_d[i].astype(F32), SSM_HEADDIM).reshape(1, ssm_w)
        y_ssm = _ssd(zm, zs, _pad_rows(scw[:, :ssm_w], 8), _row(scb[:ssm_w]),
                     _pad_rows(scw[:, ssm_w:], 8), _row(scb[ssm_w:]), gate_bias, alog_row, d_row,
                     _row(ssm_norm_g[i]), bsz=bsz, seq=seq, col_z=c_sz, col_x=c_sx, col_bc=c_bc)

        qn, kn, ft = _fox_prep(zm, zs, gate_bias, _row(fox_q_norm_g[i]), _row(fox_k_norm_g[i]),
                               bsz=bsz, seq=seq, col_q=c_fq, col_k=c_fk)
        y_fox = _fox_attn(qn, kn, zm, ft, bsz=bsz, seq=seq, col_v=c_fv)

        x2 = _merge(x2, y_ml, y_ssm, y_fox, zm, w_branch_ml[i].astype(BF16), w_branch_ssm[i].astype(BF16),
                    w_branch_fox[i].astype(BF16), w_out[i].astype(BF16), col_g=c_gml)

        x2 = _peer(x2, _row(norm_ffn_g[i]), peer_w_q[i].astype(BF16), peer_keys1[i].astype(BF16),
                   peer_keys2[i].astype(BF16), peer_u[i].astype(BF16), peer_v[i])

        x2 = _ple(x2, p[i].reshape(t, -1), _row(norm_ple_g[i]), ple_w_gate[i].astype(BF16),
                  ple_w_proj[i].astype(BF16), _row(final_norm_g), final=(i == depth - 1))
    return x2.reshape(bsz, seq, d)
```

```python
import functools
import math

import jax
import jax.numpy as jnp
from jax import lax
from jax.experimental import pallas as pl
from jax.experimental.pallas import tpu as pltpu

F32 = jnp.float32
BF16 = jnp.bfloat16
EPS = 1e-6
LOG2E = math.log2(math.e)

CONV_WIDTH = 4
ML_HEADS = 8
ML_DQK = 64
ML_DV = 128
SSM_HEADS = 16
SSM_HEADDIM = 64
SSM_GROUPS = 2
SSM_STATE = 128
FOX_HEADS = 8
FOX_HEADDIM = 128
PEER_HEADS = 8
PEER_NKEYS = 128
PEER_QDIM = 256
PEER_TOPK = 16

LANES = 128
SUB = 8
HALO = 8

COL_ML_I = 0
COL_ML_F = 8
COL_DT = 16
COL_FOX_F = 32

VMEM_LIMIT = 56 * 1024 * 1024


def _cparams(sem, flags=None):
    return pltpu.CompilerParams(dimension_semantics=sem, vmem_limit_bytes=VMEM_LIMIT, flags=flags)


def _log_sigmoid(v):
    return jnp.minimum(v, 0.0) - jnp.log1p(jnp.exp(-jnp.abs(v)))


def _sigmoid(v):
    return 1.0 / (1.0 + jnp.exp(-v))


def _silu(v):
    return v * _sigmoid(v)


def _softplus(v):
    return jnp.maximum(v, 0.0) + jnp.log1p(jnp.exp(-jnp.abs(v)))


def _tril(n):
    r = lax.broadcasted_iota(jnp.int32, (n, n), 0)
    c = lax.broadcasted_iota(jnp.int32, (n, n), 1)
    return r >= c


def _cumsum_rows(v):
    n = v.shape[0]
    return jnp.dot(_tril(n).astype(F32), v, precision=lax.Precision.HIGHEST,
                   preferred_element_type=F32)


def _inproj_kernel(x_ref, g_ref, wm_ref, ws_ref, zm_ref, zs_ref, h_scr):
    @pl.when(pl.program_id(1) == 0)
    def _():
        x = x_ref[...]
        ms = jnp.mean(x * x, axis=-1, keepdims=True)
        h = (x * lax.rsqrt(ms + EPS) * g_ref[...]).astype(BF16)
        h_scr[...] = h
        zs_ref[...] = jnp.dot(h, ws_ref[...], preferred_element_type=F32)

    zm_ref[...] = jnp.dot(h_scr[...], wm_ref[...], preferred_element_type=F32).astype(zm_ref.dtype)


def _inproj(x2, g, w_main, w_small, tm=2048, tn=512):
    t, d = x2.shape
    tm = min(tm, t)
    n = w_main.shape[1]
    return pl.pallas_call(
        _inproj_kernel,
        grid=(t // tm, n // tn),
        in_specs=[
            pl.BlockSpec((tm, d), lambda i, j: (i, 0)),
            pl.BlockSpec((1, d), lambda i, j: (0, 0)),
            pl.BlockSpec((d, tn), lambda i, j: (0, j)),
            pl.BlockSpec((d, LANES), lambda i, j: (0, 0)),
        ],
        out_specs=[
            pl.BlockSpec((tm, tn), lambda i, j: (i, j)),
            pl.BlockSpec((tm, LANES), lambda i, j: (i, 0)),
        ],
        out_shape=[jax.ShapeDtypeStruct((t, n), BF16), jax.ShapeDtypeStruct((t, LANES), F32)],
        scratch_shapes=[pltpu.VMEM((tm, d), BF16)],
        compiler_params=_cparams(("parallel", "arbitrary")),
        name="inproj",
    )(x2, g, w_main, w_small)


def _conv_chunk(buf_ref, w_ref, b_ref, start, length):
    acc = None
    for j in range(CONV_WIDTH):
        off = HALO - (CONV_WIDTH - 1) + j + start
        term = buf_ref[off:off + length, :] * w_ref[j:j + 1, :]
        acc = term if acc is None else acc + term
    return acc + b_ref[...]


def _mlstm_kernel(qk_ref, v_ref, o_ref, gs_ref, cw_ref, cb_ref, gb_ref, ng_ref, y_ref,
                  buf, c_scr, m_scr, *, ts, chunk):
    nh, dk, dv = ML_HEADS, ML_DQK, ML_DV
    qkw = nh * dk

    @pl.when(pl.program_id(1) == 0)
    def _():
        buf[0:HALO, :] = jnp.zeros((HALO, buf.shape[1]), F32)
        c_scr[...] = jnp.zeros(c_scr.shape, F32)
        m_scr[...] = jnp.zeros(m_scr.shape, F32)

    buf[HALO:HALO + ts, :] = qk_ref[...].astype(F32)
    causal = _tril(chunk)
    lane = lax.broadcasted_iota(jnp.int32, (chunk, LANES), 1)
    ones_col = (lane == 0).astype(BF16)

    for c in range(ts // chunk):
        r0 = c * chunk
        act = _silu(_conv_chunk(buf, cw_ref, cb_ref, r0, chunk))
        q_all = (act[:, :qkw] * (dk ** -0.5)).astype(BF16)
        kt_all = act[:, qkw:].T
        gates = gs_ref[r0:r0 + chunk, :] + gb_ref[...]
        bcum = _cumsum_rows(_log_sigmoid(gates))
        gates_t = gates.T
        bcum_t = bcum.T
        for h in range(nh):
            i_col = gates[:, COL_ML_I + h:COL_ML_I + h + 1]
            i_row = gates_t[COL_ML_I + h:COL_ML_I + h + 1, :]
            b_col = bcum[:, COL_ML_F + h:COL_ML_F + h + 1]
            b_row = bcum_t[COL_ML_F + h:COL_ML_F + h + 1, :]
            g_tot = b_col[chunk - 1:chunk, :]
            m_prev = m_scr[h:h + 1, 0:1]
            dmat = jnp.where(causal, b_col - b_row + i_row, -jnp.inf)
            inter = b_col + m_prev
            m_t = jnp.maximum(inter, jnp.max(dmat, axis=-1, keepdims=True))
            qh = q_all[:, h * dk:(h + 1) * dk]
            kt = kt_all[h * dk:(h + 1) * dk, :]
            s = jnp.dot(qh, kt.astype(BF16), preferred_element_type=F32)
            sc = (s * jnp.exp(dmat - m_t)).astype(BF16)
            w_inter = jnp.exp(inter - m_t)
            vaug = jnp.concatenate([v_ref[r0:r0 + chunk, h * dv:(h + 1) * dv], ones_col], axis=1)
            cmem = c_scr[h]
            tot = (jnp.dot(sc, vaug, preferred_element_type=F32)
                   + w_inter * jnp.dot(qh, cmem.astype(BF16), preferred_element_type=F32))
            num = tot[:, :dv]
            den = tot[:, dv:dv + 1]
            hv = num / jnp.maximum(jnp.abs(den), jnp.exp(-m_t))
            hn = hv * lax.rsqrt(jnp.mean(hv * hv, axis=-1, keepdims=True) + EPS)
            hn = hn * ng_ref[:, h * dv:(h + 1) * dv]
            og = _sigmoid(o_ref[r0:r0 + chunk, h * dv:(h + 1) * dv].astype(F32))
            y_ref[r0:r0 + chunk, h * dv:(h + 1) * dv] = (og * hn).astype(y_ref.dtype)
            a_row = g_tot - b_row + i_row
            m_new = jnp.maximum(g_tot + m_prev, jnp.max(a_row, axis=-1, keepdims=True))
            wa_row = jnp.exp(a_row - m_new)
            decay = jnp.exp(g_tot + m_prev - m_new)
            kw = (kt * wa_row).astype(BF16)
            c_scr[h] = decay * cmem + jnp.dot(kw, vaug, preferred_element_type=F32)
            m_scr[h:h + 1, :] = jnp.broadcast_to(m_new, (1, LANES))

    buf[0:HALO, :] = buf[ts:ts + HALO, :]


def _mlstm(zm, zs, conv_w, conv_b, gate_bias, norm_g, *, bsz, seq, ts=256, chunk=128):
    t = zm.shape[0]
    w = ML_HEADS * ML_DV
    nblk = seq // ts
    row = lambda b, s: b * nblk + s
    return pl.pallas_call(
        functools.partial(_mlstm_kernel, ts=ts, chunk=chunk),
        grid=(bsz, nblk),
        in_specs=[
            pl.BlockSpec((ts, w), lambda b, s: (row(b, s), 0)),
            pl.BlockSpec((ts, w), lambda b, s: (row(b, s), 1)),
            pl.BlockSpec((ts, w), lambda b, s: (row(b, s), 2)),
            pl.BlockSpec((ts, LANES), lambda b, s: (row(b, s), 0)),
            pl.BlockSpec(conv_w.shape, lambda b, s: (0, 0)),
            pl.BlockSpec(conv_b.shape, lambda b, s: (0, 0)),
            pl.BlockSpec(gate_bias.shape, lambda b, s: (0, 0)),
            pl.BlockSpec(norm_g.shape, lambda b, s: (0, 0)),
        ],
        out_specs=pl.BlockSpec((ts, w), lambda b, s: (row(b, s), 0)),
        out_shape=jax.ShapeDtypeStruct((t, w), BF16),
        scratch_shapes=[
            pltpu.VMEM((HALO + ts, w), F32),
            pltpu.VMEM((ML_HEADS, ML_DQK, 2 * ML_DV), F32),
            pltpu.VMEM((ML_HEADS, LANES), F32),
        ],
        compiler_params=_cparams(("parallel", "arbitrary")),
        name="mlstm",
    )(zm, zm, zm, zs, conv_w, conv_b, gate_bias, norm_g)


def _ssd_kernel(z_ref, x_ref, bc_ref, gs_ref, cwx_ref, cbx_ref, cwbc_ref, cbbc_ref, gb_ref,
                alog_ref, d_ref, ng_ref, y_ref, bufx, bufbc, st_scr, *, ts, chunk):
    ng, nst, p = SSM_GROUPS, SSM_STATE, SSM_HEADDIM
    hg = SSM_HEADS // ng
    gw = hg * p
    pairs_per_group = gw // LANES

    @pl.when(pl.program_id(1) == 0)
    def _():
        bufx[0:HALO, :] = jnp.zeros((HALO, bufx.shape[1]), F32)
        bufbc[0:HALO, :] = jnp.zeros((HALO, bufbc.shape[1]), F32)
        st_scr[...] = jnp.zeros(st_scr.shape, F32)

    bufx[HALO:HALO + ts, :] = x_ref[...].astype(F32)
    bufbc[HALO:HALO + ts, :] = bc_ref[...].astype(F32)
    causal = _tril(chunk)
    low_half = lax.broadcasted_iota(jnp.int32, (chunk, LANES), 1) < p
    a_row_all = -jnp.exp(alog_ref[...])

    for c in range(ts // chunk):
        r0 = c * chunk
        xa = _silu(_conv_chunk(bufx, cwx_ref, cbx_ref, r0, chunk))
        bca = _silu(_conv_chunk(bufbc, cwbc_ref, cbbc_ref, r0, chunk))
        dt = _softplus(gs_ref[r0:r0 + chunk, :] + gb_ref[...])
        acum = _cumsum_rows(dt * a_row_all)
        acum_t = acum.T
        y_parts = []
        for g in range(ng):
            bg = bca[:, g * nst:(g + 1) * nst]
            cg = bca[:, ng * nst + g * nst:ng * nst + (g + 1) * nst].astype(BF16)
            bg_t = bg.T
            cbt = jnp.dot(cg, bg_t.astype(BF16), preferred_element_type=F32)
            state = st_scr[g]
            inter = jnp.dot(cg, state.astype(BF16), preferred_element_type=F32)
            xs_scaled, last_parts = [], []
            for pp in range(pairs_per_group):
                pidx = g * pairs_per_group + pp
                ha = 2 * pidx
                ca, cb = COL_DT + ha, COL_DT + ha + 1
                sl = slice(pidx * LANES, (pidx + 1) * LANES)
                xa_p = xa[:, sl]
                dt_pair = jnp.where(low_half, dt[:, ca:ca + 1], dt[:, cb:cb + 1])
                ac_pair = jnp.where(low_half, acum[:, ca:ca + 1], acum[:, cb:cb + 1])
                xs_p = xa_p * dt_pair
                xs_b = xs_p.astype(BF16)
                lm_a = jnp.exp(jnp.where(causal, acum[:, ca:ca + 1] - acum_t[ca:ca + 1, :], -jnp.inf))
                lm_b = jnp.exp(jnp.where(causal, acum[:, cb:cb + 1] - acum_t[cb:cb + 1, :], -jnp.inf))
                ya = jnp.dot((cbt * lm_a).astype(BF16), xs_b, preferred_element_type=F32)
                yb = jnp.dot((cbt * lm_b).astype(BF16), xs_b, preferred_element_type=F32)
                y_p = (jnp.where(low_half, ya, yb)
                       + inter[:, pp * LANES:(pp + 1) * LANES] * jnp.exp(ac_pair)
                       + d_ref[:, sl] * xa_p)
                y_parts.append(y_p)
                last = ac_pair[chunk - 1:chunk, :]
                xs_scaled.append((xs_p * jnp.exp(last - ac_pair)).astype(BF16))
                last_parts.append(last)
            xs_g = jnp.concatenate(xs_scaled, axis=1)
            last_g = jnp.concatenate(last_parts, axis=1)
            st_scr[g] = jnp.exp(last_g) * state + jnp.dot(bg_t.astype(BF16), xs_g,
                                                          preferred_element_type=F32)
        ys = jnp.concatenate(y_parts, axis=1)
        ys = ys * _silu(z_ref[r0:r0 + chunk, :].astype(F32))
        outs = []
        for g in range(ng):
            yg = ys[:, g * gw:(g + 1) * gw]
            yn = yg * lax.rsqrt(jnp.mean(yg * yg, axis=-1, keepdims=True) + EPS)
            outs.append(yn * ng_ref[:, g * gw:(g + 1) * gw])
        y_ref[r0:r0 + chunk, :] = jnp.concatenate(outs, axis=1).astype(y_ref.dtype)

    bufx[0:HALO, :] = bufx[ts:ts + HALO, :]
    bufbc[0:HALO, :] = bufbc[ts:ts + HALO, :]


def _ssd(zm, zs, cwx, cbx, cwbc, cbbc, gate_bias, alog_row, d_row, norm_g, *, bsz, seq,
         col_z, col_x, col_bc, ts=256, chunk=128):
    t = zm.shape[0]
    w = SSM_HEADS * SSM_HEADDIM
    bcw = 2 * SSM_GROUPS * SSM_STATE
    nblk = seq // ts
    row = lambda b, s: b * nblk + s
    full = lambda a: pl.BlockSpec(a.shape, lambda b, s: (0, 0))
    return pl.pallas_call(
        functools.partial(_ssd_kernel, ts=ts, chunk=chunk),
        grid=(bsz, nblk),
        in_specs=[
            pl.BlockSpec((ts, w), lambda b, s: (row(b, s), col_z // w)),
            pl.BlockSpec((ts, w), lambda b, s: (row(b, s), col_x // w)),
            pl.BlockSpec((ts, bcw), lambda b, s: (row(b, s), col_bc // bcw)),
            pl.BlockSpec((ts, LANES), lambda b, s: (row(b, s), 0)),
            full(cwx), full(cbx), full(cwbc), full(cbbc), full(gate_bias),
            full(alog_row), full(d_row), full(norm_g),
        ],
        out_specs=pl.BlockSpec((ts, w), lambda b, s: (row(b, s), 0)),
        out_shape=jax.ShapeDtypeStruct((t, w), BF16),
        scratch_shapes=[
            pltpu.VMEM((HALO + ts, w), F32),
            pltpu.VMEM((HALO + ts, bcw), F32),
            pltpu.VMEM((SSM_GROUPS, SSM_STATE, w // SSM_GROUPS), F32),
        ],
        compiler_params=_cparams(("parallel", "arbitrary")),
        name="ssd",
    )(zm, zm, zm, zs, cwx, cbx, cwbc, cbbc, gate_bias, alog_row, d_row, norm_g)


def _fox_prep_kernel(q_ref, k_ref, gs_ref, gb_ref, gq_ref, gk_ref, qn_ref, kn_ref, ft_ref,
                     carry, *, ts):
    d = FOX_HEADDIM

    @pl.when(pl.program_id(1) == 0)
    def _():
        carry[...] = jnp.zeros(carry.shape, F32)

    for h in range(FOX_HEADS):
        sl = slice(h * d, (h + 1) * d)
        qh = q_ref[:, sl].astype(F32)
        qn = qh * lax.rsqrt(jnp.mean(qh * qh, axis=-1, keepdims=True) + EPS) * gq_ref[...]
        qn_ref[:, sl] = qn.astype(qn_ref.dtype)
        kh = k_ref[:, sl].astype(F32)
        kn = kh * lax.rsqrt(jnp.mean(kh * kh, axis=-1, keepdims=True) + EPS) * gk_ref[...]
        kn_ref[:, sl] = (kn * (d ** -0.5 * LOG2E)).astype(kn_ref.dtype)

    lf = _log_sigmoid(gs_ref[...] + gb_ref[...])
    fc = _cumsum_rows(lf) + carry[0:1, :]
    carry[0:1, :] = fc[ts - 1:ts, :]
    ft_ref[...] = fc.T[COL_FOX_F:COL_FOX_F + FOX_HEADS, :] * LOG2E


def _fox_prep(zm, zs, gate_bias, gq, gk, *, bsz, seq, col_q, col_k, ts=256):
    t = zm.shape[0]
    w = FOX_HEADS * FOX_HEADDIM
    nblk = seq // ts
    row = lambda b, s: b * nblk + s
    full = lambda a: pl.BlockSpec(a.shape, lambda b, s: (0, 0))
    return pl.pallas_call(
        functools.partial(_fox_prep_kernel, ts=ts),
        grid=(bsz, nblk),
        in_specs=[
            pl.BlockSpec((ts, w), lambda b, s: (row(b, s), col_q // w)),
            pl.BlockSpec((ts, w), lambda b, s: (row(b, s), col_k // w)),
            pl.BlockSpec((ts, LANES), lambda b, s: (row(b, s), 0)),
            full(gate_bias), full(gq), full(gk),
        ],
        out_specs=[
            pl.BlockSpec((ts, w), lambda b, s: (row(b, s), 0)),
            pl.BlockSpec((ts, w), lambda b, s: (row(b, s), 0)),
            pl.BlockSpec((None, FOX_HEADS, ts), lambda b, s: (b, 0, s)),
        ],
        out_shape=[jax.ShapeDtypeStruct((t, w), BF16), jax.ShapeDtypeStruct((t, w), BF16),
                   jax.ShapeDtypeStruct((bsz, FOX_HEADS, seq), F32)],
        scratch_shapes=[pltpu.VMEM((8, LANES), F32)],
        compiler_params=_cparams(("parallel", "arbitrary")),
        name="fox_prep",
    )(zm, zm, zs, gate_bias, gq, gk)


def _fox_attn_kernel(qi_ref, kj_ref, q_ref, k_ref, v_ref, f_ref, o_ref, m_scr, acc_scr, vaug_scr,
                     *, tq, tk, rb, rb_diag):
    pr = pl.program_id(2)
    qi = qi_ref[pr]
    kj = kj_ref[pr]
    h = pl.program_id(1)
    d = FOX_HEADDIM
    ratio = tq // tk
    per_tile = tk // rb

    @pl.when(kj == 0)
    def _():
        m_scr[...] = jnp.full(m_scr.shape, -jnp.inf, F32)
        acc_scr[...] = jnp.zeros(acc_scr.shape, F32)

    vaug_scr[:, :d] = v_ref[...]
    vaug_scr[:, d:] = (lax.broadcasted_iota(jnp.int32, (tk, d), 1) == 0).astype(BF16)
    fk_all = f_ref[pl.ds(h, 1), :]

    def row_block(row0, nrows, diag_row0):
        rows = slice(row0, row0 + nrows)
        ncol = tk if diag_row0 is None else diag_row0 + nrows
        s = lax.dot_general(q_ref[rows, :], k_ref[:ncol, :], (((1,), (1,)), ((), ())),
                            preferred_element_type=F32) - fk_all[:, :ncol]
        if diag_row0 is not None:
            tail = jnp.where(_tril(nrows), s[:, ncol - nrows:], -jnp.inf)
            s = tail if ncol == nrows else jnp.concatenate([s[:, :ncol - nrows], tail], axis=1)
        m_old = m_scr[rows, :]
        m_new = jnp.maximum(m_old, jnp.max(s, axis=-1, keepdims=True))
        alpha = jnp.exp2(m_old - m_new)
        p = jnp.exp2(s - m_new[:, :1]).astype(BF16)
        pv = jnp.dot(p, vaug_scr[:ncol, :], preferred_element_type=F32)
        acc_scr[rows, :] = jnp.concatenate([alpha, alpha], axis=1) * acc_scr[rows, :] + pv
        m_scr[rows, :] = m_new

    @pl.when(kj < qi * ratio)
    def _():
        for r in range(tq // rb):
            row_block(r * rb, rb, None)

    for c in range(ratio):
        @pl.when(kj == qi * ratio + c)
        def _(c=c):
            for r in range(tk // rb_diag):
                row_block(c * tk + r * rb_diag, rb_diag, r * rb_diag)
            for r in range((c + 1) * per_tile, tq // rb):
                row_block(r * rb, rb, None)

    @pl.when(kj == qi * ratio + ratio - 1)
    def _():
        acc = acc_scr[...]
        o_ref[...] = (acc[:, :d] / acc[:, d:d + 1]).astype(o_ref.dtype)


def _fox_attn(qn, kn, zm, ft, *, bsz, seq, col_v, tk=2048, ratio=1, rb=128, rb_diag=512):
    t = qn.shape[0]
    d = FOX_HEADDIM
    tk = min(tk, seq // ratio)
    tq = ratio * tk
    rb = min(rb, tk)
    rb_diag = min(rb_diag, tk)
    nq = seq // tq
    nkt = seq // tk
    qi_tab, kj_tab = [], []
    for i in range(nq):
        for j in range(ratio * (i + 1)):
            qi_tab.append(i)
            kj_tab.append(j)
    qi_tab = jnp.asarray(qi_tab, jnp.int32)
    kj_tab = jnp.asarray(kj_tab, jnp.int32)
    vblk = col_v // d
    grid_spec = pltpu.PrefetchScalarGridSpec(
        num_scalar_prefetch=2,
        grid=(bsz, FOX_HEADS, qi_tab.shape[0]),
        in_specs=[
            pl.BlockSpec((tq, d), lambda b, h, p, qi, kj: (b * nq + qi[p], h)),
            pl.BlockSpec((tk, d), lambda b, h, p, qi, kj: (b * nkt + kj[p], h)),
            pl.BlockSpec((tk, d), lambda b, h, p, qi, kj: (b * nkt + kj[p], vblk + h)),
            pl.BlockSpec((None, FOX_HEADS, tk), lambda b, h, p, qi, kj: (b, 0, kj[p])),
        ],
        out_specs=pl.BlockSpec((tq, d), lambda b, h, p, qi, kj: (b * nq + qi[p], h)),
        scratch_shapes=[pltpu.VMEM((tq, LANES), F32), pltpu.VMEM((tq, 2 * d), F32),
                        pltpu.VMEM((tk, 2 * d), BF16)],
    )
    return pl.pallas_call(
        functools.partial(_fox_attn_kernel, tq=tq, tk=tk, rb=rb, rb_diag=rb_diag),
        grid_spec=grid_spec,
        out_shape=jax.ShapeDtypeStruct((t, FOX_HEADS * d), BF16),
        compiler_params=_cparams(("parallel", "parallel", "arbitrary")),
        name="fox_attn",
    )(qi_tab, kj_tab, qn, kn, zm, ft)


def _merge_kernel(x_ref, yml_ref, yssm_ref, yfox_ref, gml_ref, gssm_ref, gfox_ref,
                  pml_ref, pssm_ref, pfox_ref, wout_ref, o_ref):
    def branch(y_ref, g_ref, p_ref):
        proj = jnp.dot(y_ref[...], p_ref[...], preferred_element_type=F32)
        return _sigmoid(g_ref[...].astype(F32)) * proj

    merged = (branch(yml_ref, gml_ref, pml_ref) + branch(yssm_ref, gssm_ref, pssm_ref)
              + branch(yfox_ref, gfox_ref, pfox_ref))
    o_ref[...] = x_ref[...] + jnp.dot(merged.astype(BF16), wout_ref[...], preferred_element_type=F32)


def _merge(x2, y_ml, y_ssm, y_fox, zm, p_ml, p_ssm, p_fox, w_out, *, col_g, tm=512):
    t, d = x2.shape
    rowblk = lambda c: pl.BlockSpec((tm, d), lambda i: (i, c))
    wfull = lambda a: pl.BlockSpec(a.shape, lambda i: (0, 0))
    gblk = col_g // d
    return pl.pallas_call(
        _merge_kernel,
        grid=(t // tm,),
        in_specs=[rowblk(0), rowblk(0), rowblk(0), rowblk(0),
                  rowblk(gblk), rowblk(gblk + 1), rowblk(gblk + 2),
                  wfull(p_ml), wfull(p_ssm), wfull(p_fox), wfull(w_out)],
        out_specs=rowblk(0),
        out_shape=jax.ShapeDtypeStruct((t, d), F32),
        compiler_params=_cparams(("parallel",)),
        name="merge",
    )(x2, y_ml, y_ssm, y_fox, zm, zm, zm, p_ml, p_ssm, p_fox, w_out)


def _erf(v):
    return lax.erf(v)


def _gelu_x2(v):
    return v * (1.0 + _erf(v * (2.0 ** -0.5)))


def _sorted_top(tiles):
    n = len(tiles)
    a = list(tiles)

    def keep_larger_first(i, j):
        a[i], a[j] = jnp.maximum(a[i], a[j]), jnp.minimum(a[i], a[j])

    def merge(lo_stride):
        j = lo_stride
        while j >= 1:
            for i in range(n):
                if i ^ j > i:
                    keep_larger_first(i, i ^ j)
            j //= 2

    k = 2
    while k <= n:
        j = k // 2
        while j >= 1:
            for i in range(n):
                l = i ^ j
                if l > i:
                    if i & k == 0:
                        keep_larger_first(i, l)
                    else:
                        keep_larger_first(l, i)
            j //= 2
        k *= 2
    shift = SUB // 2
    while shift >= 1:
        b = [pltpu.roll(x, shift, 0) for x in a]
        a = [jnp.maximum(a[j], b[n - 1 - j]) for j in range(n)]
        merge(n // 2)
        shift //= 2
    return a


def _as_rows(vals):
    sub = lax.broadcasted_iota(jnp.int32, (SUB, LANES), 0)
    tiles = []
    for g in range(len(vals) // SUB):
        t = vals[SUB * g]
        for r in range(1, SUB):
            t = jnp.where(sub == r, vals[SUB * g + r], t)
        tiles.append(t)
    return tiles


def _candidate_tiles(v1, v2, v1_rows, v2_rows, k):
    sub = lax.broadcasted_iota(jnp.int32, (SUB, LANES), 0)
    tiles = [v1[0] + v2_rows[g] for g in range(k // SUB)]
    a = 1
    while k // (a + 1) > 1:
        nb = k // (a + 1)
        assert nb <= SUB
        t = v1[a] + v2_rows[0]
        tiles.append(t if nb == SUB else jnp.where(sub < nb, t, -jnp.inf))
        a += 1
    assert a % SUB == 0
    tiles.extend(v1_rows[g] + v2[0] for g in range(a // SUB, k // SUB))
    assert len(tiles) <= 16
    tiles.extend([jnp.full((SUB, LANES), -jnp.inf, F32)] * (16 - len(tiles)))
    return tiles


def _peer_kernel(x_ref, g_ref, wq_ref, k1_ref, k2_ref, u_ref, vt_ref, o_ref,
                 ht_scr, q_scr, theta_scr, e1_scr, s2_scr, e2_scr, acc_scr, g0_scr, g1_scr,
                 w0_scr, w1_scr, *, tb, eb, n_e):
    j = pl.program_id(1)
    nk, topk, nhead = PEER_NKEYS, PEER_TOPK, PEER_HEADS
    assert nk // SUB == topk
    half = PEER_QDIM // 2
    n_lc = tb // LANES
    pk = SUB

    @pl.when(j == 0)
    def _():
        x = x_ref[...]
        hn32 = x * lax.rsqrt(jnp.mean(x * x, axis=-1, keepdims=True) + EPS) * g_ref[...]
        ht_scr[...] = hn32.T.astype(BF16)
        q = jnp.dot(hn32.astype(BF16), wq_ref[...], preferred_element_type=F32).astype(BF16)
        for c in range(q.shape[1] // half):
            q_scr[c] = q[:, c * half:(c + 1) * half]
        acc_scr[...] = jnp.zeros(acc_scr.shape, F32)

        def head_body(h, carry):
            nt = (((1,), (1,)), ((), ()))
            s1 = lax.dot_general(k1_ref[h], q_scr[2 * h], nt, preferred_element_type=F32)
            s2 = lax.dot_general(k2_ref[h], q_scr[2 * h + 1], nt, preferred_element_type=F32)
            for lc in range(n_lc):
                ls = slice(lc * LANES, (lc + 1) * LANES)
                s1c = s1[:, ls].reshape(nk // SUB, SUB, LANES)
                s2c = s2[:, ls].reshape(nk // SUB, SUB, LANES)
                v1 = _sorted_top([s1c[i] for i in range(nk // SUB)])
                v2 = _sorted_top([s2c[i] for i in range(nk // SUB)])
                v1_rows, v2_rows = _as_rows(v1), _as_rows(v2)
                tv = _sorted_top(_candidate_tiles(v1, v2, v1_rows, v2_rows, topk))
                tau, mx = tv[topk - 1], tv[0]
                zsum = jnp.zeros_like(mx)
                for a in range(topk):
                    zsum = zsum + jnp.exp(tv[a] - mx)
                th_rows = [jnp.full((SUB, LANES), jnp.inf, F32) for _ in v1_rows]
                for b in range(topk):
                    th_rows = [jnp.where(v1_rows[g] + v2[b] >= tau, v2[b], th_rows[g])
                               for g in range(len(v1_rows))]
                theta = jnp.full(s1c.shape, jnp.inf, F32)
                for a in range(topk):
                    th_a = jnp.broadcast_to(th_rows[a // SUB][a % SUB:a % SUB + 1, :], (SUB, LANES))
                    theta = jnp.where(s1c == v1[a][None], th_a[None], theta)
                theta_scr[h, lc] = theta.reshape(nk, LANES)
                e1_scr[h, lc] = (jnp.exp(s1c - v1[0][None]) * (0.5 / zsum)[None]).reshape(nk, LANES)
                s2_scr[h, lc] = s2c
                e2_scr[h, lc] = jnp.exp(s2c - v2[0][None])
            return carry

        lax.fori_loop(0, nhead, head_body, 0)

    n_r = eb // nk

    def stage_a(w_wr, lc, r_range):
        for r in r_range:
            i1 = j * n_r + r
            wsum = None
            for h in range(nhead):
                th = jnp.broadcast_to(theta_scr[h, lc, pl.ds(i1, 1), :], (pk, LANES))
                e1 = jnp.broadcast_to(e1_scr[h, lc, pl.ds(i1, 1), :], (pk, LANES))
                term = jnp.where(s2_scr[h, lc] >= th[None], e2_scr[h, lc] * e1[None], 0.0)
                wsum = term if wsum is None else wsum + term
            w_wr[lc, r * (nk // pk):(r + 1) * (nk // pk)] = wsum

    def stage_b(w_rd, g_wr, n, m):
        ts_ = slice(n * 2 * LANES, (n + 1) * 2 * LANES)
        es = slice(m * (eb // 2), (m + 1) * (eb // 2))
        ps = slice(m * (eb // 2 // pk), (m + 1) * (eb // 2 // pk))
        act = _gelu_x2(jnp.dot(u_ref[es, :], ht_scr[:, ts_], preferred_element_type=F32))
        w_prev = jnp.concatenate([w_rd[2 * n + c, ps].reshape(eb // 2, LANES) for c in range(2)], axis=1)
        g_wr[es, ts_] = (act * w_prev).astype(BF16)

    def stage_c(g_rd, n, m):
        ts_ = slice(n * 2 * LANES, (n + 1) * 2 * LANES)
        ds_ = slice(m * (acc_scr.shape[0] // 2), (m + 1) * (acc_scr.shape[0] // 2))
        acc_scr[ds_, ts_] += jnp.dot(vt_ref[ds_, :], g_rd[:, ts_], preferred_element_type=F32)

    def step(parity, do_a=True, do_b=True, do_c=True):
        w_wr, w_rd, g_wr, g_rd = ((w0_scr, w1_scr, g1_scr, g0_scr) if parity == 0
                                  else (w1_scr, w0_scr, g0_scr, g1_scr))
        for n in range(n_lc // 2):
            for m in range(2):
                if do_b:
                    stage_b(w_rd, g_wr, n, m)
                if do_a:
                    stage_a(w_wr, 2 * n + m, range(0, n_r // 2))
                if do_c:
                    stage_c(g_rd, n, m)
                if do_a:
                    stage_a(w_wr, 2 * n + m, range(n_r // 2, n_r))

    @pl.when(j == 0)
    def _():
        step(0, do_b=False, do_c=False)

    @pl.when(j == 1)
    def _():
        step(1, do_c=False)

    steady = jnp.logical_and(j >= 2, j < n_e)

    @pl.when(jnp.logical_and(steady, j % 2 == 0))
    def _():
        step(0)

    @pl.when(jnp.logical_and(steady, j % 2 == 1))
    def _():
        step(1)

    @pl.when(j == n_e)
    def _():
        step(n_e % 2, do_a=False)

    @pl.when(j == n_e + 1)
    def _():
        step((n_e + 1) % 2, do_a=False, do_b=False)

    @pl.when(j == n_e + 1)
    def _():
        o_ref[...] = x_ref[...] + acc_scr[...].T


def _peer(x2, g, w_q, keys1, keys2, u, v, *, tb=512, eb=512):
    t, d = x2.shape
    tb = min(tb, t)
    ne = u.shape[0]
    nq = w_q.shape[1]
    n_e = ne // eb
    n_lc = tb // LANES
    pk = SUB
    v_t = v.reshape(n_e, eb, d).transpose(0, 2, 1).astype(BF16)
    once = pl.Buffered(1)
    return pl.pallas_call(
        functools.partial(_peer_kernel, tb=tb, eb=eb, n_e=n_e),
        grid=(t // tb, n_e + 2),
        in_specs=[
            pl.BlockSpec((tb, d), lambda i, j: (i, 0), pipeline_mode=once),
            pl.BlockSpec((1, d), lambda i, j: (0, 0)),
            pl.BlockSpec((d, nq), lambda i, j: (0, 0), pipeline_mode=once),
            pl.BlockSpec(keys1.shape, lambda i, j: (0, 0, 0)),
            pl.BlockSpec(keys2.shape, lambda i, j: (0, 0, 0)),
            pl.BlockSpec((eb, d), lambda i, j: (jnp.clip(j - 1, 0, n_e - 1), 0)),
            pl.BlockSpec((None, d, eb), lambda i, j: (jnp.clip(j - 2, 0, n_e - 1), 0, 0)),
        ],
        out_specs=pl.BlockSpec((tb, d), lambda i, j: (i, 0)),
        out_shape=jax.ShapeDtypeStruct((t, d), F32),
        scratch_shapes=[
            pltpu.VMEM((d, tb), BF16),
            pltpu.VMEM((2 * PEER_HEADS, tb, PEER_QDIM // 2), BF16),
            pltpu.VMEM((PEER_HEADS, n_lc, PEER_NKEYS, LANES), F32),
            pltpu.VMEM((PEER_HEADS, n_lc, PEER_NKEYS, LANES), F32),
            pltpu.VMEM((PEER_HEADS, n_lc, PEER_NKEYS // pk, pk, LANES), F32),
            pltpu.VMEM((PEER_HEADS, n_lc, PEER_NKEYS // pk, pk, LANES), F32),
            pltpu.VMEM((d, tb), F32),
            pltpu.VMEM((eb, tb), BF16),
            pltpu.VMEM((eb, tb), BF16),
            pltpu.VMEM((n_lc, eb // pk, pk, LANES), F32),
            pltpu.VMEM((n_lc, eb // pk, pk, LANES), F32),
        ],
        compiler_params=_cparams(("parallel", "arbitrary")),
        name="peer",
    )(x2, g, w_q, keys1, keys2, u, v_t)


def _ple_kernel(x_ref, p_ref, g_ref, wg_ref, wp_ref, fg_ref, o_ref, *, final):
    x = x_ref[...]
    hn = (x * lax.rsqrt(jnp.mean(x * x, axis=-1, keepdims=True) + EPS) * g_ref[...]).astype(BF16)
    gate = _sigmoid(jnp.dot(hn, wg_ref[...], preferred_element_type=F32))
    proj = jnp.dot(p_ref[...].astype(BF16), wp_ref[...], preferred_element_type=F32)
    y = x + gate * proj
    if final:
        y = y * lax.rsqrt(jnp.mean(y * y, axis=-1, keepdims=True) + EPS) * fg_ref[...]
    o_ref[...] = y


def _ple(x2, p2, g, w_gate, w_proj, final_g, *, final, tm=512):
    t, d = x2.shape
    pd = p2.shape[1]
    full = lambda a: pl.BlockSpec(a.shape, lambda i: (0, 0))
    return pl.pallas_call(
        functools.partial(_ple_kernel, final=final),
        grid=(t // tm,),
        in_specs=[pl.BlockSpec((tm, d), lambda i: (i, 0)), pl.BlockSpec((tm, pd), lambda i: (i, 0)),
                  full(g), full(w_gate), full(w_proj), full(final_g)],
        out_specs=pl.BlockSpec((tm, d), lambda i: (i, 0)),
        out_shape=jax.ShapeDtypeStruct((t, d), F32),
        compiler_params=_cparams(("parallel",)),
        name="ple",
    )(x2, p2, g, w_gate, w_proj, final_g)


def _row(v):
    return v.reshape(1, -1).astype(F32)


def _gate_bias_row(ml_b_i, ml_b_f, dt_bias, fox_b_f):
    r = jnp.zeros((LANES,), F32)
    r = r.at[COL_ML_I:COL_ML_I + ML_HEADS].set(ml_b_i.astype(F32))
    r = r.at[COL_ML_F:COL_ML_F + ML_HEADS].set(ml_b_f.astype(F32))
    r = r.at[COL_DT:COL_DT + SSM_HEADS].set(dt_bias.astype(F32))
    r = r.at[COL_FOX_F:COL_FOX_F + FOX_HEADS].set(fox_b_f.astype(F32))
    return r.reshape(1, LANES)


def _pad_rows(w, rows):
    return jnp.concatenate([w, jnp.zeros((rows - w.shape[0],) + w.shape[1:], w.dtype)], axis=0)


def kernel(x, p, norm_mix_g, w_in, ml_conv_w, ml_conv_b, ml_b_i, ml_b_f, ml_norm_g, ssm_conv_w, ssm_conv_b, ssm_dt_bias, ssm_a_log, ssm_d, ssm_norm_g, fox_q_norm_g, fox_k_norm_g, fox_b_f, w_branch_ml, w_branch_ssm, w_branch_fox, w_out, norm_ffn_g, peer_w_q, peer_keys1, peer_keys2, peer_u, peer_v, norm_ple_g, ple_w_gate, ple_w_proj, final_norm_g):
    bsz, seq, d = x.shape
    depth = w_in.shape[0]
    t = bsz * seq
    ml_qk = ML_HEADS * ML_DQK
    ml_w = ML_HEADS * ML_DV
    ssm_w = SSM_HEADS * SSM_HEADDIM
    ssm_bc = SSM_GROUPS * SSM_STATE
    fox_w = FOX_HEADS * FOX_HEADDIM
    splits = (ml_qk, ml_qk, ml_w, ml_w, ML_HEADS, ML_HEADS, ssm_w, ssm_w, ssm_bc, ssm_bc, SSM_HEADS,
              fox_w, fox_w, fox_w, FOX_HEADS, d, d, d)
    offs = [0]
    for s_ in splits:
        offs.append(offs[-1] + s_)
    (o_mlq, o_mlk, o_mlv, o_mlo, o_mli, o_mlf, o_sz, o_sx, o_sb, o_sc, o_sdt,
     o_fq, o_fk, o_fv, o_ff, o_gml, o_gssm, o_gfox) = offs[:-1]

    main_segments = [(o_mlq, 2 * ml_qk), (o_mlv, ml_w), (o_mlo, ml_w), (o_sz, ssm_w), (o_sx, ssm_w),
                     (o_fq, fox_w), (o_fk, fox_w), (o_fv, fox_w), (o_gml, d), (o_gssm, d), (o_gfox, d),
                     (o_sb, 2 * ssm_bc)]
    seg_off = [0]
    for _, wd in main_segments:
        seg_off.append(seg_off[-1] + wd)
    (c_mlqk, c_mlv, c_mlo, c_sz, c_sx, c_fq, c_fk, c_fv, c_gml, c_gssm, c_gfox, c_bc) = seg_off[:-1]
    assert (c_mlqk, c_mlv, c_mlo) == (0, ml_w, 2 * ml_w)

    x2 = x.reshape(t, d)
    for i in range(depth):
        w = w_in[i]
        w_main = jnp.concatenate([w[:, o:o + wd] for o, wd in main_segments], axis=1).astype(BF16)
        small = jnp.zeros((d, LANES), w.dtype)
        small = small.at[:, COL_ML_I:COL_ML_I + ML_HEADS].set(w[:, o_mli:o_mli + ML_HEADS])
        small = small.at[:, COL_ML_F:COL_ML_F + ML_HEADS].set(w[:, o_mlf:o_mlf + ML_HEADS])
        small = small.at[:, COL_DT:COL_DT + SSM_HEADS].set(w[:, o_sdt:o_sdt + SSM_HEADS])
        small = small.at[:, COL_FOX_F:COL_FOX_F + FOX_HEADS].set(w[:, o_ff:o_ff + FOX_HEADS])
        w_small = small.astype(BF16)
        gate_bias = _gate_bias_row(ml_b_i[i], ml_b_f[i], ssm_dt_bias[i], fox_b_f[i])

        zm, zs = _inproj(x2, _row(norm_mix_g[i]), w_main, w_small)

        y_ml = _mlstm(zm, zs, _pad_rows(ml_conv_w[i].astype(F32), 8), _row(ml_conv_b[i]), gate_bias,
                      _row(ml_norm_g[i]), bsz=bsz, seq=seq)

        scw = ssm_conv_w[i].astype(F32)
        scb = ssm_conv_b[i].astype(F32)
        alog_row = jnp.zeros((LANES,), F32).at[COL_DT:COL_DT + SSM_HEADS].set(
            ssm_a_log[i].astype(F32)).reshape(1, LANES)
        d_row = jnp.repeat(ssm_d[i].astype(F32), SSM_HEADDIM).reshape(1, ssm_w)
        y_ssm = _ssd(zm, zs, _pad_rows(scw[:, :ssm_w], 8), _row(scb[:ssm_w]),
                     _pad_rows(scw[:, ssm_w:], 8), _row(scb[ssm_w:]), gate_bias, alog_row, d_row,
                     _row(ssm_norm_g[i]), bsz=bsz, seq=seq, col_z=c_sz, col_x=c_sx, col_bc=c_bc)

        qn, kn, ft = _fox_prep(zm, zs, gate_bias, _row(fox_q_norm_g[i]), _row(fox_k_norm_g[i]),
                               bsz=bsz, seq=seq, col_q=c_fq, col_k=c_fk)
        y_fox = _fox_attn(qn, kn, zm, ft, bsz=bsz, seq=seq, col_v=c_fv)

        x2 = _merge(x2, y_ml, y_ssm, y_fox, zm, w_branch_ml[i].astype(BF16), w_branch_ssm[i].astype(BF16),
                    w_branch_fox[i].astype(BF16), w_out[i].astype(BF16), col_g=c_gml)

        x2 = _peer(x2, _row(norm_ffn_g[i]), peer_w_q[i].astype(BF16), peer_keys1[i].astype(BF16),
                   peer_keys2[i].astype(BF16), peer_u[i].astype(BF16), peer_v[i])

        x2 = _ple(x2, p[i].reshape(t, -1), _row(norm_ple_g[i]), ple_w_gate[i].astype(BF16),
                  ple_w_proj[i].astype(BF16), _row(final_norm_g), final=(i == depth - 1))
    return x2.reshape(bsz, seq, d)
```

```python
import functools
import math

import jax
import jax.numpy as jnp
from jax import lax
from jax.experimental import pallas as pl
from jax.experimental.pallas import tpu as pltpu

F32 = jnp.float32
BF16 = jnp.bfloat16
EPS = 1e-6
LOG2E = math.log2(math.e)

CONV_WIDTH = 4
ML_HEADS = 8
ML_DQK = 64
ML_DV = 128
SSM_HEADS = 16
SSM_HEADDIM = 64
SSM_GROUPS = 2
SSM_STATE = 128
FOX_HEADS = 8
FOX_HEADDIM = 128
PEER_HEADS = 8
PEER_NKEYS = 128
PEER_QDIM = 256
PEER_TOPK = 16

LANES = 128
SUB = 8
HALO = 8

COL_ML_I = 0
COL_ML_F = 8
COL_DT = 16
COL_FOX_F = 32

VMEM_LIMIT = 56 * 1024 * 1024


def _cparams(sem, flags=None):
    return pltpu.CompilerParams(dimension_semantics=sem, vmem_limit_bytes=VMEM_LIMIT, flags=flags)


def _log_sigmoid(v):
    return jnp.minimum(v, 0.0) - jnp.log1p(jnp.exp(-jnp.abs(v)))


def _sigmoid(v):
    return 1.0 / (1.0 + jnp.exp(-v))


def _silu(v):
    return v * _sigmoid(v)


def _softplus(v):
    return jnp.maximum(v, 0.0) + jnp.log1p(jnp.exp(-jnp.abs(v)))


def _tril(n):
    r = lax.broadcasted_iota(jnp.int32, (n, n), 0)
    c = lax.broadcasted_iota(jnp.int32, (n, n), 1)
    return r >= c


def _cumsum_rows(v):
    n = v.shape[0]
    return jnp.dot(_tril(n).astype(F32), v, precision=lax.Precision.HIGHEST,
                   preferred_element_type=F32)


def _inproj_kernel(x_ref, g_ref, wm_ref, ws_ref, zm_ref, zs_ref, h_scr):
    @pl.when(pl.program_id(1) == 0)
    def _():
        x = x_ref[...]
        ms = jnp.mean(x * x, axis=-1, keepdims=True)
        h = (x * lax.rsqrt(ms + EPS) * g_ref[...]).astype(BF16)
        h_scr[...] = h
        zs_ref[...] = jnp.dot(h, ws_ref[...], preferred_element_type=F32)

    zm_ref[...] = jnp.dot(h_scr[...], wm_ref[...], preferred_element_type=F32).astype(zm_ref.dtype)


def _inproj(x2, g, w_main, w_small, tm=2048, tn=512):
    t, d = x2.shape
    tm = min(tm, t)
    n = w_main.shape[1]
    return pl.pallas_call(
        _inproj_kernel,
        grid=(t // tm, n // tn),
        in_specs=[
            pl.BlockSpec((tm, d), lambda i, j: (i, 0)),
            pl.BlockSpec((1, d), lambda i, j: (0, 0)),
            pl.BlockSpec((d, tn), lambda i, j: (0, j)),
            pl.BlockSpec((d, LANES), lambda i, j: (0, 0)),
        ],
        out_specs=[
            pl.BlockSpec((tm, tn), lambda i, j: (i, j)),
            pl.BlockSpec((tm, LANES), lambda i, j: (i, 0)),
        ],
        out_shape=[jax.ShapeDtypeStruct((t, n), BF16), jax.ShapeDtypeStruct((t, LANES), F32)],
        scratch_shapes=[pltpu.VMEM((tm, d), BF16)],
        compiler_params=_cparams(("parallel", "arbitrary")),
        name="inproj",
    )(x2, g, w_main, w_small)


def _conv_chunk(buf_ref, w_ref, b_ref, start, length):
    acc = None
    for j in range(CONV_WIDTH):
        off = HALO - (CONV_WIDTH - 1) + j + start
        term = buf_ref[off:off + length, :] * w_ref[j:j + 1, :]
        acc = term if acc is None else acc + term
    return acc + b_ref[...]


def _mlstm_kernel(qk_ref, v_ref, o_ref, gs_ref, cw_ref, cb_ref, gb_ref, ng_ref, y_ref,
                  buf, c_scr, m_scr, *, ts, chunk):
    nh, dk, dv = ML_HEADS, ML_DQK, ML_DV
    qkw = nh * dk
    bsz = qk_ref.shape[0]

    @pl.when(pl.program_id(0) == 0)
    def _():
        buf[:, 0:HALO, :] = jnp.zeros((bsz, HALO, buf.shape[2]), F32)
        c_scr[...] = jnp.zeros(c_scr.shape, F32)
        m_scr[...] = jnp.zeros(m_scr.shape, F32)

    buf[:, HALO:HALO + ts, :] = qk_ref[...].astype(F32)
    causal = _tril(chunk)
    lane = lax.broadcasted_iota(jnp.int32, (chunk, LANES), 1)
    ones_col = (lane == 0).astype(BF16)

    for c in range(ts // chunk):
        r0 = c * chunk
        for b in range(bsz):
            act = _silu(_conv_chunk(buf.at[b], cw_ref, cb_ref, r0, chunk))
            q_all = (act[:, :qkw] * (dk ** -0.5)).astype(BF16)
            kt_all = act[:, qkw:].T
            gates = gs_ref[b, r0:r0 + chunk, :] + gb_ref[...]
            bcum = _cumsum_rows(_log_sigmoid(gates))
            gates_t = gates.T
            bcum_t = bcum.T
            for h in range(nh):
                i_col = gates[:, COL_ML_I + h:COL_ML_I + h + 1]
                i_row = gates_t[COL_ML_I + h:COL_ML_I + h + 1, :]
                b_col = bcum[:, COL_ML_F + h:COL_ML_F + h + 1]
                b_row = bcum_t[COL_ML_F + h:COL_ML_F + h + 1, :]
                g_tot = b_col[chunk - 1:chunk, :]
                m_prev = m_scr[b, h:h + 1, 0:1]
                dmat = jnp.where(causal, b_col - b_row + i_row, -jnp.inf)
                inter = b_col + m_prev
                m_t = jnp.maximum(inter, jnp.max(dmat, axis=-1, keepdims=True))
                qh = q_all[:, h * dk:(h + 1) * dk]
                kt = kt_all[h * dk:(h + 1) * dk, :]
                s = jnp.dot(qh, kt.astype(BF16), preferred_element_type=F32)
                sc = (s * jnp.exp(dmat - m_t)).astype(BF16)
                w_inter = jnp.exp(inter - m_t)
                vaug = jnp.concatenate([v_ref[b, r0:r0 + chunk, h * dv:(h + 1) * dv], ones_col], axis=1)
                cmem = c_scr[b, h]
                tot = (jnp.dot(sc, vaug, preferred_element_type=F32)
                       + w_inter * jnp.dot(qh, cmem.astype(BF16), preferred_element_type=F32))
                num = tot[:, :dv]
                den = tot[:, dv:dv + 1]
                hv = num / jnp.maximum(jnp.abs(den), jnp.exp(-m_t))
                hn = hv * lax.rsqrt(jnp.mean(hv * hv, axis=-1, keepdims=True) + EPS)
                hn = hn * ng_ref[:, h * dv:(h + 1) * dv]
                og = _sigmoid(o_ref[b, r0:r0 + chunk, h * dv:(h + 1) * dv].astype(F32))
                y_ref[b, r0:r0 + chunk, h * dv:(h + 1) * dv] = (og * hn).astype(y_ref.dtype)
                a_row = g_tot - b_row + i_row
                m_new = jnp.maximum(g_tot + m_prev, jnp.max(a_row, axis=-1, keepdims=True))
                wa_row = jnp.exp(a_row - m_new)
                decay = jnp.exp(g_tot + m_prev - m_new)
                kw = (kt * wa_row).astype(BF16)
                c_scr[b, h] = decay * cmem + jnp.dot(kw, vaug, preferred_element_type=F32)
                m_scr[b, h:h + 1, :] = jnp.broadcast_to(m_new, (1, LANES))

    buf[:, 0:HALO, :] = buf[:, ts:ts + HALO, :]


def _mlstm(zm, zs, conv_w, conv_b, gate_bias, norm_g, *, bsz, seq, ts=256, chunk=256):
    t, n = zm.shape
    w = ML_HEADS * ML_DV
    zm3 = zm.reshape(bsz, seq, n)
    zs3 = zs.reshape(bsz, seq, LANES)
    full = lambda a: pl.BlockSpec(a.shape, lambda s: (0, 0))
    y = pl.pallas_call(
        functools.partial(_mlstm_kernel, ts=ts, chunk=chunk),
        grid=(seq // ts,),
        in_specs=[
            pl.BlockSpec((bsz, ts, w), lambda s: (0, s, 0)),
            pl.BlockSpec((bsz, ts, w), lambda s: (0, s, 1)),
            pl.BlockSpec((bsz, ts, w), lambda s: (0, s, 2)),
            pl.BlockSpec((bsz, ts, LANES), lambda s: (0, s, 0)),
            full(conv_w), full(conv_b), full(gate_bias), full(norm_g),
        ],
        out_specs=pl.BlockSpec((bsz, ts, w), lambda s: (0, s, 0)),
        out_shape=jax.ShapeDtypeStruct((bsz, seq, w), BF16),
        scratch_shapes=[
            pltpu.VMEM((bsz, HALO + ts, w), F32),
            pltpu.VMEM((bsz, ML_HEADS, ML_DQK, 2 * ML_DV), F32),
            pltpu.VMEM((bsz, ML_HEADS, LANES), F32),
        ],
        compiler_params=_cparams(("arbitrary",)),
        name="mlstm",
    )(zm3, zm3, zm3, zs3, conv_w, conv_b, gate_bias, norm_g)
    return y.reshape(t, w)


def _ssd_kernel(z_ref, x_ref, bc_ref, gs_ref, cwx_ref, cbx_ref, cwbc_ref, cbbc_ref, gb_ref,
                alog_ref, d_ref, ng_ref, y_ref, bufx, bufbc, st_scr, *, ts, chunk):
    ng, nst, p = SSM_GROUPS, SSM_STATE, SSM_HEADDIM
    hg = SSM_HEADS // ng
    gw = hg * p
    pairs_per_group = gw // LANES

    @pl.when(pl.program_id(1) == 0)
    def _():
        bufx[0:HALO, :] = jnp.zeros((HALO, bufx.shape[1]), F32)
        bufbc[0:HALO, :] = jnp.zeros((HALO, bufbc.shape[1]), F32)
        st_scr[...] = jnp.zeros(st_scr.shape, F32)

    bufx[HALO:HALO + ts, :] = x_ref[...].astype(F32)
    bufbc[HALO:HALO + ts, :] = bc_ref[...].astype(F32)
    causal = _tril(chunk)
    low_half = lax.broadcasted_iota(jnp.int32, (chunk, LANES), 1) < p
    a_row_all = -jnp.exp(alog_ref[...])

    for c in range(ts // chunk):
        r0 = c * chunk
        xa = _silu(_conv_chunk(bufx, cwx_ref, cbx_ref, r0, chunk))
        bca = _silu(_conv_chunk(bufbc, cwbc_ref, cbbc_ref, r0, chunk))
        dt = _softplus(gs_ref[r0:r0 + chunk, :] + gb_ref[...])
        acum = _cumsum_rows(dt * a_row_all)
        acum_t = acum.T
        y_parts = []
        for g in range(ng):
            bg = bca[:, g * nst:(g + 1) * nst]
            cg = bca[:, ng * nst + g * nst:ng * nst + (g + 1) * nst].astype(BF16)
            bg_t = bg.T
            cbt = jnp.dot(cg, bg_t.astype(BF16), preferred_element_type=F32)
            state = st_scr[g]
            inter = jnp.dot(cg, state.astype(BF16), preferred_element_type=F32)
            xs_scaled, last_parts = [], []
            for pp in range(pairs_per_group):
                pidx = g * pairs_per_group + pp
                ha = 2 * pidx
                ca, cb = COL_DT + ha, COL_DT + ha + 1
                sl = slice(pidx * LANES, (pidx + 1) * LANES)
                xa_p = xa[:, sl]
                dt_pair = jnp.where(low_half, dt[:, ca:ca + 1], dt[:, cb:cb + 1])
                ac_pair = jnp.where(low_half, acum[:, ca:ca + 1], acum[:, cb:cb + 1])
                xs_p = xa_p * dt_pair
                xs_b = xs_p.astype(BF16)
                lm_a = jnp.exp(jnp.where(causal, acum[:, ca:ca + 1] - acum_t[ca:ca + 1, :], -jnp.inf))
                lm_b = jnp.exp(jnp.where(causal, acum[:, cb:cb + 1] - acum_t[cb:cb + 1, :], -jnp.inf))
                ya = jnp.dot((cbt * lm_a).astype(BF16), xs_b, preferred_element_type=F32)
                yb = jnp.dot((cbt * lm_b).astype(BF16), xs_b, preferred_element_type=F32)
                y_p = (jnp.where(low_half, ya, yb)
                       + inter[:, pp * LANES:(pp + 1) * LANES] * jnp.exp(ac_pair)
                       + d_ref[:, sl] * xa_p)
                y_parts.append(y_p)
                last = ac_pair[chunk - 1:chunk, :]
                xs_scaled.append((xs_p * jnp.exp(last - ac_pair)).astype(BF16))
                last_parts.append(last)
            xs_g = jnp.concatenate(xs_scaled, axis=1)
            last_g = jnp.concatenate(last_parts, axis=1)
            st_scr[g] = jnp.exp(last_g) * state + jnp.dot(bg_t.astype(BF16), xs_g,
                                                          preferred_element_type=F32)
        ys = jnp.concatenate(y_parts, axis=1)
        ys = ys * _silu(z_ref[r0:r0 + chunk, :].astype(F32))
        outs = []
        for g in range(ng):
            yg = ys[:, g * gw:(g + 1) * gw]
            yn = yg * lax.rsqrt(jnp.mean(yg * yg, axis=-1, keepdims=True) + EPS)
            outs.append(yn * ng_ref[:, g * gw:(g + 1) * gw])
        y_ref[r0:r0 + chunk, :] = jnp.concatenate(outs, axis=1).astype(y_ref.dtype)

    bufx[0:HALO, :] = bufx[ts:ts + HALO, :]
    bufbc[0:HALO, :] = bufbc[ts:ts + HALO, :]


def _ssd(zm, zs, cwx, cbx, cwbc, cbbc, gate_bias, alog_row, d_row, norm_g, *, bsz, seq,
         col_z, col_x, col_bc, ts=256, chunk=128):
    t = zm.shape[0]
    w = SSM_HEADS * SSM_HEADDIM
    bcw = 2 * SSM_GROUPS * SSM_STATE
    nblk = seq // ts
    row = lambda b, s: b * nblk + s
    full = lambda a: pl.BlockSpec(a.shape, lambda b, s: (0, 0))
    return pl.pallas_call(
        functools.partial(_ssd_kernel, ts=ts, chunk=chunk),
        grid=(bsz, nblk),
        in_specs=[
            pl.BlockSpec((ts, w), lambda b, s: (row(b, s), col_z // w)),
            pl.BlockSpec((ts, w), lambda b, s: (row(b, s), col_x // w)),
            pl.BlockSpec((ts, bcw), lambda b, s: (row(b, s), col_bc // bcw)),
            pl.BlockSpec((ts, LANES), lambda b, s: (row(b, s), 0)),
            full(cwx), full(cbx), full(cwbc), full(cbbc), full(gate_bias),
            full(alog_row), full(d_row), full(norm_g),
        ],
        out_specs=pl.BlockSpec((ts, w), lambda b, s: (row(b, s), 0)),
        out_shape=jax.ShapeDtypeStruct((t, w), BF16),
        scratch_shapes=[
            pltpu.VMEM((HALO + ts, w), F32),
            pltpu.VMEM((HALO + ts, bcw), F32),
            pltpu.VMEM((SSM_GROUPS, SSM_STATE, w // SSM_GROUPS), F32),
        ],
        compiler_params=_cparams(("parallel", "arbitrary")),
        name="ssd",
    )(zm, zm, zm, zs, cwx, cbx, cwbc, cbbc, gate_bias, alog_row, d_row, norm_g)


def _fox_prep_kernel(q_ref, k_ref, gs_ref, gb_ref, gq_ref, gk_ref, qn_ref, kn_ref, ft_ref,
                     carry, *, ts):
    d = FOX_HEADDIM

    @pl.when(pl.program_id(1) == 0)
    def _():
        carry[...] = jnp.zeros(carry.shape, F32)

    for h in range(FOX_HEADS):
        sl = slice(h * d, (h + 1) * d)
        qh = q_ref[:, sl].astype(F32)
        qn = qh * lax.rsqrt(jnp.mean(qh * qh, axis=-1, keepdims=True) + EPS) * gq_ref[...]
        qn_ref[:, sl] = qn.astype(qn_ref.dtype)
        kh = k_ref[:, sl].astype(F32)
        kn = kh * lax.rsqrt(jnp.mean(kh * kh, axis=-1, keepdims=True) + EPS) * gk_ref[...]
        kn_ref[:, sl] = (kn * (d ** -0.5 * LOG2E)).astype(kn_ref.dtype)

    lf = _log_sigmoid(gs_ref[...] + gb_ref[...])
    fc = _cumsum_rows(lf) + carry[0:1, :]
    carry[0:1, :] = fc[ts - 1:ts, :]
    ft_ref[...] = fc.T[COL_FOX_F:COL_FOX_F + FOX_HEADS, :] * LOG2E


def _fox_prep(zm, zs, gate_bias, gq, gk, *, bsz, seq, col_q, col_k, ts=256):
    t = zm.shape[0]
    w = FOX_HEADS * FOX_HEADDIM
    nblk = seq // ts
    row = lambda b, s: b * nblk + s
    full = lambda a: pl.BlockSpec(a.shape, lambda b, s: (0, 0))
    return pl.pallas_call(
        functools.partial(_fox_prep_kernel, ts=ts),
        grid=(bsz, nblk),
        in_specs=[
            pl.BlockSpec((ts, w), lambda b, s: (row(b, s), col_q // w)),
            pl.BlockSpec((ts, w), lambda b, s: (row(b, s), col_k // w)),
            pl.BlockSpec((ts, LANES), lambda b, s: (row(b, s), 0)),
            full(gate_bias), full(gq), full(gk),
        ],
        out_specs=[
            pl.BlockSpec((ts, w), lambda b, s: (row(b, s), 0)),
            pl.BlockSpec((ts, w), lambda b, s: (row(b, s), 0)),
            pl.BlockSpec((None, FOX_HEADS, ts), lambda b, s: (b, 0, s)),
        ],
        out_shape=[jax.ShapeDtypeStruct((t, w), BF16), jax.ShapeDtypeStruct((t, w), BF16),
                   jax.ShapeDtypeStruct((bsz, FOX_HEADS, seq), F32)],
        scratch_shapes=[pltpu.VMEM((8, LANES), F32)],
        compiler_params=_cparams(("parallel", "arbitrary")),
        name="fox_prep",
    )(zm, zm, zs, gate_bias, gq, gk)


def _fox_attn_kernel(qi_ref, kj_ref, q_ref, k_ref, v_ref, f_ref, o_ref, m_scr, acc_scr, vaug_scr,
                     *, tq, tk, rb, rb_diag):
    pr = pl.program_id(2)
    qi = qi_ref[pr]
    kj = kj_ref[pr]
    h = pl.program_id(1)
    d = FOX_HEADDIM
    ratio = tq // tk
    per_tile = tk // rb

    @pl.when(kj == 0)
    def _():
        m_scr[...] = jnp.full(m_scr.shape, -jnp.inf, F32)
        acc_scr[...] = jnp.zeros(acc_scr.shape, F32)

    vaug_scr[:, :d] = v_ref[...]
    vaug_scr[:, d:] = (lax.broadcasted_iota(jnp.int32, (tk, d), 1) == 0).astype(BF16)
    fk_all = f_ref[pl.ds(h, 1), :]

    def row_block(row0, nrows, diag_row0):
        rows = slice(row0, row0 + nrows)
        ncol = tk if diag_row0 is None else diag_row0 + nrows
        s = lax.dot_general(q_ref[rows, :], k_ref[:ncol, :], (((1,), (1,)), ((), ())),
                            preferred_element_type=F32) - fk_all[:, :ncol]
        if diag_row0 is not None:
            tail = jnp.where(_tril(nrows), s[:, ncol - nrows:], -jnp.inf)
            s = tail if ncol == nrows else jnp.concatenate([s[:, :ncol - nrows], tail], axis=1)
        m_old = m_scr[rows, :]
        m_new = jnp.maximum(m_old, jnp.max(s, axis=-1, keepdims=True))
        alpha = jnp.exp2(m_old - m_new)
        p = jnp.exp2(s - m_new[:, :1]).astype(BF16)
        pv = jnp.dot(p, vaug_scr[:ncol, :], preferred_element_type=F32)
        acc_scr[rows, :] = jnp.concatenate([alpha, alpha], axis=1) * acc_scr[rows, :] + pv
        m_scr[rows, :] = m_new

    @pl.when(kj < qi * ratio)
    def _():
        for r in range(tq // rb):
            row_block(r * rb, rb, None)

    for c in range(ratio):
        @pl.when(kj == qi * ratio + c)
        def _(c=c):
            for r in range(tk // rb_diag):
                row_block(c * tk + r * rb_diag, rb_diag, r * rb_diag)
            for r in range((c + 1) * per_tile, tq // rb):
                row_block(r * rb, rb, None)

    @pl.when(kj == qi * ratio + ratio - 1)
    def _():
        acc = acc_scr[...]
        o_ref[...] = (acc[:, :d] / acc[:, d:d + 1]).astype(o_ref.dtype)


def _fox_attn(qn, kn, zm, ft, *, bsz, seq, col_v, tk=2048, ratio=1, rb=128, rb_diag=512):
    t = qn.shape[0]
    d = FOX_HEADDIM
    tk = min(tk, seq // ratio)
    tq = ratio * tk
    rb = min(rb, tk)
    rb_diag = min(rb_diag, tk)
    nq = seq // tq
    nkt = seq // tk
    qi_tab, kj_tab = [], []
    for i in range(nq):
        for j in range(ratio * (i + 1)):
            qi_tab.append(i)
            kj_tab.append(j)
    qi_tab = jnp.asarray(qi_tab, jnp.int32)
    kj_tab = jnp.asarray(kj_tab, jnp.int32)
    vblk = col_v // d
    grid_spec = pltpu.PrefetchScalarGridSpec(
        num_scalar_prefetch=2,
        grid=(bsz, FOX_HEADS, qi_tab.shape[0]),
        in_specs=[
            pl.BlockSpec((tq, d), lambda b, h, p, qi, kj: (b * nq + qi[p], h)),
            pl.BlockSpec((tk, d), lambda b, h, p, qi, kj: (b * nkt + kj[p], h)),
            pl.BlockSpec((tk, d), lambda b, h, p, qi, kj: (b * nkt + kj[p], vblk + h)),
            pl.BlockSpec((None, FOX_HEADS, tk), lambda b, h, p, qi, kj: (b, 0, kj[p])),
        ],
        out_specs=pl.BlockSpec((tq, d), lambda b, h, p, qi, kj: (b * nq + qi[p], h)),
        scratch_shapes=[pltpu.VMEM((tq, LANES), F32), pltpu.VMEM((tq, 2 * d), F32),
                        pltpu.VMEM((tk, 2 * d), BF16)],
    )
    return pl.pallas_call(
        functools.partial(_fox_attn_kernel, tq=tq, tk=tk, rb=rb, rb_diag=rb_diag),
        grid_spec=grid_spec,
        out_shape=jax.ShapeDtypeStruct((t, FOX_HEADS * d), BF16),
        compiler_params=_cparams(("parallel", "parallel", "arbitrary")),
        name="fox_attn",
    )(qi_tab, kj_tab, qn, kn, zm, ft)


def _merge_kernel(x_ref, yml_ref, yssm_ref, yfox_ref, gml_ref, gssm_ref, gfox_ref,
                  pml_ref, pssm_ref, pfox_ref, wout_ref, o_ref):
    def branch(y_ref, g_ref, p_ref):
        proj = jnp.dot(y_ref[...], p_ref[...], preferred_element_type=F32)
        return _sigmoid(g_ref[...].astype(F32)) * proj

    merged = (branch(yml_ref, gml_ref, pml_ref) + branch(yssm_ref, gssm_ref, pssm_ref)
              + branch(yfox_ref, gfox_ref, pfox_ref))
    o_ref[...] = x_ref[...] + jnp.dot(merged.astype(BF16), wout_ref[...], preferred_element_type=F32)


def _merge(x2, y_ml, y_ssm, y_fox, zm, p_ml, p_ssm, p_fox, w_out, *, col_g, tm=512):
    t, d = x2.shape
    rowblk = lambda c: pl.BlockSpec((tm, d), lambda i: (i, c))
    wfull = lambda a: pl.BlockSpec(a.shape, lambda i: (0, 0))
    gblk = col_g // d
    return pl.pallas_call(
        _merge_kernel,
        grid=(t // tm,),
        in_specs=[rowblk(0), rowblk(0), rowblk(0), rowblk(0),
                  rowblk(gblk), rowblk(gblk + 1), rowblk(gblk + 2),
                  wfull(p_ml), wfull(p_ssm), wfull(p_fox), wfull(w_out)],
        out_specs=rowblk(0),
        out_shape=jax.ShapeDtypeStruct((t, d), F32),
        compiler_params=_cparams(("parallel",)),
        name="merge",
    )(x2, y_ml, y_ssm, y_fox, zm, zm, zm, p_ml, p_ssm, p_fox, w_out)


def _erf(v):
    return lax.erf(v)


def _gelu_x2(v):
    return v * (1.0 + _erf(v * (2.0 ** -0.5)))


def _sorted_top(tiles):
    n = len(tiles)
    a = list(tiles)

    def keep_larger_first(i, j):
        a[i], a[j] = jnp.maximum(a[i], a[j]), jnp.minimum(a[i], a[j])

    def merge(lo_stride):
        j = lo_stride
        while j >= 1:
            for i in range(n):
                if i ^ j > i:
                    keep_larger_first(i, i ^ j)
            j //= 2

    k = 2
    while k <= n:
        j = k // 2
        while j >= 1:
            for i in range(n):
                l = i ^ j
                if l > i:
                    if i & k == 0:
                        keep_larger_first(i, l)
                    else:
                        keep_larger_first(l, i)
            j //= 2
        k *= 2
    shift = SUB // 2
    while shift >= 1:
        b = [pltpu.roll(x, shift, 0) for x in a]
        a = [jnp.maximum(a[j], b[n - 1 - j]) for j in range(n)]
        merge(n // 2)
        shift //= 2
    return a


def _as_rows(vals):
    sub = lax.broadcasted_iota(jnp.int32, (SUB, LANES), 0)
    tiles = []
    for g in range(len(vals) // SUB):
        t = vals[SUB * g]
        for r in range(1, SUB):
            t = jnp.where(sub == r, vals[SUB * g + r], t)
        tiles.append(t)
    return tiles


def _candidate_tiles(v1, v2, v1_rows, v2_rows, k):
    sub = lax.broadcasted_iota(jnp.int32, (SUB, LANES), 0)
    tiles = [v1[0] + v2_rows[g] for g in range(k // SUB)]
    a = 1
    while k // (a + 1) > 1:
        nb = k // (a + 1)
        assert nb <= SUB
        t = v1[a] + v2_rows[0]
        tiles.append(t if nb == SUB else jnp.where(sub < nb, t, -jnp.inf))
        a += 1
    assert a % SUB == 0
    tiles.extend(v1_rows[g] + v2[0] for g in range(a // SUB, k // SUB))
    assert len(tiles) <= 16
    tiles.extend([jnp.full((SUB, LANES), -jnp.inf, F32)] * (16 - len(tiles)))
    return tiles


def _peer_kernel(x_ref, g_ref, wq_ref, k1_ref, k2_ref, u_ref, vt_ref, o_ref,
                 ht_scr, q_scr, theta_scr, e1_scr, s2_scr, e2_scr, acc_scr, g0_scr, g1_scr,
                 w0_scr, w1_scr, *, tb, eb, n_e):
    j = pl.program_id(1)
    nk, topk, nhead = PEER_NKEYS, PEER_TOPK, PEER_HEADS
    assert nk // SUB == topk
    half = PEER_QDIM // 2
    n_lc = tb // LANES
    pk = SUB

    @pl.when(j == 0)
    def _():
        x = x_ref[...]
        hn32 = x * lax.rsqrt(jnp.mean(x * x, axis=-1, keepdims=True) + EPS) * g_ref[...]
        ht_scr[...] = hn32.T.astype(BF16)
        q = jnp.dot(hn32.astype(BF16), wq_ref[...], preferred_element_type=F32).astype(BF16)
        for c in range(q.shape[1] // half):
            q_scr[c] = q[:, c * half:(c + 1) * half]
        acc_scr[...] = jnp.zeros(acc_scr.shape, F32)

        def head_body(h, carry):
            nt = (((1,), (1,)), ((), ()))
            s1 = lax.dot_general(k1_ref[h], q_scr[2 * h], nt, preferred_element_type=F32)
            s2 = lax.dot_general(k2_ref[h], q_scr[2 * h + 1], nt, preferred_element_type=F32)
            for lc in range(n_lc):
                ls = slice(lc * LANES, (lc + 1) * LANES)
                s1c = s1[:, ls].reshape(nk // SUB, SUB, LANES)
                s2c = s2[:, ls].reshape(nk // SUB, SUB, LANES)
                v1 = _sorted_top([s1c[i] for i in range(nk // SUB)])
                v2 = _sorted_top([s2c[i] for i in range(nk // SUB)])
                v1_rows, v2_rows = _as_rows(v1), _as_rows(v2)
                tv = _sorted_top(_candidate_tiles(v1, v2, v1_rows, v2_rows, topk))
                tau, mx = tv[topk - 1], tv[0]
                zsum = jnp.zeros_like(mx)
                for a in range(topk):
                    zsum = zsum + jnp.exp(tv[a] - mx)
                th_rows = [jnp.full((SUB, LANES), jnp.inf, F32) for _ in v1_rows]
                for b in range(topk):
                    th_rows = [jnp.where(v1_rows[g] + v2[b] >= tau, v2[b], th_rows[g])
                               for g in range(len(v1_rows))]
                theta = jnp.full(s1c.shape, jnp.inf, F32)
                for a in range(topk):
                    th_a = jnp.broadcast_to(th_rows[a // SUB][a % SUB:a % SUB + 1, :], (SUB, LANES))
                    theta = jnp.where(s1c == v1[a][None], th_a[None], theta)
                theta_scr[h, lc] = theta.reshape(nk, LANES)
                e1_scr[h, lc] = (jnp.exp(s1c - v1[0][None]) * (0.5 / zsum)[None]).reshape(nk, LANES)
                s2_scr[h, lc] = s2c
                e2_scr[h, lc] = jnp.exp(s2c - v2[0][None])
            return carry

        lax.fori_loop(0, nhead, head_body, 0)

    n_r = eb // nk

    def stage_a(w_wr, lc, r_range):
        for r in r_range:
            i1 = j * n_r + r
            wsum = None
            for h in range(nhead):
                th = jnp.broadcast_to(theta_scr[h, lc, pl.ds(i1, 1), :], (pk, LANES))
                e1 = jnp.broadcast_to(e1_scr[h, lc, pl.ds(i1, 1), :], (pk, LANES))
                term = jnp.where(s2_scr[h, lc] >= th[None], e2_scr[h, lc] * e1[None], 0.0)
                wsum = term if wsum is None else wsum + term
            w_wr[lc, r * (nk // pk):(r + 1) * (nk // pk)] = wsum

    def stage_b(w_rd, g_wr, n, m):
        ts_ = slice(n * 2 * LANES, (n + 1) * 2 * LANES)
        es = slice(m * (eb // 2), (m + 1) * (eb // 2))
        ps = slice(m * (eb // 2 // pk), (m + 1) * (eb // 2 // pk))
        act = _gelu_x2(jnp.dot(u_ref[es, :], ht_scr[:, ts_], preferred_element_type=F32))
        w_prev = jnp.concatenate([w_rd[2 * n + c, ps].reshape(eb // 2, LANES) for c in range(2)], axis=1)
        g_wr[es, ts_] = (act * w_prev).astype(BF16)

    def stage_c(g_rd, n, m):
        ts_ = slice(n * 2 * LANES, (n + 1) * 2 * LANES)
        ds_ = slice(m * (acc_scr.shape[0] // 2), (m + 1) * (acc_scr.shape[0] // 2))
        acc_scr[ds_, ts_] += jnp.dot(vt_ref[ds_, :], g_rd[:, ts_], preferred_element_type=F32)

    def step(parity, do_a=True, do_b=True, do_c=True):
        w_wr, w_rd, g_wr, g_rd = ((w0_scr, w1_scr, g1_scr, g0_scr) if parity == 0
                                  else (w1_scr, w0_scr, g0_scr, g1_scr))
        for n in range(n_lc // 2):
            for m in range(2):
                if do_b:
                    stage_b(w_rd, g_wr, n, m)
                if do_a:
                    stage_a(w_wr, 2 * n + m, range(0, n_r // 2))
                if do_c:
                    stage_c(g_rd, n, m)
                if do_a:
                    stage_a(w_wr, 2 * n + m, range(n_r // 2, n_r))

    @pl.when(j == 0)
    def _():
        step(0, do_b=False, do_c=False)

    @pl.when(j == 1)
    def _():
        step(1, do_c=False)

    steady = jnp.logical_and(j >= 2, j < n_e)

    @pl.when(jnp.logical_and(steady, j % 2 == 0))
    def _():
        step(0)

    @pl.when(jnp.logical_and(steady, j % 2 == 1))
    def _():
        step(1)

    @pl.when(j == n_e)
    def _():
        step(n_e % 2, do_a=False)

    @pl.when(j == n_e + 1)
    def _():
        step((n_e + 1) % 2, do_a=False, do_b=False)

    @pl.when(j == n_e + 1)
    def _():
        o_ref[...] = x_ref[...] + acc_scr[...].T


def _peer(x2, g, w_q, keys1, keys2, u, v, *, tb=512, eb=512):
    t, d = x2.shape
    tb = min(tb, t)
    ne = u.shape[0]
    nq = w_q.shape[1]
    n_e = ne // eb
    n_lc = tb // LANES
    pk = SUB
    v_t = v.reshape(n_e, eb, d).transpose(0, 2, 1).astype(BF16)
    once = pl.Buffered(1)
    return pl.pallas_call(
        functools.partial(_peer_kernel, tb=tb, eb=eb, n_e=n_e),
        grid=(t // tb, n_e + 2),
        in_specs=[
            pl.BlockSpec((tb, d), lambda i, j: (i, 0), pipeline_mode=once),
            pl.BlockSpec((1, d), lambda i, j: (0, 0)),
            pl.BlockSpec((d, nq), lambda i, j: (0, 0), pipeline_mode=once),
            pl.BlockSpec(keys1.shape, lambda i, j: (0, 0, 0)),
            pl.BlockSpec(keys2.shape, lambda i, j: (0, 0, 0)),
            pl.BlockSpec((eb, d), lambda i, j: (jnp.clip(j - 1, 0, n_e - 1), 0)),
            pl.BlockSpec((None, d, eb), lambda i, j: (jnp.clip(j - 2, 0, n_e - 1), 0, 0)),
        ],
        out_specs=pl.BlockSpec((tb, d), lambda i, j: (i, 0)),
        out_shape=jax.ShapeDtypeStruct((t, d), F32),
        scratch_shapes=[
            pltpu.VMEM((d, tb), BF16),
            pltpu.VMEM((2 * PEER_HEADS, tb, PEER_QDIM // 2), BF16),
            pltpu.VMEM((PEER_HEADS, n_lc, PEER_NKEYS, LANES), F32),
            pltpu.VMEM((PEER_HEADS, n_lc, PEER_NKEYS, LANES), F32),
            pltpu.VMEM((PEER_HEADS, n_lc, PEER_NKEYS // pk, pk, LANES), F32),
            pltpu.VMEM((PEER_HEADS, n_lc, PEER_NKEYS // pk, pk, LANES), F32),
            pltpu.VMEM((d, tb), F32),
            pltpu.VMEM((eb, tb), BF16),
            pltpu.VMEM((eb, tb), BF16),
            pltpu.VMEM((n_lc, eb // pk, pk, LANES), F32),
            pltpu.VMEM((n_lc, eb // pk, pk, LANES), F32),
        ],
        compiler_params=_cparams(("parallel", "arbitrary")),
        name="peer",
    )(x2, g, w_q, keys1, keys2, u, v_t)


def _ple_kernel(x_ref, p_ref, g_ref, wg_ref, wp_ref, fg_ref, o_ref, *, final):
    x = x_ref[...]
    hn = (x * lax.rsqrt(jnp.mean(x * x, axis=-1, keepdims=True) + EPS) * g_ref[...]).astype(BF16)
    gate = _sigmoid(jnp.dot(hn, wg_ref[...], preferred_element_type=F32))
    proj = jnp.dot(p_ref[...].astype(BF16), wp_ref[...], preferred_element_type=F32)
    y = x + gate * proj
    if final:
        y = y * lax.rsqrt(jnp.mean(y * y, axis=-1, keepdims=True) + EPS) * fg_ref[...]
    o_ref[...] = y


def _ple(x2, p2, g, w_gate, w_proj, final_g, *, final, tm=512):
    t, d = x2.shape
    pd = p2.shape[1]
    full = lambda a: pl.BlockSpec(a.shape, lambda i: (0, 0))
    return pl.pallas_call(
        functools.partial(_ple_kernel, final=final),
        grid=(t // tm,),
        in_specs=[pl.BlockSpec((tm, d), lambda i: (i, 0)), pl.BlockSpec((tm, pd), lambda i: (i, 0)),
                  full(g), full(w_gate), full(w_proj), full(final_g)],
        out_specs=pl.BlockSpec((tm, d), lambda i: (i, 0)),
        out_shape=jax.ShapeDtypeStruct((t, d), F32),
        compiler_params=_cparams(("parallel",)),
        name="ple",
    )(x2, p2, g, w_gate, w_proj, final_g)


def _row(v):
    return v.reshape(1, -1).astype(F32)


def _gate_bias_row(ml_b_i, ml_b_f, dt_bias, fox_b_f):
    r = jnp.zeros((LANES,), F32)
    r = r.at[COL_ML_I:COL_ML_I + ML_HEADS].set(ml_b_i.astype(F32))
    r = r.at[COL_ML_F:COL_ML_F + ML_HEADS].set(ml_b_f.astype(F32))
    r = r.at[COL_DT:COL_DT + SSM_HEADS].set(dt_bias.astype(F32))
    r = r.at[COL_FOX_F:COL_FOX_F + FOX_HEADS].set(fox_b_f.astype(F32))
    return r.reshape(1, LANES)


def _pad_rows(w, rows):
    return jnp.concatenate([w, jnp.zeros((rows - w.shape[0],) + w.shape[1:], w.dtype)], axis=0)


def kernel(x, p, norm_mix_g, w_in, ml_conv_w, ml_conv_b, ml_b_i, ml_b_f, ml_norm_g, ssm_conv_w, ssm_conv_b, ssm_dt_bias, ssm_a_log, ssm_d, ssm_norm_g, fox_q_norm_g, fox_k_norm_g, fox_b_f, w_branch_ml, w_branch_ssm, w_branch_fox, w_out, norm_ffn_g, peer_w_q, peer_keys1, peer_keys2, peer_u, peer_v, norm_ple_g, ple_w_gate, ple_w_proj, final_norm_g):
    bsz, seq, d = x.shape
    depth = w_in.shape[0]
    t = bsz * seq
    ml_qk = ML_HEADS * ML_DQK
    ml_w = ML_HEADS * ML_DV
    ssm_w = SSM_HEADS * SSM_HEADDIM
    ssm_bc = SSM_GROUPS * SSM_STATE
    fox_w = FOX_HEADS * FOX_HEADDIM
    splits = (ml_qk, ml_qk, ml_w, ml_w, ML_HEADS, ML_HEADS, ssm_w, ssm_w, ssm_bc, ssm_bc, SSM_HEADS,
              fox_w, fox_w, fox_w, FOX_HEADS, d, d, d)
    offs = [0]
    for s_ in splits:
        offs.append(offs[-1] + s_)
    (o_mlq, o_mlk, o_mlv, o_mlo, o_mli, o_mlf, o_sz, o_sx, o_sb, o_sc, o_sdt,
     o_fq, o_fk, o_fv, o_ff, o_gml, o_gssm, o_gfox) = offs[:-1]

    main_segments = [(o_mlq, 2 * ml_qk), (o_mlv, ml_w), (o_mlo, ml_w), (o_sz, ssm_w), (o_sx, ssm_w),
                     (o_fq, fox_w), (o_fk, fox_w), (o_fv, fox_w), (o_gml, d), (o_gssm, d), (o_gfox, d),
                     (o_sb, 2 * ssm_bc)]
    seg_off = [0]
    for _, wd in main_segments:
        seg_off.append(seg_off[-1] + wd)
    (c_mlqk, c_mlv, c_mlo, c_sz, c_sx, c_fq, c_fk, c_fv, c_gml, c_gssm, c_gfox, c_bc) = seg_off[:-1]
    assert (c_mlqk, c_mlv, c_mlo) == (0, ml_w, 2 * ml_w)

    x2 = x.reshape(t, d)
    for i in range(depth):
        w = w_in[i]
        w_main = jnp.concatenate([w[:, o:o + wd] for o, wd in main_segments], axis=1).astype(BF16)
        small = jnp.zeros((d, LANES), w.dtype)
        small = small.at[:, COL_ML_I:COL_ML_I + ML_HEADS].set(w[:, o_mli:o_mli + ML_HEADS])
        small = small.at[:, COL_ML_F:COL_ML_F + ML_HEADS].set(w[:, o_mlf:o_mlf + ML_HEADS])
        small = small.at[:, COL_DT:COL_DT + SSM_HEADS].set(w[:, o_sdt:o_sdt + SSM_HEADS])
        small = small.at[:, COL_FOX_F:COL_FOX_F + FOX_HEADS].set(w[:, o_ff:o_ff + FOX_HEADS])
        w_small = small.astype(BF16)
        gate_bias = _gate_bias_row(ml_b_i[i], ml_b_f[i], ssm_dt_bias[i], fox_b_f[i])

        zm, zs = _inproj(x2, _row(norm_mix_g[i]), w_main, w_small)

        y_ml = _mlstm(zm, zs, _pad_rows(ml_conv_w[i].astype(F32), 8), _row(ml_conv_b[i]), gate_bias,
                      _row(ml_norm_g[i]), bsz=bsz, seq=seq)

        scw = ssm_conv_w[i].astype(F32)
        scb = ssm_conv_b[i].astype(F32)
        alog_row = jnp.zeros((LANES,), F32).at[COL_DT:COL_DT + SSM_HEADS].set(
            ssm_a_log[i].astype(F32)).reshape(1, LANES)
        d_row = jnp.repeat(ssm_d[i].astype(F32), SSM_HEADDIM).reshape(1, ssm_w)
        y_ssm = _ssd(zm, zs, _pad_rows(scw[:, :ssm_w], 8), _row(scb[:ssm_w]),
                     _pad_rows(scw[:, ssm_w:], 8), _row(scb[ssm_w:]), gate_bias, alog_row, d_row,
                     _row(ssm_norm_g[i]), bsz=bsz, seq=seq, col_z=c_sz, col_x=c_sx, col_bc=c_bc)

        qn, kn, ft = _fox_prep(zm, zs, gate_bias, _row(fox_q_norm_g[i]), _row(fox_k_norm_g[i]),
                               bsz=bsz, seq=seq, col_q=c_fq, col_k=c_fk)
        y_fox = _fox_attn(qn, kn, zm, ft, bsz=bsz, seq=seq, col_v=c_fv)

        x2 = _merge(x2, y_ml, y_ssm, y_fox, zm, w_branch_ml[i].astype(BF16), w_branch_ssm[i].astype(BF16),
                    w_branch_fox[i].astype(BF16), w_out[i].astype(BF16), col_g=c_gml)

        x2 = _peer(x2, _row(norm_ffn_g[i]), peer_w_q[i].astype(BF16), peer_keys1[i].astype(BF16),
                   peer_keys2[i].astype(BF16), peer_u[i].astype(BF16), peer_v[i])

        x2 = _ple(x2, p[i].reshape(t, -1), _row(norm_ple_g[i]), ple_w_gate[i].astype(BF16),
                  ple_w_proj[i].astype(BF16), _row(final_norm_g), final=(i == depth - 1))
    return x2.reshape(bsz, seq, d)
```

```python
import functools
import math

import jax
import jax.numpy as jnp
from jax import lax
from jax.experimental import pallas as pl
from jax.experimental.pallas import tpu as pltpu

F32 = jnp.float32
BF16 = jnp.bfloat16
EPS = 1e-6
LOG2E = math.log2(math.e)

CONV_WIDTH = 4
ML_HEADS = 8
ML_DQK = 64
ML_DV = 128
SSM_HEADS = 16
SSM_HEADDIM = 64
SSM_GROUPS = 2
SSM_STATE = 128
FOX_HEADS = 8
FOX_HEADDIM = 128
PEER_HEADS = 8
PEER_NKEYS = 128
PEER_QDIM = 256
PEER_TOPK = 16

LANES = 128
SUB = 8
HALO = SUB
MXU_TILE = 256

COL_ML_I = 0
COL_ML_F = COL_ML_I + ML_HEADS
COL_DT = COL_ML_F + ML_HEADS
COL_FOX_F = COL_DT + SSM_HEADS

V7X_VMEM_BYTES = 64 * 1024 * 1024
VMEM_LIMIT = V7X_VMEM_BYTES - 8 * 1024 * 1024


def _cparams(sem):
    return pltpu.CompilerParams(dimension_semantics=sem, vmem_limit_bytes=VMEM_LIMIT)


def _log_sigmoid(v):
    return jnp.minimum(v, 0.0) - jnp.log1p(jnp.exp(-jnp.abs(v)))


def _sigmoid(v):
    return 1.0 / (1.0 + jnp.exp(-v))


def _silu(v):
    return v * _sigmoid(v)


def _softplus(v):
    return jnp.maximum(v, 0.0) + jnp.log1p(jnp.exp(-jnp.abs(v)))


def _tril(n):
    r = lax.broadcasted_iota(jnp.int32, (n, n), 0)
    c = lax.broadcasted_iota(jnp.int32, (n, n), 1)
    return r >= c


def _cumsum_rows(v):
    n = v.shape[0]
    return jnp.dot(_tril(n).astype(F32), v, precision=lax.Precision.HIGHEST,
                   preferred_element_type=F32)


def _inproj_kernel(x_ref, g_ref, wm_ref, ws_ref, zm_ref, zs_ref, h_scr):
    @pl.when(pl.program_id(1) == 0)
    def _():
        x = x_ref[...]
        ms = jnp.mean(x * x, axis=-1, keepdims=True)
        h = (x * lax.rsqrt(ms + EPS) * g_ref[...]).astype(BF16)
        h_scr[...] = h
        zs_ref[...] = jnp.dot(h, ws_ref[...], preferred_element_type=F32)

    zm_ref[...] = jnp.dot(h_scr[...], wm_ref[...], preferred_element_type=F32).astype(zm_ref.dtype)


def _inproj(x2, g, w_main, w_small, tm=2048, tn=512):
    t, d = x2.shape
    tm = min(tm, t)
    n = w_main.shape[1]
    return pl.pallas_call(
        _inproj_kernel,
        grid=(t // tm, n // tn),
        in_specs=[
            pl.BlockSpec((tm, d), lambda i, j: (i, 0)),
            pl.BlockSpec((1, d), lambda i, j: (0, 0)),
            pl.BlockSpec((d, tn), lambda i, j: (0, j)),
            pl.BlockSpec((d, LANES), lambda i, j: (0, 0)),
        ],
        out_specs=[
            pl.BlockSpec((tm, tn), lambda i, j: (i, j)),
            pl.BlockSpec((tm, LANES), lambda i, j: (i, 0)),
        ],
        out_shape=[jax.ShapeDtypeStruct((t, n), BF16), jax.ShapeDtypeStruct((t, LANES), F32)],
        scratch_shapes=[pltpu.VMEM((tm, d), BF16)],
        compiler_params=_cparams(("parallel", "arbitrary")),
        name="inproj",
    )(x2, g, w_main, w_small)


def _conv_chunk(buf_ref, w_ref, b_ref, start, length):
    acc = None
    for j in range(CONV_WIDTH):
        off = HALO - (CONV_WIDTH - 1) + j + start
        term = buf_ref[off:off + length, :] * w_ref[j:j + 1, :]
        acc = term if acc is None else acc + term
    return acc + b_ref[...]


def _mlstm_kernel(qk_ref, v_ref, o_ref, gs_ref, cw_ref, cb_ref, gb_ref, ng_ref, y_ref,
                  buf, c_scr, m_scr, *, ts, chunk):
    nh, dk, dv = ML_HEADS, ML_DQK, ML_DV
    qkw = nh * dk
    bsz = qk_ref.shape[0]

    @pl.when(pl.program_id(0) == 0)
    def _():
        buf[:, 0:HALO, :] = jnp.zeros((bsz, HALO, buf.shape[2]), F32)
        c_scr[...] = jnp.zeros(c_scr.shape, F32)
        m_scr[...] = jnp.zeros(m_scr.shape, F32)

    buf[:, HALO:HALO + ts, :] = qk_ref[...].astype(F32)
    causal = _tril(chunk)
    lane = lax.broadcasted_iota(jnp.int32, (chunk, LANES), 1)
    ones_col = (lane == 0).astype(BF16)

    for c in range(ts // chunk):
        r0 = c * chunk
        for b in range(bsz):
            act = _silu(_conv_chunk(buf.at[b], cw_ref, cb_ref, r0, chunk))
            q_all = (act[:, :qkw] * (dk ** -0.5)).astype(BF16)
            kt_all = act[:, qkw:].T
            gates = gs_ref[b, r0:r0 + chunk, :] + gb_ref[...]
            bcum = _cumsum_rows(_log_sigmoid(gates))
            gates_t = gates.T
            bcum_t = bcum.T
            for h in range(nh):
                i_col = gates[:, COL_ML_I + h:COL_ML_I + h + 1]
                i_row = gates_t[COL_ML_I + h:COL_ML_I + h + 1, :]
                b_col = bcum[:, COL_ML_F + h:COL_ML_F + h + 1]
                b_row = bcum_t[COL_ML_F + h:COL_ML_F + h + 1, :]
                g_tot = b_col[chunk - 1:chunk, :]
                m_prev = m_scr[b, h:h + 1, 0:1]
                dmat = jnp.where(causal, b_col - b_row + i_row, -jnp.inf)
                inter = b_col + m_prev
                m_t = jnp.maximum(inter, jnp.max(dmat, axis=-1, keepdims=True))
                qh = q_all[:, h * dk:(h + 1) * dk]
                kt = kt_all[h * dk:(h + 1) * dk, :]
                s = jnp.dot(qh, kt.astype(BF16), preferred_element_type=F32)
                sc = (s * jnp.exp(dmat - m_t)).astype(BF16)
                w_inter = jnp.exp(inter - m_t)
                vaug = jnp.concatenate([v_ref[b, r0:r0 + chunk, h * dv:(h + 1) * dv], ones_col], axis=1)
                cmem = c_scr[b, h]
                tot = (jnp.dot(sc, vaug, preferred_element_type=F32)
                       + w_inter * jnp.dot(qh, cmem.astype(BF16), preferred_element_type=F32))
                num = tot[:, :dv]
                den = tot[:, dv:dv + 1]
                hv = num / jnp.maximum(jnp.abs(den), jnp.exp(-m_t))
                hn = hv * lax.rsqrt(jnp.mean(hv * hv, axis=-1, keepdims=True) + EPS)
                hn = hn * ng_ref[:, h * dv:(h + 1) * dv]
                og = _sigmoid(o_ref[b, r0:r0 + chunk, h * dv:(h + 1) * dv].astype(F32))
                y_ref[b, r0:r0 + chunk, h * dv:(h + 1) * dv] = (og * hn).astype(y_ref.dtype)
                a_row = g_tot - b_row + i_row
                m_new = jnp.maximum(g_tot + m_prev, jnp.max(a_row, axis=-1, keepdims=True))
                wa_row = jnp.exp(a_row - m_new)
                decay = jnp.exp(g_tot + m_prev - m_new)
                kw = (kt * wa_row).astype(BF16)
                c_scr[b, h] = decay * cmem + jnp.dot(kw, vaug, preferred_element_type=F32)
                m_scr[b, h:h + 1, :] = jnp.broadcast_to(m_new, (1, LANES))

    buf[:, 0:HALO, :] = buf[:, ts:ts + HALO, :]


def _mlstm(zm, zs, conv_w, conv_b, gate_bias, norm_g, *, bsz, seq, ts=256, chunk=256):
    t, n = zm.shape
    w = ML_HEADS * ML_DV
    zm3 = zm.reshape(bsz, seq, n)
    zs3 = zs.reshape(bsz, seq, LANES)
    full = lambda a: pl.BlockSpec(a.shape, lambda s: (0, 0))
    y = pl.pallas_call(
        functools.partial(_mlstm_kernel, ts=ts, chunk=chunk),
        grid=(seq // ts,),
        in_specs=[
            pl.BlockSpec((bsz, ts, w), lambda s: (0, s, 0)),
            pl.BlockSpec((bsz, ts, w), lambda s: (0, s, 1)),
            pl.BlockSpec((bsz, ts, w), lambda s: (0, s, 2)),
            pl.BlockSpec((bsz, ts, LANES), lambda s: (0, s, 0)),
            full(conv_w), full(conv_b), full(gate_bias), full(norm_g),
        ],
        out_specs=pl.BlockSpec((bsz, ts, w), lambda s: (0, s, 0)),
        out_shape=jax.ShapeDtypeStruct((bsz, seq, w), BF16),
        scratch_shapes=[
            pltpu.VMEM((bsz, HALO + ts, w), F32),
            pltpu.VMEM((bsz, ML_HEADS, ML_DQK, 2 * ML_DV), F32),
            pltpu.VMEM((bsz, ML_HEADS, LANES), F32),
        ],
        compiler_params=_cparams(("arbitrary",)),
        name="mlstm",
    )(zm3, zm3, zm3, zs3, conv_w, conv_b, gate_bias, norm_g)
    return y.reshape(t, w)


def _ssd_kernel(z_ref, x_ref, bc_ref, gs_ref, cwx_ref, cbx_ref, cwbc_ref, cbbc_ref, gb_ref,
                alog_ref, d_ref, ng_ref, y_ref, bufx, bufbc, st_scr, *, ts, chunk):
    ng, nst, p = SSM_GROUPS, SSM_STATE, SSM_HEADDIM
    hg = SSM_HEADS // ng
    gw = hg * p
    pairs_per_group = gw // LANES

    @pl.when(pl.program_id(1) == 0)
    def _():
        bufx[0:HALO, :] = jnp.zeros((HALO, bufx.shape[1]), F32)
        bufbc[0:HALO, :] = jnp.zeros((HALO, bufbc.shape[1]), F32)
        st_scr[...] = jnp.zeros(st_scr.shape, F32)

    bufx[HALO:HALO + ts, :] = x_ref[...].astype(F32)
    bufbc[HALO:HALO + ts, :] = bc_ref[...].astype(F32)
    causal = _tril(chunk)
    low_half = lax.broadcasted_iota(jnp.int32, (chunk, LANES), 1) < p
    a_row_all = -jnp.exp(alog_ref[...])

    for c in range(ts // chunk):
        r0 = c * chunk
        xa = _silu(_conv_chunk(bufx, cwx_ref, cbx_ref, r0, chunk))
        bca = _silu(_conv_chunk(bufbc, cwbc_ref, cbbc_ref, r0, chunk))
        dt = _softplus(gs_ref[r0:r0 + chunk, :] + gb_ref[...])
        acum = _cumsum_rows(dt * a_row_all)
        acum_t = acum.T
        y_parts = []
        for g in range(ng):
            bg = bca[:, g * nst:(g + 1) * nst]
            cg = bca[:, ng * nst + g * nst:ng * nst + (g + 1) * nst].astype(BF16)
            bg_t = bg.T
            cbt = jnp.dot(cg, bg_t.astype(BF16), preferred_element_type=F32)
            state = st_scr[g]
            inter = jnp.dot(cg, state.astype(BF16), preferred_element_type=F32)
            xs_scaled, last_parts = [], []
            for pp in range(pairs_per_group):
                pidx = g * pairs_per_group + pp
                ha = 2 * pidx
                ca, cb = COL_DT + ha, COL_DT + ha + 1
                sl = slice(pidx * LANES, (pidx + 1) * LANES)
                xa_p = xa[:, sl]
                dt_pair = jnp.where(low_half, dt[:, ca:ca + 1], dt[:, cb:cb + 1])
                ac_pair = jnp.where(low_half, acum[:, ca:ca + 1], acum[:, cb:cb + 1])
                xs_p = xa_p * dt_pair
                xs_b = xs_p.astype(BF16)
                lm_a = jnp.exp(jnp.where(causal, acum[:, ca:ca + 1] - acum_t[ca:ca + 1, :], -jnp.inf))
                lm_b = jnp.exp(jnp.where(causal, acum[:, cb:cb + 1] - acum_t[cb:cb + 1, :], -jnp.inf))
                ya = jnp.dot((cbt * lm_a).astype(BF16), xs_b, preferred_element_type=F32)
                yb = jnp.dot((cbt * lm_b).astype(BF16), xs_b, preferred_element_type=F32)
                y_p = (jnp.where(low_half, ya, yb)
                       + inter[:, pp * LANES:(pp + 1) * LANES] * jnp.exp(ac_pair)
                       + d_ref[:, sl] * xa_p)
                y_parts.append(y_p)
                last = ac_pair[chunk - 1:chunk, :]
                xs_scaled.append((xs_p * jnp.exp(last - ac_pair)).astype(BF16))
                last_parts.append(last)
            xs_g = jnp.concatenate(xs_scaled, axis=1)
            last_g = jnp.concatenate(last_parts, axis=1)
            st_scr[g] = jnp.exp(last_g) * state + jnp.dot(bg_t.astype(BF16), xs_g,
                                                          preferred_element_type=F32)
        ys = jnp.concatenate(y_parts, axis=1)
        ys = ys * _silu(z_ref[r0:r0 + chunk, :].astype(F32))
        outs = []
        for g in range(ng):
            yg = ys[:, g * gw:(g + 1) * gw]
            yn = yg * lax.rsqrt(jnp.mean(yg * yg, axis=-1, keepdims=True) + EPS)
            outs.append(yn * ng_ref[:, g * gw:(g + 1) * gw])
        y_ref[r0:r0 + chunk, :] = jnp.concatenate(outs, axis=1).astype(y_ref.dtype)

    bufx[0:HALO, :] = bufx[ts:ts + HALO, :]
    bufbc[0:HALO, :] = bufbc[ts:ts + HALO, :]


def _ssd(zm, zs, cwx, cbx, cwbc, cbbc, gate_bias, alog_row, d_row, norm_g, *, bsz, seq,
         col_z, col_x, col_bc, ts=256, chunk=128):
    t = zm.shape[0]
    w = SSM_HEADS * SSM_HEADDIM
    bcw = 2 * SSM_GROUPS * SSM_STATE
    nblk = seq // ts
    row = lambda b, s: b * nblk + s
    full = lambda a: pl.BlockSpec(a.shape, lambda b, s: (0, 0))
    return pl.pallas_call(
        functools.partial(_ssd_kernel, ts=ts, chunk=chunk),
        grid=(bsz, nblk),
        in_specs=[
            pl.BlockSpec((ts, w), lambda b, s: (row(b, s), col_z // w)),
            pl.BlockSpec((ts, w), lambda b, s: (row(b, s), col_x // w)),
            pl.BlockSpec((ts, bcw), lambda b, s: (row(b, s), col_bc // bcw)),
            pl.BlockSpec((ts, LANES), lambda b, s: (row(b, s), 0)),
            full(cwx), full(cbx), full(cwbc), full(cbbc), full(gate_bias),
            full(alog_row), full(d_row), full(norm_g),
        ],
        out_specs=pl.BlockSpec((ts, w), lambda b, s: (row(b, s), 0)),
        out_shape=jax.ShapeDtypeStruct((t, w), BF16),
        scratch_shapes=[
            pltpu.VMEM((HALO + ts, w), F32),
            pltpu.VMEM((HALO + ts, bcw), F32),
            pltpu.VMEM((SSM_GROUPS, SSM_STATE, w // SSM_GROUPS), F32),
        ],
        compiler_params=_cparams(("parallel", "arbitrary")),
        name="ssd",
    )(zm, zm, zm, zs, cwx, cbx, cwbc, cbbc, gate_bias, alog_row, d_row, norm_g)


def _fox_prep_kernel(q_ref, k_ref, gs_ref, gb_ref, gq_ref, gk_ref, qn_ref, kn_ref, ft_ref,
                     carry, *, ts):
    d = FOX_HEADDIM

    @pl.when(pl.program_id(1) == 0)
    def _():
        carry[...] = jnp.zeros(carry.shape, F32)

    for h in range(FOX_HEADS):
        sl = slice(h * d, (h + 1) * d)
        qh = q_ref[:, sl].astype(F32)
        qn = qh * lax.rsqrt(jnp.mean(qh * qh, axis=-1, keepdims=True) + EPS) * gq_ref[...]
        qn_ref[:, sl] = qn.astype(qn_ref.dtype)
        kh = k_ref[:, sl].astype(F32)
        kn = kh * lax.rsqrt(jnp.mean(kh * kh, axis=-1, keepdims=True) + EPS) * gk_ref[...]
        kn_ref[:, sl] = (kn * (d ** -0.5 * LOG2E)).astype(kn_ref.dtype)

    lf = _log_sigmoid(gs_ref[...] + gb_ref[...])
    fc = _cumsum_rows(lf) + carry[0:1, :]
    carry[0:1, :] = fc[ts - 1:ts, :]
    ft_ref[...] = fc.T[COL_FOX_F:COL_FOX_F + FOX_HEADS, :] * LOG2E


def _fox_prep(zm, zs, gate_bias, gq, gk, *, bsz, seq, col_q, col_k, ts=256):
    t = zm.shape[0]
    w = FOX_HEADS * FOX_HEADDIM
    nblk = seq // ts
    row = lambda b, s: b * nblk + s
    full = lambda a: pl.BlockSpec(a.shape, lambda b, s: (0, 0))
    return pl.pallas_call(
        functools.partial(_fox_prep_kernel, ts=ts),
        grid=(bsz, nblk),
        in_specs=[
            pl.BlockSpec((ts, w), lambda b, s: (row(b, s), col_q // w)),
            pl.BlockSpec((ts, w), lambda b, s: (row(b, s), col_k // w)),
            pl.BlockSpec((ts, LANES), lambda b, s: (row(b, s), 0)),
            full(gate_bias), full(gq), full(gk),
        ],
        out_specs=[
            pl.BlockSpec((ts, w), lambda b, s: (row(b, s), 0)),
            pl.BlockSpec((ts, w), lambda b, s: (row(b, s), 0)),
            pl.BlockSpec((None, FOX_HEADS, ts), lambda b, s: (b, 0, s)),
        ],
        out_shape=[jax.ShapeDtypeStruct((t, w), BF16), jax.ShapeDtypeStruct((t, w), BF16),
                   jax.ShapeDtypeStruct((bsz, FOX_HEADS, seq), F32)],
        scratch_shapes=[pltpu.VMEM((8, LANES), F32)],
        compiler_params=_cparams(("parallel", "arbitrary")),
        name="fox_prep",
    )(zm, zm, zs, gate_bias, gq, gk)


def _fox_attn_kernel(qi_ref, kj_ref, q_ref, k_ref, v_ref, f_ref, o_ref, m_scr, acc_scr, vaug_scr,
                     *, tq, tk, rb, rb_diag):
    pr = pl.program_id(2)
    qi = qi_ref[pr]
    kj = kj_ref[pr]
    h = pl.program_id(1)
    d = FOX_HEADDIM
    ratio = tq // tk
    per_tile = tk // rb

    @pl.when(kj == 0)
    def _():
        m_scr[...] = jnp.full(m_scr.shape, -jnp.inf, F32)
        acc_scr[...] = jnp.zeros(acc_scr.shape, F32)

    vaug_scr[:, :d] = v_ref[...]
    vaug_scr[:, d:] = (lax.broadcasted_iota(jnp.int32, (tk, d), 1) == 0).astype(BF16)
    fk_all = f_ref[pl.ds(h, 1), :]

    def row_block(row0, nrows, diag_row0):
        rows = slice(row0, row0 + nrows)
        ncol = tk if diag_row0 is None else diag_row0 + nrows
        s = lax.dot_general(q_ref[rows, :], k_ref[:ncol, :], (((1,), (1,)), ((), ())),
                            preferred_element_type=F32) - fk_all[:, :ncol]
        if diag_row0 is not None:
            tail = jnp.where(_tril(nrows), s[:, ncol - nrows:], -jnp.inf)
            s = tail if ncol == nrows else jnp.concatenate([s[:, :ncol - nrows], tail], axis=1)
        m_old = m_scr[rows, :]
        m_new = jnp.maximum(m_old, jnp.max(s, axis=-1, keepdims=True))
        alpha = jnp.exp2(m_old - m_new)
        p = jnp.exp2(s - m_new[:, :1]).astype(BF16)
        pv = jnp.dot(p, vaug_scr[:ncol, :], preferred_element_type=F32)
        acc_scr[rows, :] = jnp.concatenate([alpha, alpha], axis=1) * acc_scr[rows, :] + pv
        m_scr[rows, :] = m_new

    @pl.when(kj < qi * ratio)
    def _():
        for r in range(tq // rb):
            row_block(r * rb, rb, None)

    for c in range(ratio):
        @pl.when(kj == qi * ratio + c)
        def _(c=c):
            for r in reversed(range(tk // rb_diag)):
                row_block(c * tk + r * rb_diag, rb_diag, r * rb_diag)
            for r in range((c + 1) * per_tile, tq // rb):
                row_block(r * rb, rb, None)

    @pl.when(kj == qi * ratio + ratio - 1)
    def _():
        acc = acc_scr[...]
        o_ref[...] = (acc[:, :d] / acc[:, d:d + 1]).astype(o_ref.dtype)


def _fox_attn(qn, kn, zm, ft, *, bsz, seq, col_v, tk=2048, ratio=1, rb=128, rb_diag=512):
    t = qn.shape[0]
    d = FOX_HEADDIM
    tk = min(tk, seq // ratio)
    tq = ratio * tk
    rb = min(rb, tk)
    rb_diag = min(rb_diag, tk)
    nq = seq // tq
    nkt = seq // tk
    qi_tab, kj_tab = [], []
    for i in range(nq):
        for j in range(ratio * (i + 1)):
            qi_tab.append(i)
            kj_tab.append(j)
    qi_tab = jnp.asarray(qi_tab, jnp.int32)
    kj_tab = jnp.asarray(kj_tab, jnp.int32)
    vblk = col_v // d
    grid_spec = pltpu.PrefetchScalarGridSpec(
        num_scalar_prefetch=2,
        grid=(bsz, FOX_HEADS, qi_tab.shape[0]),
        in_specs=[
            pl.BlockSpec((tq, d), lambda b, h, p, qi, kj: (b * nq + qi[p], h)),
            pl.BlockSpec((tk, d), lambda b, h, p, qi, kj: (b * nkt + kj[p], h)),
            pl.BlockSpec((tk, d), lambda b, h, p, qi, kj: (b * nkt + kj[p], vblk + h)),
            pl.BlockSpec((None, FOX_HEADS, tk), lambda b, h, p, qi, kj: (b, 0, kj[p])),
        ],
        out_specs=pl.BlockSpec((tq, d), lambda b, h, p, qi, kj: (b * nq + qi[p], h)),
        scratch_shapes=[pltpu.VMEM((tq, LANES), F32), pltpu.VMEM((tq, 2 * d), F32),
                        pltpu.VMEM((tk, 2 * d), BF16)],
    )
    return pl.pallas_call(
        functools.partial(_fox_attn_kernel, tq=tq, tk=tk, rb=rb, rb_diag=rb_diag),
        grid_spec=grid_spec,
        out_shape=jax.ShapeDtypeStruct((t, FOX_HEADS * d), BF16),
        compiler_params=_cparams(("parallel", "parallel", "arbitrary")),
        name="fox_attn",
    )(qi_tab, kj_tab, qn, kn, zm, ft)


def _merge_kernel(x_ref, yml_ref, yssm_ref, yfox_ref, gml_ref, gssm_ref, gfox_ref,
                  pml_ref, pssm_ref, pfox_ref, wout_ref, o_ref):
    def branch(y_ref, g_ref, p_ref):
        proj = jnp.dot(y_ref[...], p_ref[...], preferred_element_type=F32)
        return _sigmoid(g_ref[...].astype(F32)) * proj

    merged = (branch(yml_ref, gml_ref, pml_ref) + branch(yssm_ref, gssm_ref, pssm_ref)
              + branch(yfox_ref, gfox_ref, pfox_ref))
    o_ref[...] = x_ref[...] + jnp.dot(merged.astype(BF16), wout_ref[...], preferred_element_type=F32)


def _merge(x2, y_ml, y_ssm, y_fox, zm, p_ml, p_ssm, p_fox, w_out, *, col_g, tm=512):
    t, d = x2.shape
    rowblk = lambda c: pl.BlockSpec((tm, d), lambda i: (i, c))
    wfull = lambda a: pl.BlockSpec(a.shape, lambda i: (0, 0))
    gblk = col_g // d
    return pl.pallas_call(
        _merge_kernel,
        grid=(t // tm,),
        in_specs=[rowblk(0), rowblk(0), rowblk(0), rowblk(0),
                  rowblk(gblk), rowblk(gblk + 1), rowblk(gblk + 2),
                  wfull(p_ml), wfull(p_ssm), wfull(p_fox), wfull(w_out)],
        out_specs=rowblk(0),
        out_shape=jax.ShapeDtypeStruct((t, d), F32),
        compiler_params=_cparams(("parallel",)),
        name="merge",
    )(x2, y_ml, y_ssm, y_fox, zm, zm, zm, p_ml, p_ssm, p_fox, w_out)


def _erf(v):
    return lax.erf(v)


def _gelu_x2(v):
    return v * (1.0 + _erf(v * (2.0 ** -0.5)))


def _sorted_top(tiles):
    n = len(tiles)
    a = list(tiles)

    def keep_larger_first(i, j):
        a[i], a[j] = jnp.maximum(a[i], a[j]), jnp.minimum(a[i], a[j])

    def merge(lo_stride):
        j = lo_stride
        while j >= 1:
            for i in range(n):
                if i ^ j > i:
                    keep_larger_first(i, i ^ j)
            j //= 2

    k = 2
    while k <= n:
        j = k // 2
        while j >= 1:
            for i in range(n):
                l = i ^ j
                if l > i:
                    if i & k == 0:
                        keep_larger_first(i, l)
                    else:
                        keep_larger_first(l, i)
            j //= 2
        k *= 2
    shift = SUB // 2
    while shift >= 1:
        b = [pltpu.roll(x, shift, 0) for x in a]
        a = [jnp.maximum(a[j], b[n - 1 - j]) for j in range(n)]
        merge(n // 2)
        shift //= 2
    return a


def _as_rows(vals):
    sub = lax.broadcasted_iota(jnp.int32, (SUB, LANES), 0)
    tiles = []
    for g in range(len(vals) // SUB):
        t = vals[SUB * g]
        for r in range(1, SUB):
            t = jnp.where(sub == r, vals[SUB * g + r], t)
        tiles.append(t)
    return tiles


def _candidate_tiles(v1, v2, v1_rows, v2_rows, k):
    sub = lax.broadcasted_iota(jnp.int32, (SUB, LANES), 0)
    tiles = [v1[0] + v2_rows[g] for g in range(k // SUB)]
    a = 1
    while k // (a + 1) > 1:
        nb = k // (a + 1)
        assert nb <= SUB
        t = v1[a] + v2_rows[0]
        tiles.append(t if nb == SUB else jnp.where(sub < nb, t, -jnp.inf))
        a += 1
    assert a % SUB == 0
    tiles.extend(v1_rows[g] + v2[0] for g in range(a // SUB, k // SUB))
    assert len(tiles) <= 16
    tiles.extend([jnp.full((SUB, LANES), -jnp.inf, F32)] * (16 - len(tiles)))
    return tiles


def _peer_kernel(x_ref, g_ref, wq_ref, k1_ref, k2_ref, u_ref, vt_ref, o_ref,
                 ht_scr, q_scr, theta_scr, e1_scr, s2_scr, e2_scr, acc_scr, g0_scr, g1_scr,
                 w0_scr, w1_scr, *, tb, eb, n_e):
    j = pl.program_id(1)
    nk, topk, nhead = PEER_NKEYS, PEER_TOPK, PEER_HEADS
    assert nk // SUB == topk
    half = PEER_QDIM // 2
    n_lc = tb // LANES
    pk = SUB

    @pl.when(j == 0)
    def _():
        x = x_ref[...]
        hn32 = x * lax.rsqrt(jnp.mean(x * x, axis=-1, keepdims=True) + EPS) * g_ref[...]
        ht_scr[...] = hn32.T.astype(BF16)
        q = jnp.dot(hn32.astype(BF16), wq_ref[...], preferred_element_type=F32).astype(BF16)
        for c in range(q.shape[1] // half):
            q_scr[c] = q[:, c * half:(c + 1) * half]
        acc_scr[...] = jnp.zeros(acc_scr.shape, F32)

        def head_body(h, carry):
            nt = (((1,), (1,)), ((), ()))
            s1 = lax.dot_general(k1_ref[h], q_scr[2 * h], nt, preferred_element_type=F32)
            s2 = lax.dot_general(k2_ref[h], q_scr[2 * h + 1], nt, preferred_element_type=F32)
            for lc in range(n_lc):
                ls = slice(lc * LANES, (lc + 1) * LANES)
                s1c = s1[:, ls].reshape(nk // SUB, SUB, LANES)
                s2c = s2[:, ls].reshape(nk // SUB, SUB, LANES)
                v1 = _sorted_top([s1c[i] for i in range(nk // SUB)])
                v2 = _sorted_top([s2c[i] for i in range(nk // SUB)])
                v1_rows, v2_rows = _as_rows(v1), _as_rows(v2)
                tv = _sorted_top(_candidate_tiles(v1, v2, v1_rows, v2_rows, topk))
                tau, mx = tv[topk - 1], tv[0]
                zsum = jnp.zeros_like(mx)
                for a in range(topk):
                    zsum = zsum + jnp.exp(tv[a] - mx)
                th_rows = [jnp.full((SUB, LANES), jnp.inf, F32) for _ in v1_rows]
                for b in range(topk):
                    th_rows = [jnp.where(v1_rows[g] + v2[b] >= tau, v2[b], th_rows[g])
                               for g in range(len(v1_rows))]
                theta = jnp.full(s1c.shape, jnp.inf, F32)
                for a in range(topk):
                    th_a = jnp.broadcast_to(th_rows[a // SUB][a % SUB:a % SUB + 1, :], (SUB, LANES))
                    theta = jnp.where(s1c == v1[a][None], th_a[None], theta)
                theta_scr[h, lc] = theta.reshape(nk, LANES)
                e1_scr[h, lc] = (jnp.exp(s1c - v1[0][None]) * (0.5 / zsum)[None]).reshape(nk, LANES)
                s2_scr[h, lc] = s2c
                e2_scr[h, lc] = jnp.exp(s2c - v2[0][None])
            return carry

        lax.fori_loop(0, nhead, head_body, 0)

    n_r = eb // nk

    def stage_a(w_wr, lc, r_range):
        for r in r_range:
            i1 = j * n_r + r
            wsum = None
            for h in range(nhead):
                th = jnp.broadcast_to(theta_scr[h, lc, pl.ds(i1, 1), :], (pk, LANES))
                e1 = jnp.broadcast_to(e1_scr[h, lc, pl.ds(i1, 1), :], (pk, LANES))
                term = jnp.where(s2_scr[h, lc] >= th[None], e2_scr[h, lc] * e1[None], 0.0)
                wsum = term if wsum is None else wsum + term
            w_wr[lc, r * (nk // pk):(r + 1) * (nk // pk)] = wsum

    cpt = MXU_TILE // LANES
    e_piece = eb // cpt
    d_piece = acc_scr.shape[0] // cpt

    def stage_b(w_rd, g_wr, n, m):
        ts_ = slice(n * MXU_TILE, (n + 1) * MXU_TILE)
        es = slice(m * e_piece, (m + 1) * e_piece)
        ps = slice(m * (e_piece // pk), (m + 1) * (e_piece // pk))
        act = _gelu_x2(jnp.dot(u_ref[es, :], ht_scr[:, ts_], preferred_element_type=F32))
        w_prev = jnp.concatenate([w_rd[cpt * n + c, ps].reshape(e_piece, LANES) for c in range(cpt)], axis=1)
        g_wr[es, ts_] = (act * w_prev).astype(BF16)

    def stage_c(g_rd, n, m):
        ts_ = slice(n * MXU_TILE, (n + 1) * MXU_TILE)
        ds_ = slice(m * d_piece, (m + 1) * d_piece)
        acc_scr[ds_, ts_] += jnp.dot(vt_ref[ds_, :], g_rd[:, ts_], preferred_element_type=F32)

    def step(parity, do_a=True, do_b=True, do_c=True):
        w_wr, w_rd, g_wr, g_rd = ((w0_scr, w1_scr, g1_scr, g0_scr) if parity == 0
                                  else (w1_scr, w0_scr, g0_scr, g1_scr))
        for n in range(n_lc // cpt):
            for m in range(cpt):
                if do_c:
                    stage_c(g_rd, n, m)
                if do_a:
                    stage_a(w_wr, cpt * n + m, range(0, n_r // 2))
                if do_b:
                    stage_b(w_rd, g_wr, n, m)
                if do_a:
                    stage_a(w_wr, cpt * n + m, range(n_r // 2, n_r))

    @pl.when(j == 0)
    def _():
        step(0, do_b=False, do_c=False)

    @pl.when(j == 1)
    def _():
        step(1, do_c=False)

    steady = jnp.logical_and(j >= 2, j < n_e)

    @pl.when(jnp.logical_and(steady, j % 2 == 0))
    def _():
        step(0)

    @pl.when(jnp.logical_and(steady, j % 2 == 1))
    def _():
        step(1)

    @pl.when(j == n_e)
    def _():
        step(n_e % 2, do_a=False)

    @pl.when(j == n_e + 1)
    def _():
        step((n_e + 1) % 2, do_a=False, do_b=False)

    @pl.when(j == n_e + 1)
    def _():
        o_ref[...] = x_ref[...] + acc_scr[...].T


def _peer(x2, g, w_q, keys1, keys2, u, v, *, tb=512, eb=512):
    t, d = x2.shape
    tb = min(tb, t)
    ne = u.shape[0]
    nq = w_q.shape[1]
    n_e = ne // eb
    n_lc = tb // LANES
    pk = SUB
    v_t = v.reshape(n_e, eb, d).transpose(0, 2, 1).astype(BF16)
    once = pl.Buffered(1)
    return pl.pallas_call(
        functools.partial(_peer_kernel, tb=tb, eb=eb, n_e=n_e),
        grid=(t // tb, n_e + 2),
        in_specs=[
            pl.BlockSpec((tb, d), lambda i, j: (i, 0), pipeline_mode=once),
            pl.BlockSpec((1, d), lambda i, j: (0, 0)),
            pl.BlockSpec((d, nq), lambda i, j: (0, 0), pipeline_mode=once),
            pl.BlockSpec(keys1.shape, lambda i, j: (0, 0, 0)),
            pl.BlockSpec(keys2.shape, lambda i, j: (0, 0, 0)),
            pl.BlockSpec((eb, d), lambda i, j: (jnp.clip(j - 1, 0, n_e - 1), 0)),
            pl.BlockSpec((None, d, eb), lambda i, j: (jnp.clip(j - 2, 0, n_e - 1), 0, 0)),
        ],
        out_specs=pl.BlockSpec((tb, d), lambda i, j: (i, 0)),
        out_shape=jax.ShapeDtypeStruct((t, d), F32),
        scratch_shapes=[
            pltpu.VMEM((d, tb), BF16),
            pltpu.VMEM((2 * PEER_HEADS, tb, PEER_QDIM // 2), BF16),
            pltpu.VMEM((PEER_HEADS, n_lc, PEER_NKEYS, LANES), F32),
            pltpu.VMEM((PEER_HEADS, n_lc, PEER_NKEYS, LANES), F32),
            pltpu.VMEM((PEER_HEADS, n_lc, PEER_NKEYS // pk, pk, LANES), F32),
            pltpu.VMEM((PEER_HEADS, n_lc, PEER_NKEYS // pk, pk, LANES), F32),
            pltpu.VMEM((d, tb), F32),
            pltpu.VMEM((eb, tb), BF16),
            pltpu.VMEM((eb, tb), BF16),
            pltpu.VMEM((n_lc, eb // pk, pk, LANES), F32),
            pltpu.VMEM((n_lc, eb // pk, pk, LANES), F32),
        ],
        compiler_params=_cparams(("parallel", "arbitrary")),
        name="peer",
    )(x2, g, w_q, keys1, keys2, u, v_t)


def _ple_kernel(x_ref, p_ref, g_ref, wg_ref, wp_ref, fg_ref, o_ref, *, final):
    x = x_ref[...]
    hn = (x * lax.rsqrt(jnp.mean(x * x, axis=-1, keepdims=True) + EPS) * g_ref[...]).astype(BF16)
    gate = _sigmoid(jnp.dot(hn, wg_ref[...], preferred_element_type=F32))
    proj = jnp.dot(p_ref[...].astype(BF16), wp_ref[...], preferred_element_type=F32)
    y = x + gate * proj
    if final:
        y = y * lax.rsqrt(jnp.mean(y * y, axis=-1, keepdims=True) + EPS) * fg_ref[...]
    o_ref[...] = y


def _ple(x2, p2, g, w_gate, w_proj, final_g, *, final, tm=512):
    t, d = x2.shape
    pd = p2.shape[1]
    full = lambda a: pl.BlockSpec(a.shape, lambda i: (0, 0))
    return pl.pallas_call(
        functools.partial(_ple_kernel, final=final),
        grid=(t // tm,),
        in_specs=[pl.BlockSpec((tm, d), lambda i: (i, 0)), pl.BlockSpec((tm, pd), lambda i: (i, 0)),
                  full(g), full(w_gate), full(w_proj), full(final_g)],
        out_specs=pl.BlockSpec((tm, d), lambda i: (i, 0)),
        out_shape=jax.ShapeDtypeStruct((t, d), F32),
        compiler_params=_cparams(("parallel",)),
        name="ple",
    )(x2, p2, g, w_gate, w_proj, final_g)


def _row(v):
    return v.reshape(1, -1).astype(F32)


def _small_gate_block(ml_i, ml_f, dt, fox_f):
    parts = [ml_i, ml_f, dt, fox_f]
    used = sum(p_.shape[-1] for p_ in parts)
    pad = jnp.zeros(ml_i.shape[:-1] + (LANES - used,), ml_i.dtype)
    return jnp.concatenate(parts + [pad], axis=-1)


def _pad_rows(w, rows):
    return jnp.concatenate([w, jnp.zeros((rows - w.shape[0],) + w.shape[1:], w.dtype)], axis=0)


def kernel(x, p, norm_mix_g, w_in, ml_conv_w, ml_conv_b, ml_b_i, ml_b_f, ml_norm_g, ssm_conv_w, ssm_conv_b, ssm_dt_bias, ssm_a_log, ssm_d, ssm_norm_g, fox_q_norm_g, fox_k_norm_g, fox_b_f, w_branch_ml, w_branch_ssm, w_branch_fox, w_out, norm_ffn_g, peer_w_q, peer_keys1, peer_keys2, peer_u, peer_v, norm_ple_g, ple_w_gate, ple_w_proj, final_norm_g):
    bsz, seq, d = x.shape
    depth = w_in.shape[0]
    t = bsz * seq
    ml_qk = ML_HEADS * ML_DQK
    ml_w = ML_HEADS * ML_DV
    ssm_w = SSM_HEADS * SSM_HEADDIM
    ssm_bc = SSM_GROUPS * SSM_STATE
    fox_w = FOX_HEADS * FOX_HEADDIM
    splits = (ml_qk, ml_qk, ml_w, ml_w, ML_HEADS, ML_HEADS, ssm_w, ssm_w, ssm_bc, ssm_bc, SSM_HEADS,
              fox_w, fox_w, fox_w, FOX_HEADS, d, d, d)
    offs = [0]
    for s_ in splits:
        offs.append(offs[-1] + s_)
    (o_mlq, o_mlk, o_mlv, o_mlo, o_mli, o_mlf, o_sz, o_sx, o_sb, o_sc, o_sdt,
     o_fq, o_fk, o_fv, o_ff, o_gml, o_gssm, o_gfox) = offs[:-1]

    main_segments = [(o_mlq, 2 * ml_qk), (o_mlv, ml_w), (o_mlo, ml_w), (o_sz, ssm_w), (o_sx, ssm_w),
                     (o_fq, fox_w), (o_fk, fox_w), (o_fv, fox_w), (o_gml, d), (o_gssm, d), (o_gfox, d),
                     (o_sb, 2 * ssm_bc)]
    seg_off = [0]
    for _, wd in main_segments:
        seg_off.append(seg_off[-1] + wd)
    (c_mlqk, c_mlv, c_mlo, c_sz, c_sx, c_fq, c_fk, c_fv, c_gml, c_gssm, c_gfox, c_bc) = seg_off[:-1]
    assert (c_mlqk, c_mlv, c_mlo) == (0, ml_w, 2 * ml_w)

    x2 = x.reshape(t, d)
    for i in range(depth):
        w = w_in[i]
        w_main = jnp.concatenate([w[:, o:o + wd] for o, wd in main_segments], axis=1).astype(BF16)
        w_small = _small_gate_block(w[:, o_mli:o_mli + ML_HEADS], w[:, o_mlf:o_mlf + ML_HEADS],
                                    w[:, o_sdt:o_sdt + SSM_HEADS], w[:, o_ff:o_ff + FOX_HEADS]).astype(BF16)
        gate_bias = _small_gate_block(ml_b_i[i], ml_b_f[i], ssm_dt_bias[i],
                                      fox_b_f[i]).astype(F32).reshape(1, LANES)

        zm, zs = _inproj(x2, _row(norm_mix_g[i]), w_main, w_small)

        y_ml = _mlstm(zm, zs, _pad_rows(ml_conv_w[i].astype(F32), 8), _row(ml_conv_b[i]), gate_bias,
                      _row(ml_norm_g[i]), bsz=bsz, seq=seq)

        scw = ssm_conv_w[i].astype(F32)
        scb = ssm_conv_b[i].astype(F32)
        no_heads = jnp.zeros((ML_HEADS,), ssm_a_log.dtype)
        alog_row = _small_gate_block(no_heads, no_heads, ssm_a_log[i],
                                     no_heads).astype(F32).reshape(1, LANES)
        d_row = jnp.repeat(ssm_d[i].astype(F32), SSM_HEADDIM).reshape(1, ssm_w)
        y_ssm = _ssd(zm, zs, _pad_rows(scw[:, :ssm_w], 8), _row(scb[:ssm_w]),
                     _pad_rows(scw[:, ssm_w:], 8), _row(scb[ssm_w:]), gate_bias, alog_row, d_row,
                     _row(ssm_norm_g[i]), bsz=bsz, seq=seq, col_z=c_sz, col_x=c_sx, col_bc=c_bc)

        qn, kn, ft = _fox_prep(zm, zs, gate_bias, _row(fox_q_norm_g[i]), _row(fox_k_norm_g[i]),
                               bsz=bsz, seq=seq, col_q=c_fq, col_k=c_fk)
        y_fox = _fox_attn(qn, kn, zm, ft, bsz=bsz, seq=seq, col_v=c_fv)

        x2 = _merge(x2, y_ml, y_ssm, y_fox, zm, w_branch_ml[i].astype(BF16), w_branch_ssm[i].astype(BF16),
                    w_branch_fox[i].astype(BF16), w_out[i].astype(BF16), col_g=c_gml)

        x2 = _peer(x2, _row(norm_ffn_g[i]), peer_w_q[i].astype(BF16), peer_keys1[i].astype(BF16),
                   peer_keys2[i].astype(BF16), peer_u[i].astype(BF16), peer_v[i])

        x2 = _ple(x2, p[i].reshape(t, -1), _row(norm_ple_g[i]), ple_w_gate[i].astype(BF16),
                  ple_w_proj[i].astype(BF16), _row(final_norm_g), final=(i == depth - 1))
    return x2.reshape(bsz, seq, d)
```

```python
import functools
import math

import jax
import jax.numpy as jnp
from jax import lax
from jax.experimental import pallas as pl
from jax.experimental.pallas import tpu as pltpu

F32 = jnp.float32
BF16 = jnp.bfloat16
EPS = 1e-6
LOG2E = math.log2(math.e)

CONV_WIDTH = 4
ML_HEADS = 8
ML_DQK = 64
ML_DV = 128
SSM_HEADS = 16
SSM_HEADDIM = 64
SSM_GROUPS = 2
SSM_STATE = 128
FOX_HEADS = 8
FOX_HEADDIM = 128
PEER_HEADS = 8
PEER_NKEYS = 128
PEER_QDIM = 256
PEER_TOPK = 16

LANES = 128
SUB = 8
HALO = SUB
MXU_TILE = 256

COL_ML_I = 0
COL_ML_F = COL_ML_I + ML_HEADS
COL_DT = COL_ML_F + ML_HEADS
COL_FOX_F = COL_DT + SSM_HEADS

V7X_VMEM_BYTES = 64 * 1024 * 1024
VMEM_LIMIT = V7X_VMEM_BYTES - 8 * 1024 * 1024


def _cparams(sem):
    return pltpu.CompilerParams(dimension_semantics=sem, vmem_limit_bytes=VMEM_LIMIT)


def _log_sigmoid(v):
    return jnp.minimum(v, 0.0) - jnp.log1p(jnp.exp(-jnp.abs(v)))


def _sigmoid(v):
    return 1.0 / (1.0 + jnp.exp(-v))


def _silu(v):
    return v * _sigmoid(v)


def _softplus(v):
    return jnp.maximum(v, 0.0) + jnp.log1p(jnp.exp(-jnp.abs(v)))


def _tril(n):
    r = lax.broadcasted_iota(jnp.int32, (n, n), 0)
    c = lax.broadcasted_iota(jnp.int32, (n, n), 1)
    return r >= c


def _cumsum_rows(v):
    n = v.shape[0]
    return jnp.dot(_tril(n).astype(F32), v, precision=lax.Precision.HIGHEST,
                   preferred_element_type=F32)


def _inproj_kernel(x_ref, g_ref, wm_ref, ws_ref, zm_ref, zs_ref, h_scr):
    @pl.when(pl.program_id(1) == 0)
    def _():
        x = x_ref[...]
        ms = jnp.mean(x * x, axis=-1, keepdims=True)
        h = (x * lax.rsqrt(ms + EPS) * g_ref[...]).astype(BF16)
        h_scr[...] = h
        zs_ref[...] = jnp.dot(h, ws_ref[...], preferred_element_type=F32)

    zm_ref[...] = jnp.dot(h_scr[...], wm_ref[...], preferred_element_type=F32).astype(zm_ref.dtype)


def _inproj(x2, g, w_main, w_small, tm=2048, tn=512):
    t, d = x2.shape
    tm = min(tm, t)
    n = w_main.shape[1]
    return pl.pallas_call(
        _inproj_kernel,
        grid=(t // tm, n // tn),
        in_specs=[
            pl.BlockSpec((tm, d), lambda i, j: (i, 0)),
            pl.BlockSpec((1, d), lambda i, j: (0, 0)),
            pl.BlockSpec((d, tn), lambda i, j: (0, j)),
            pl.BlockSpec((d, LANES), lambda i, j: (0, 0)),
        ],
        out_specs=[
            pl.BlockSpec((tm, tn), lambda i, j: (i, j)),
            pl.BlockSpec((tm, LANES), lambda i, j: (i, 0)),
        ],
        out_shape=[jax.ShapeDtypeStruct((t, n), BF16), jax.ShapeDtypeStruct((t, LANES), F32)],
        scratch_shapes=[pltpu.VMEM((tm, d), BF16)],
        compiler_params=_cparams(("parallel", "arbitrary")),
        name="inproj",
    )(x2, g, w_main, w_small)


def _conv_chunk(buf_ref, w_ref, b_ref, start, length):
    acc = None
    for j in range(CONV_WIDTH):
        off = HALO - (CONV_WIDTH - 1) + j + start
        term = buf_ref[off:off + length, :] * w_ref[j:j + 1, :]
        acc = term if acc is None else acc + term
    return acc + b_ref[...]


def _mlstm_kernel(qk_ref, v_ref, o_ref, gs_ref, cw_ref, cb_ref, gb_ref, ng_ref, y_ref,
                  buf, c_scr, m_scr, *, ts, chunk):
    nh, dk, dv = ML_HEADS, ML_DQK, ML_DV
    qkw = nh * dk
    bsz = qk_ref.shape[0]

    @pl.when(pl.program_id(0) == 0)
    def _():
        buf[:, 0:HALO, :] = jnp.zeros((bsz, HALO, buf.shape[2]), F32)
        c_scr[...] = jnp.zeros(c_scr.shape, F32)
        m_scr[...] = jnp.zeros(m_scr.shape, F32)

    buf[:, HALO:HALO + ts, :] = qk_ref[...].astype(F32)
    causal = _tril(chunk)
    lane = lax.broadcasted_iota(jnp.int32, (chunk, LANES), 1)
    ones_col = (lane == 0).astype(BF16)

    for c in range(ts // chunk):
        r0 = c * chunk
        for b in range(bsz):
            act = _silu(_conv_chunk(buf.at[b], cw_ref, cb_ref, r0, chunk))
            q_all = (act[:, :qkw] * (dk ** -0.5)).astype(BF16)
            kt_all = act[:, qkw:].T
            gates = gs_ref[b, r0:r0 + chunk, :] + gb_ref[...]
            bcum = _cumsum_rows(_log_sigmoid(gates))
            gates_t = gates.T
            bcum_t = bcum.T
            for h in range(nh):
                i_col = gates[:, COL_ML_I + h:COL_ML_I + h + 1]
                i_row = gates_t[COL_ML_I + h:COL_ML_I + h + 1, :]
                b_col = bcum[:, COL_ML_F + h:COL_ML_F + h + 1]
                b_row = bcum_t[COL_ML_F + h:COL_ML_F + h + 1, :]
                g_tot = b_col[chunk - 1:chunk, :]
                m_prev = m_scr[b, h:h + 1, 0:1]
                dmat = jnp.where(causal, b_col - b_row + i_row, -jnp.inf)
                inter = b_col + m_prev
                m_t = jnp.maximum(inter, jnp.max(dmat, axis=-1, keepdims=True))
                qh = q_all[:, h * dk:(h + 1) * dk]
                kt = kt_all[h * dk:(h + 1) * dk, :]
                s = jnp.dot(qh, kt.astype(BF16), preferred_element_type=F32)
                sc = (s * jnp.exp(dmat - m_t)).astype(BF16)
                w_inter = jnp.exp(inter - m_t)
                vaug = jnp.concatenate([v_ref[b, r0:r0 + chunk, h * dv:(h + 1) * dv], ones_col], axis=1)
                cmem = c_scr[b, h]
                tot = (jnp.dot(sc, vaug, preferred_element_type=F32)
                       + w_inter * jnp.dot(qh, cmem.astype(BF16), preferred_element_type=F32))
                num = tot[:, :dv]
                den = tot[:, dv:dv + 1]
                hv = num / jnp.maximum(jnp.abs(den), jnp.exp(-m_t))
                hn = hv * lax.rsqrt(jnp.mean(hv * hv, axis=-1, keepdims=True) + EPS)
                hn = hn * ng_ref[:, h * dv:(h + 1) * dv]
                og = _sigmoid(o_ref[b, r0:r0 + chunk, h * dv:(h + 1) * dv].astype(F32))
                y_ref[b, r0:r0 + chunk, h * dv:(h + 1) * dv] = (og * hn).astype(y_ref.dtype)
                a_row = g_tot - b_row + i_row
                m_new = jnp.maximum(g_tot + m_prev, jnp.max(a_row, axis=-1, keepdims=True))
                wa_row = jnp.exp(a_row - m_new)
                decay = jnp.exp(g_tot + m_prev - m_new)
                kw = (kt * wa_row).astype(BF16)
                c_scr[b, h] = decay * cmem + jnp.dot(kw, vaug, preferred_element_type=F32)
                m_scr[b, h:h + 1, :] = jnp.broadcast_to(m_new, (1, LANES))

    buf[:, 0:HALO, :] = buf[:, ts:ts + HALO, :]


def _mlstm(zm, zs, conv_w, conv_b, gate_bias, norm_g, *, bsz, seq, ts=256, chunk=256):
    t, n = zm.shape
    w = ML_HEADS * ML_DV
    zm3 = zm.reshape(bsz, seq, n)
    zs3 = zs.reshape(bsz, seq, LANES)
    full = lambda a: pl.BlockSpec(a.shape, lambda s: (0, 0))
    y = pl.pallas_call(
        functools.partial(_mlstm_kernel, ts=ts, chunk=chunk),
        grid=(seq // ts,),
        in_specs=[
            pl.BlockSpec((bsz, ts, w), lambda s: (0, s, 0)),
            pl.BlockSpec((bsz, ts, w), lambda s: (0, s, 1)),
            pl.BlockSpec((bsz, ts, w), lambda s: (0, s, 2)),
            pl.BlockSpec((bsz, ts, LANES), lambda s: (0, s, 0)),
            full(conv_w), full(conv_b), full(gate_bias), full(norm_g),
        ],
        out_specs=pl.BlockSpec((bsz, ts, w), lambda s: (0, s, 0)),
        out_shape=jax.ShapeDtypeStruct((bsz, seq, w), BF16),
        scratch_shapes=[
            pltpu.VMEM((bsz, HALO + ts, w), F32),
            pltpu.VMEM((bsz, ML_HEADS, ML_DQK, 2 * ML_DV), F32),
            pltpu.VMEM((bsz, ML_HEADS, LANES), F32),
        ],
        compiler_params=_cparams(("arbitrary",)),
        name="mlstm",
    )(zm3, zm3, zm3, zs3, conv_w, conv_b, gate_bias, norm_g)
    return y.reshape(t, w)


def _ssd_kernel(z_ref, x_ref, bc_ref, gs_ref, cwx_ref, cbx_ref, cwbc_ref, cbbc_ref, gb_ref,
                alog_ref, d_ref, ng_ref, y_ref, bufx, bufbc, st_scr, *, ts, chunk):
    ng, nst, p = SSM_GROUPS, SSM_STATE, SSM_HEADDIM
    hg = SSM_HEADS // ng
    gw = hg * p
    pairs_per_group = gw // LANES

    @pl.when(pl.program_id(1) == 0)
    def _():
        bufx[0:HALO, :] = jnp.zeros((HALO, bufx.shape[1]), F32)
        bufbc[0:HALO, :] = jnp.zeros((HALO, bufbc.shape[1]), F32)
        st_scr[...] = jnp.zeros(st_scr.shape, F32)

    bufx[HALO:HALO + ts, :] = x_ref[...].astype(F32)
    bufbc[HALO:HALO + ts, :] = bc_ref[...].astype(F32)
    causal = _tril(chunk)
    low_half = lax.broadcasted_iota(jnp.int32, (chunk, LANES), 1) < p
    a_row_all = -jnp.exp(alog_ref[...])

    for c in range(ts // chunk):
        r0 = c * chunk
        xa = _silu(_conv_chunk(bufx, cwx_ref, cbx_ref, r0, chunk))
        bca = _silu(_conv_chunk(bufbc, cwbc_ref, cbbc_ref, r0, chunk))
        dt = _softplus(gs_ref[r0:r0 + chunk, :] + gb_ref[...])
        acum = _cumsum_rows(dt * a_row_all)
        acum_t = acum.T
        y_parts = []
        for g in range(ng):
            bg = bca[:, g * nst:(g + 1) * nst]
            cg = bca[:, ng * nst + g * nst:ng * nst + (g + 1) * nst].astype(BF16)
            bg_t = bg.T
            cbt = jnp.dot(cg, bg_t.astype(BF16), preferred_element_type=F32)
            state = st_scr[g]
            inter = jnp.dot(cg, state.astype(BF16), preferred_element_type=F32)
            xs_scaled, last_parts = [], []
            for pp in range(pairs_per_group):
                pidx = g * pairs_per_group + pp
                ha = 2 * pidx
                ca, cb = COL_DT + ha, COL_DT + ha + 1
                sl = slice(pidx * LANES, (pidx + 1) * LANES)
                xa_p = xa[:, sl]
                dt_pair = jnp.where(low_half, dt[:, ca:ca + 1], dt[:, cb:cb + 1])
                ac_pair = jnp.where(low_half, acum[:, ca:ca + 1], acum[:, cb:cb + 1])
                xs_p = xa_p * dt_pair
                xs_b = xs_p.astype(BF16)
                lm_a = jnp.exp(jnp.where(causal, acum[:, ca:ca + 1] - acum_t[ca:ca + 1, :], -jnp.inf))
                lm_b = jnp.exp(jnp.where(causal, acum[:, cb:cb + 1] - acum_t[cb:cb + 1, :], -jnp.inf))
                ya = jnp.dot((cbt * lm_a).astype(BF16), xs_b, preferred_element_type=F32)
                yb = jnp.dot((cbt * lm_b).astype(BF16), xs_b, preferred_element_type=F32)
                y_p = (jnp.where(low_half, ya, yb)
                       + inter[:, pp * LANES:(pp + 1) * LANES] * jnp.exp(ac_pair)
                       + d_ref[:, sl] * xa_p)
                y_parts.append(y_p)
                last = ac_pair[chunk - 1:chunk, :]
                xs_scaled.append((xs_p * jnp.exp(last - ac_pair)).astype(BF16))
                last_parts.append(last)
            xs_g = jnp.concatenate(xs_scaled, axis=1)
            last_g = jnp.concatenate(last_parts, axis=1)
            st_scr[g] = jnp.exp(last_g) * state + jnp.dot(bg_t.astype(BF16), xs_g,
                                                          preferred_element_type=F32)
        ys = jnp.concatenate(y_parts, axis=1)
        ys = ys * _silu(z_ref[r0:r0 + chunk, :].astype(F32))
        outs = []
        for g in range(ng):
            yg = ys[:, g * gw:(g + 1) * gw]
            yn = yg * lax.rsqrt(jnp.mean(yg * yg, axis=-1, keepdims=True) + EPS)
            outs.append(yn * ng_ref[:, g * gw:(g + 1) * gw])
        y_ref[r0:r0 + chunk, :] = jnp.concatenate(outs, axis=1).astype(y_ref.dtype)

    bufx[0:HALO, :] = bufx[ts:ts + HALO, :]
    bufbc[0:HALO, :] = bufbc[ts:ts + HALO, :]


def _ssd(zm, zs, cwx, cbx, cwbc, cbbc, gate_bias, alog_row, d_row, norm_g, *, bsz, seq,
         col_z, col_x, col_bc, ts=256, chunk=128):
    t = zm.shape[0]
    w = SSM_HEADS * SSM_HEADDIM
    bcw = 2 * SSM_GROUPS * SSM_STATE
    nblk = seq // ts
    row = lambda b, s: b * nblk + s
    full = lambda a: pl.BlockSpec(a.shape, lambda b, s: (0, 0))
    return pl.pallas_call(
        functools.partial(_ssd_kernel, ts=ts, chunk=chunk),
        grid=(bsz, nblk),
        in_specs=[
            pl.BlockSpec((ts, w), lambda b, s: (row(b, s), col_z // w)),
            pl.BlockSpec((ts, w), lambda b, s: (row(b, s), col_x // w)),
            pl.BlockSpec((ts, bcw), lambda b, s: (row(b, s), col_bc // bcw)),
            pl.BlockSpec((ts, LANES), lambda b, s: (row(b, s), 0)),
            full(cwx), full(cbx), full(cwbc), full(cbbc), full(gate_bias),
            full(alog_row), full(d_row), full(norm_g),
        ],
        out_specs=pl.BlockSpec((ts, w), lambda b, s: (row(b, s), 0)),
        out_shape=jax.ShapeDtypeStruct((t, w), BF16),
        scratch_shapes=[
            pltpu.VMEM((HALO + ts, w), F32),
            pltpu.VMEM((HALO + ts, bcw), F32),
            pltpu.VMEM((SSM_GROUPS, SSM_STATE, w // SSM_GROUPS), F32),
        ],
        compiler_params=_cparams(("parallel", "arbitrary")),
        name="ssd",
    )(zm, zm, zm, zs, cwx, cbx, cwbc, cbbc, gate_bias, alog_row, d_row, norm_g)


def _fox_prep_kernel(q_ref, k_ref, gs_ref, gb_ref, gq_ref, gk_ref, qn_ref, kn_ref, ft_ref,
                     carry, *, ts):
    d = FOX_HEADDIM

    @pl.when(pl.program_id(1) == 0)
    def _():
        carry[...] = jnp.zeros(carry.shape, F32)

    for h in range(FOX_HEADS):
        sl = slice(h * d, (h + 1) * d)
        qh = q_ref[:, sl].astype(F32)
        qn = qh * lax.rsqrt(jnp.mean(qh * qh, axis=-1, keepdims=True) + EPS) * gq_ref[...]
        qn_ref[:, sl] = qn.astype(qn_ref.dtype)
        kh = k_ref[:, sl].astype(F32)
        kn = kh * lax.rsqrt(jnp.mean(kh * kh, axis=-1, keepdims=True) + EPS) * gk_ref[...]
        kn_ref[:, sl] = (kn * (d ** -0.5 * LOG2E)).astype(kn_ref.dtype)

    lf = _log_sigmoid(gs_ref[...] + gb_ref[...])
    fc = _cumsum_rows(lf) + carry[0:1, :]
    carry[0:1, :] = fc[ts - 1:ts, :]
    ft_ref[...] = fc.T[COL_FOX_F:COL_FOX_F + FOX_HEADS, :] * LOG2E


def _fox_prep(zm, zs, gate_bias, gq, gk, *, bsz, seq, col_q, col_k, ts=256):
    t = zm.shape[0]
    w = FOX_HEADS * FOX_HEADDIM
    nblk = seq // ts
    row = lambda b, s: b * nblk + s
    full = lambda a: pl.BlockSpec(a.shape, lambda b, s: (0, 0))
    return pl.pallas_call(
        functools.partial(_fox_prep_kernel, ts=ts),
        grid=(bsz, nblk),
        in_specs=[
            pl.BlockSpec((ts, w), lambda b, s: (row(b, s), col_q // w)),
            pl.BlockSpec((ts, w), lambda b, s: (row(b, s), col_k // w)),
            pl.BlockSpec((ts, LANES), lambda b, s: (row(b, s), 0)),
            full(gate_bias), full(gq), full(gk),
        ],
        out_specs=[
            pl.BlockSpec((ts, w), lambda b, s: (row(b, s), 0)),
            pl.BlockSpec((ts, w), lambda b, s: (row(b, s), 0)),
            pl.BlockSpec((None, FOX_HEADS, ts), lambda b, s: (b, 0, s)),
        ],
        out_shape=[jax.ShapeDtypeStruct((t, w), BF16), jax.ShapeDtypeStruct((t, w), BF16),
                   jax.ShapeDtypeStruct((bsz, FOX_HEADS, seq), F32)],
        scratch_shapes=[pltpu.VMEM((8, LANES), F32)],
        compiler_params=_cparams(("parallel", "arbitrary")),
        name="fox_prep",
    )(zm, zm, zs, gate_bias, gq, gk)


def _fox_attn_kernel(qi_ref, kj_ref, q_ref, k_ref, v_ref, f_ref, o_ref, m_scr, acc_scr, vaug_scr,
                     *, tq, tk, rb, rb_diag):
    pr = pl.program_id(2)
    qi = qi_ref[pr]
    kj = kj_ref[pr]
    h = pl.program_id(1)
    d = FOX_HEADDIM
    ratio = tq // tk
    per_tile = tk // rb

    @pl.when(kj == 0)
    def _():
        m_scr[...] = jnp.full(m_scr.shape, -jnp.inf, F32)
        acc_scr[...] = jnp.zeros(acc_scr.shape, F32)

    vaug_scr[:, :d] = v_ref[...]
    vaug_scr[:, d:] = (lax.broadcasted_iota(jnp.int32, (tk, d), 1) == 0).astype(BF16)
    fk_all = f_ref[pl.ds(h, 1), :]

    def row_block(row0, nrows, diag_row0):
        rows = slice(row0, row0 + nrows)
        ncol = tk if diag_row0 is None else diag_row0 + nrows
        s = lax.dot_general(q_ref[rows, :], k_ref[:ncol, :], (((1,), (1,)), ((), ())),
                            preferred_element_type=F32) - fk_all[:, :ncol]
        if diag_row0 is not None:
            tail = jnp.where(_tril(nrows), s[:, ncol - nrows:], -jnp.inf)
            s = tail if ncol == nrows else jnp.concatenate([s[:, :ncol - nrows], tail], axis=1)
        m_old = m_scr[rows, :]
        m_new = jnp.maximum(m_old, jnp.max(s, axis=-1, keepdims=True))
        alpha = jnp.exp2(m_old - m_new)
        p = jnp.exp2(s - m_new[:, :1]).astype(BF16)
        pv = jnp.dot(p, vaug_scr[:ncol, :], preferred_element_type=F32)
        acc_scr[rows, :] = jnp.concatenate([alpha, alpha], axis=1) * acc_scr[rows, :] + pv
        m_scr[rows, :] = m_new

    @pl.when(kj < qi * ratio)
    def _():
        for r in range(tq // rb):
            row_block(r * rb, rb, None)

    for c in range(ratio):
        @pl.when(kj == qi * ratio + c)
        def _(c=c):
            for r in reversed(range(tk // rb_diag)):
                row_block(c * tk + r * rb_diag, rb_diag, r * rb_diag)
            for r in range((c + 1) * per_tile, tq // rb):
                row_block(r * rb, rb, None)

    @pl.when(kj == qi * ratio + ratio - 1)
    def _():
        acc = acc_scr[...]
        o_ref[...] = (acc[:, :d] / acc[:, d:d + 1]).astype(o_ref.dtype)


def _fox_attn(qn, kn, zm, ft, *, bsz, seq, col_v, tk=2048, ratio=1, rb=128, rb_diag=512):
    t = qn.shape[0]
    d = FOX_HEADDIM
    tk = min(tk, seq // ratio)
    tq = ratio * tk
    rb = min(rb, tk)
    rb_diag = min(rb_diag, tk)
    nq = seq // tq
    nkt = seq // tk
    qi_tab, kj_tab = [], []
    for i in range(nq):
        for j in range(ratio * (i + 1)):
            qi_tab.append(i)
            kj_tab.append(j)
    qi_tab = jnp.asarray(qi_tab, jnp.int32)
    kj_tab = jnp.asarray(kj_tab, jnp.int32)
    vblk = col_v // d
    grid_spec = pltpu.PrefetchScalarGridSpec(
        num_scalar_prefetch=2,
        grid=(bsz, FOX_HEADS, qi_tab.shape[0]),
        in_specs=[
            pl.BlockSpec((tq, d), lambda b, h, p, qi, kj: (b * nq + qi[p], h)),
            pl.BlockSpec((tk, d), lambda b, h, p, qi, kj: (b * nkt + kj[p], h)),
            pl.BlockSpec((tk, d), lambda b, h, p, qi, kj: (b * nkt + kj[p], vblk + h)),
            pl.BlockSpec((None, FOX_HEADS, tk), lambda b, h, p, qi, kj: (b, 0, kj[p])),
        ],
        out_specs=pl.BlockSpec((tq, d), lambda b, h, p, qi, kj: (b * nq + qi[p], h)),
        scratch_shapes=[pltpu.VMEM((tq, LANES), F32), pltpu.VMEM((tq, 2 * d), F32),
                        pltpu.VMEM((tk, 2 * d), BF16)],
    )
    return pl.pallas_call(
        functools.partial(_fox_attn_kernel, tq=tq, tk=tk, rb=rb, rb_diag=rb_diag),
        grid_spec=grid_spec,
        out_shape=jax.ShapeDtypeStruct((t, FOX_HEADS * d), BF16),
        compiler_params=_cparams(("parallel", "parallel", "arbitrary")),
        name="fox_attn",
    )(qi_tab, kj_tab, qn, kn, zm, ft)


def _merge_kernel(x_ref, yml_ref, yssm_ref, yfox_ref, gml_ref, gssm_ref, gfox_ref,
                  pml_ref, pssm_ref, pfox_ref, wout_ref, o_ref):
    def branch(y_ref, g_ref, p_ref):
        proj = jnp.dot(y_ref[...], p_ref[...], preferred_element_type=F32)
        return _sigmoid(g_ref[...].astype(F32)) * proj

    merged = (branch(yml_ref, gml_ref, pml_ref) + branch(yssm_ref, gssm_ref, pssm_ref)
              + branch(yfox_ref, gfox_ref, pfox_ref))
    o_ref[...] = x_ref[...] + jnp.dot(merged.astype(BF16), wout_ref[...], preferred_element_type=F32)


def _merge(x2, y_ml, y_ssm, y_fox, zm, p_ml, p_ssm, p_fox, w_out, *, col_g, tm=512):
    t, d = x2.shape
    rowblk = lambda c: pl.BlockSpec((tm, d), lambda i: (i, c))
    wfull = lambda a: pl.BlockSpec(a.shape, lambda i: (0, 0))
    gblk = col_g // d
    return pl.pallas_call(
        _merge_kernel,
        grid=(t // tm,),
        in_specs=[rowblk(0), rowblk(0), rowblk(0), rowblk(0),
                  rowblk(gblk), rowblk(gblk + 1), rowblk(gblk + 2),
                  wfull(p_ml), wfull(p_ssm), wfull(p_fox), wfull(w_out)],
        out_specs=rowblk(0),
        out_shape=jax.ShapeDtypeStruct((t, d), F32),
        compiler_params=_cparams(("parallel",)),
        name="merge",
    )(x2, y_ml, y_ssm, y_fox, zm, zm, zm, p_ml, p_ssm, p_fox, w_out)


def _erf(v):
    return lax.erf(v)


def _gelu_x2(v):
    return v * (1.0 + _erf(v * (2.0 ** -0.5)))


def _sorted_top(tiles):
    n = len(tiles)
    a = list(tiles)

    def keep_larger_first(i, j):
        a[i], a[j] = jnp.maximum(a[i], a[j]), jnp.minimum(a[i], a[j])

    def merge(lo_stride):
        j = lo_stride
        while j >= 1:
            for i in range(n):
                if i ^ j > i:
                    keep_larger_first(i, i ^ j)
            j //= 2

    k = 2
    while k <= n:
        j = k // 2
        while j >= 1:
            for i in range(n):
                l = i ^ j
                if l > i:
                    if i & k == 0:
                        keep_larger_first(i, l)
                    else:
                        keep_larger_first(l, i)
            j //= 2
        k *= 2
    shift = SUB // 2
    while shift >= 1:
        b = [pltpu.roll(x, shift, 0) for x in a]
        a = [jnp.maximum(a[j], b[n - 1 - j]) for j in range(n)]
        merge(n // 2)
        shift //= 2
    return a


def _as_rows(vals):
    sub = lax.broadcasted_iota(jnp.int32, (SUB, LANES), 0)
    tiles = []
    for g in range(len(vals) // SUB):
        t = vals[SUB * g]
        for r in range(1, SUB):
            t = jnp.where(sub == r, vals[SUB * g + r], t)
        tiles.append(t)
    return tiles


def _candidate_tiles(v1, v2, v1_rows, v2_rows, k):
    sub = lax.broadcasted_iota(jnp.int32, (SUB, LANES), 0)
    tiles = [v1[0] + v2_rows[g] for g in range(k // SUB)]
    a = 1
    while k // (a + 1) > 1:
        nb = k // (a + 1)
        assert nb <= SUB
        t = v1[a] + v2_rows[0]
        tiles.append(t if nb == SUB else jnp.where(sub < nb, t, -jnp.inf))
        a += 1
    assert a % SUB == 0
    tiles.extend(v1_rows[g] + v2[0] for g in range(a // SUB, k // SUB))
    assert len(tiles) <= 16
    tiles.extend([jnp.full((SUB, LANES), -jnp.inf, F32)] * (16 - len(tiles)))
    return tiles


def _peer_kernel(x_ref, g_ref, wq_ref, k1_ref, k2_ref, u_ref, vt_ref, o_ref,
                 ht_scr, q_scr, theta_scr, e1_scr, s2_scr, e2_scr, acc_scr, g0_scr, g1_scr,
                 w0_scr, w1_scr, *, tb, eb, n_e):
    j = pl.program_id(1)
    nk, topk, nhead = PEER_NKEYS, PEER_TOPK, PEER_HEADS
    assert nk // SUB == topk
    half = PEER_QDIM // 2
    n_lc = tb // LANES
    pk = SUB

    @pl.when(j == 0)
    def _():
        x = x_ref[...]
        hn32 = x * lax.rsqrt(jnp.mean(x * x, axis=-1, keepdims=True) + EPS) * g_ref[...]
        ht_scr[...] = hn32.T.astype(BF16)
        q = jnp.dot(hn32.astype(BF16), wq_ref[...], preferred_element_type=F32).astype(BF16)
        for c in range(q.shape[1] // half):
            q_scr[c] = q[:, c * half:(c + 1) * half]
        acc_scr[...] = jnp.zeros(acc_scr.shape, F32)

        def head_body(h, carry):
            nt = (((1,), (1,)), ((), ()))
            s1 = lax.dot_general(k1_ref[h], q_scr[2 * h], nt, preferred_element_type=F32)
            s2 = lax.dot_general(k2_ref[h], q_scr[2 * h + 1], nt, preferred_element_type=F32)
            for lc in range(n_lc):
                ls = slice(lc * LANES, (lc + 1) * LANES)
                s1c = s1[:, ls].reshape(nk // SUB, SUB, LANES)
                s2c = s2[:, ls].reshape(nk // SUB, SUB, LANES)
                v1 = _sorted_top([s1c[i] for i in range(nk // SUB)])
                v2 = _sorted_top([s2c[i] for i in range(nk // SUB)])
                v1_rows, v2_rows = _as_rows(v1), _as_rows(v2)
                tv = _sorted_top(_candidate_tiles(v1, v2, v1_rows, v2_rows, topk))
                tau, mx = tv[topk - 1], tv[0]
                zsum = jnp.zeros_like(mx)
                for a in range(topk):
                    zsum = zsum + jnp.exp(tv[a] - mx)
                th_rows = [jnp.full((SUB, LANES), jnp.inf, F32) for _ in v1_rows]
                for b in range(topk):
                    th_rows = [jnp.where(v1_rows[g] + v2[b] >= tau, v2[b], th_rows[g])
                               for g in range(len(v1_rows))]
                theta = jnp.full(s1c.shape, jnp.inf, F32)
                for a in range(topk):
                    th_a = jnp.broadcast_to(th_rows[a // SUB][a % SUB:a % SUB + 1, :], (SUB, LANES))
                    theta = jnp.where(s1c == v1[a][None], th_a[None], theta)
                theta_scr[h, lc] = theta.reshape(nk, LANES)
                e1_scr[h, lc] = (jnp.exp(s1c - v1[0][None]) * (0.5 / zsum)[None]).reshape(nk, LANES)
                s2_scr[h, lc] = s2c
                e2_scr[h, lc] = jnp.exp(s2c - v2[0][None])
            return carry

        lax.fori_loop(0, nhead, head_body, 0)

    n_r = eb // nk

    def stage_a(w_wr, lc, r_range):
        for r in r_range:
            i1 = j * n_r + r
            wsum = None
            for h in range(nhead):
                th = jnp.broadcast_to(theta_scr[h, lc, pl.ds(i1, 1), :], (pk, LANES))
                e1 = jnp.broadcast_to(e1_scr[h, lc, pl.ds(i1, 1), :], (pk, LANES))
                term = jnp.where(s2_scr[h, lc] >= th[None], e2_scr[h, lc] * e1[None], 0.0)
                wsum = term if wsum is None else wsum + term
            w_wr[lc, r * (nk // pk):(r + 1) * (nk // pk)] = wsum

    cpt = MXU_TILE // LANES
    e_piece = eb // cpt
    d_piece = acc_scr.shape[0] // cpt

    def stage_b(w_rd, g_wr, n, m):
        ts_ = slice(n * MXU_TILE, (n + 1) * MXU_TILE)
        es = slice(m * e_piece, (m + 1) * e_piece)
        ps = slice(m * (e_piece // pk), (m + 1) * (e_piece // pk))
        act = _gelu_x2(jnp.dot(u_ref[es, :], ht_scr[:, ts_], preferred_element_type=F32))
        w_prev = jnp.concatenate([w_rd[cpt * n + c, ps].reshape(e_piece, LANES) for c in range(cpt)], axis=1)
        g_wr[es, ts_] = (act * w_prev).astype(BF16)

    def stage_c(g_rd, n, m):
        ts_ = slice(n * MXU_TILE, (n + 1) * MXU_TILE)
        ds_ = slice(m * d_piece, (m + 1) * d_piece)
        acc_scr[ds_, ts_] += jnp.dot(vt_ref[ds_, :], g_rd[:, ts_], preferred_element_type=F32)

    def step(parity, do_a=True, do_b=True, do_c=True):
        w_wr, w_rd, g_wr, g_rd = ((w0_scr, w1_scr, g1_scr, g0_scr) if parity == 0
                                  else (w1_scr, w0_scr, g0_scr, g1_scr))
        for n in range(n_lc // cpt):
            for m in range(cpt):
                if do_b:
                    stage_b(w_rd, g_wr, n, m)
                if do_a:
                    stage_a(w_wr, cpt * n + m, range(0, n_r // 2))
                if do_c:
                    stage_c(g_rd, n, m)
                if do_a:
                    stage_a(w_wr, cpt * n + m, range(n_r // 2, n_r))

    @pl.when(j == 0)
    def _():
        step(0, do_b=False, do_c=False)

    @pl.when(j == 1)
    def _():
        step(1, do_c=False)

    steady = jnp.logical_and(j >= 2, j < n_e)

    @pl.when(jnp.logical_and(steady, j % 2 == 0))
    def _():
        step(0)

    @pl.when(jnp.logical_and(steady, j % 2 == 1))
    def _():
        step(1)

    @pl.when(j == n_e)
    def _():
        step(n_e % 2, do_a=False)

    @pl.when(j == n_e + 1)
    def _():
        step((n_e + 1) % 2, do_a=False, do_b=False)

    @pl.when(j == n_e + 1)
    def _():
        o_ref[...] = x_ref[...] + acc_scr[...].T


def _peer(x2, g, w_q, keys1, keys2, u, v, *, tb=512, eb=512):
    t, d = x2.shape
    tb = min(tb, t)
    ne = u.shape[0]
    nq = w_q.shape[1]
    n_e = ne // eb
    n_lc = tb // LANES
    pk = SUB
    v_t = v.reshape(n_e, eb, d).transpose(0, 2, 1).astype(BF16)
    once = pl.Buffered(1)
    return pl.pallas_call(
        functools.partial(_peer_kernel, tb=tb, eb=eb, n_e=n_e),
        grid=(t // tb, n_e + 2),
        in_specs=[
            pl.BlockSpec((tb, d), lambda i, j: (i, 0), pipeline_mode=once),
            pl.BlockSpec((1, d), lambda i, j: (0, 0)),
            pl.BlockSpec((d, nq), lambda i, j: (0, 0), pipeline_mode=once),
            pl.BlockSpec(keys1.shape, lambda i, j: (0, 0, 0)),
            pl.BlockSpec(keys2.shape, lambda i, j: (0, 0, 0)),
            pl.BlockSpec((eb, d), lambda i, j: (jnp.clip(j - 1, 0, n_e - 1), 0)),
            pl.BlockSpec((None, d, eb), lambda i, j: (jnp.clip(j - 2, 0, n_e - 1), 0, 0)),
        ],
        out_specs=pl.BlockSpec((tb, d), lambda i, j: (i, 0)),
        out_shape=jax.ShapeDtypeStruct((t, d), F32),
        scratch_shapes=[
            pltpu.VMEM((d, tb), BF16),
            pltpu.VMEM((2 * PEER_HEADS, tb, PEER_QDIM // 2), BF16),
            pltpu.VMEM((PEER_HEADS, n_lc, PEER_NKEYS, LANES), F32),
            pltpu.VMEM((PEER_HEADS, n_lc, PEER_NKEYS, LANES), F32),
            pltpu.VMEM((PEER_HEADS, n_lc, PEER_NKEYS // pk, pk, LANES), F32),
            pltpu.VMEM((PEER_HEADS, n_lc, PEER_NKEYS // pk, pk, LANES), F32),
            pltpu.VMEM((d, tb), F32),
            pltpu.VMEM((eb, tb), BF16),
            pltpu.VMEM((eb, tb), BF16),
            pltpu.VMEM((n_lc, eb // pk, pk, LANES), F32),
            pltpu.VMEM((n_lc, eb // pk, pk, LANES), F32),
        ],
        compiler_params=_cparams(("parallel", "arbitrary")),
        name="peer",
    )(x2, g, w_q, keys1, keys2, u, v_t)


def _ple_kernel(x_ref, p_ref, g_ref, wg_ref, wp_ref, fg_ref, o_ref, *, final):
    x = x_ref[...]
    hn = (x * lax.rsqrt(jnp.mean(x * x, axis=-1, keepdims=True) + EPS) * g_ref[...]).astype(BF16)
    gate = _sigmoid(jnp.dot(hn, wg_ref[...], preferred_element_type=F32))
    proj = jnp.dot(p_ref[...].astype(BF16), wp_ref[...], preferred_element_type=F32)
    y = x + gate * proj
    if final:
        y = y * lax.rsqrt(jnp.mean(y * y, axis=-1, keepdims=True) + EPS) * fg_ref[...]
    o_ref[...] = y


def _ple(x2, p2, g, w_gate, w_proj, final_g, *, final, tm=512):
    t, d = x2.shape
    pd = p2.shape[1]
    full = lambda a: pl.BlockSpec(a.shape, lambda i: (0, 0))
    return pl.pallas_call(
        functools.partial(_ple_kernel, final=final),
        grid=(t // tm,),
        in_specs=[pl.BlockSpec((tm, d), lambda i: (i, 0)), pl.BlockSpec((tm, pd), lambda i: (i, 0)),
                  full(g), full(w_gate), full(w_proj), full(final_g)],
        out_specs=pl.BlockSpec((tm, d), lambda i: (i, 0)),
        out_shape=jax.ShapeDtypeStruct((t, d), F32),
        compiler_params=_cparams(("parallel",)),
        name="ple",
    )(x2, p2, g, w_gate, w_proj, final_g)


def _row(v):
    return v.reshape(1, -1).astype(F32)


def _small_gate_block(ml_i, ml_f, dt, fox_f):
    parts = [ml_i, ml_f, dt, fox_f]
    used = sum(p_.shape[-1] for p_ in parts)
    pad = jnp.zeros(ml_i.shape[:-1] + (LANES - used,), ml_i.dtype)
    return jnp.concatenate(parts + [pad], axis=-1)


def _pad_rows(w, rows):
    return jnp.concatenate([w, jnp.zeros((rows - w.shape[0],) + w.shape[1:], w.dtype)], axis=0)


def kernel(x, p, norm_mix_g, w_in, ml_conv_w, ml_conv_b, ml_b_i, ml_b_f, ml_norm_g, ssm_conv_w, ssm_conv_b, ssm_dt_bias, ssm_a_log, ssm_d, ssm_norm_g, fox_q_norm_g, fox_k_norm_g, fox_b_f, w_branch_ml, w_branch_ssm, w_branch_fox, w_out, norm_ffn_g, peer_w_q, peer_keys1, peer_keys2, peer_u, peer_v, norm_ple_g, ple_w_gate, ple_w_proj, final_norm_g):
    bsz, seq, d = x.shape
    depth = w_in.shape[0]
    t = bsz * seq
    ml_qk = ML_HEADS * ML_DQK
    ml_w = ML_HEADS * ML_DV
    ssm_w = SSM_HEADS * SSM_HEADDIM
    ssm_bc = SSM_GROUPS * SSM_STATE
    fox_w = FOX_HEADS * FOX_HEADDIM
    splits = (ml_qk, ml_qk, ml_w, ml_w, ML_HEADS, ML_HEADS, ssm_w, ssm_w, ssm_bc, ssm_bc, SSM_HEADS,
              fox_w, fox_w, fox_w, FOX_HEADS, d, d, d)
    offs = [0]
    for s_ in splits:
        offs.append(offs[-1] + s_)
    (o_mlq, o_mlk, o_mlv, o_mlo, o_mli, o_mlf, o_sz, o_sx, o_sb, o_sc, o_sdt,
     o_fq, o_fk, o_fv, o_ff, o_gml, o_gssm, o_gfox) = offs[:-1]

    main_segments = [(o_mlq, 2 * ml_qk), (o_mlv, ml_w), (o_mlo, ml_w), (o_sz, ssm_w), (o_sx, ssm_w),
                     (o_fq, fox_w), (o_fk, fox_w), (o_fv, fox_w), (o_gml, d), (o_gssm, d), (o_gfox, d),
                     (o_sb, 2 * ssm_bc)]
    seg_off = [0]
    for _, wd in main_segments:
        seg_off.append(seg_off[-1] + wd)
    (c_mlqk, c_mlv, c_mlo, c_sz, c_sx, c_fq, c_fk, c_fv, c_gml, c_gssm, c_gfox, c_bc) = seg_off[:-1]
    assert (c_mlqk, c_mlv, c_mlo) == (0, ml_w, 2 * ml_w)

    x2 = x.reshape(t, d)
    for i in range(depth):
        w = w_in[i]
        w_main = jnp.concatenate([w[:, o:o + wd] for o, wd in main_segments], axis=1).astype(BF16)
        w_small = _small_gate_block(w[:, o_mli:o_mli + ML_HEADS], w[:, o_mlf:o_mlf + ML_HEADS],
                                    w[:, o_sdt:o_sdt + SSM_HEADS], w[:, o_ff:o_ff + FOX_HEADS]).astype(BF16)
        gate_bias = _small_gate_block(ml_b_i[i], ml_b_f[i], ssm_dt_bias[i],
                                      fox_b_f[i]).astype(F32).reshape(1, LANES)

        zm, zs = _inproj(x2, _row(norm_mix_g[i]), w_main, w_small)

        y_ml = _mlstm(zm, zs, _pad_rows(ml_conv_w[i].astype(F32), 8), _row(ml_conv_b[i]), gate_bias,
                      _row(ml_norm_g[i]), bsz=bsz, seq=seq)

        scw = ssm_conv_w[i].astype(F32)
        scb = ssm_conv_b[i].astype(F32)
        no_heads = jnp.zeros((ML_HEADS,), ssm_a_log.dtype)
        alog_row = _small_gate_block(no_heads, no_heads, ssm_a_log[i],
                                     no_heads).astype(F32).reshape(1, LANES)
        d_row = jnp.repeat(ssm_d[i].astype(F32), SSM_HEADDIM).reshape(1, ssm_w)
        y_ssm = _ssd(zm, zs, _pad_rows(scw[:, :ssm_w], 8), _row(scb[:ssm_w]),
                     _pad_rows(scw[:, ssm_w:], 8), _row(scb[ssm_w:]), gate_bias, alog_row, d_row,
                     _row(ssm_norm_g[i]), bsz=bsz, seq=seq, col_z=c_sz, col_x=c_sx, col_bc=c_bc)

        qn, kn, ft = _fox_prep(zm, zs, gate_bias, _row(fox_q_norm_g[i]), _row(fox_k_norm_g[i]),
                               bsz=bsz, seq=seq, col_q=c_fq, col_k=c_fk)
        y_fox = _fox_attn(qn, kn, zm, ft, bsz=bsz, seq=seq, col_v=c_fv)

        x2 = _merge(x2, y_ml, y_ssm, y_fox, zm, w_branch_ml[i].astype(BF16), w_branch_ssm[i].astype(BF16),
                    w_branch_fox[i].astype(BF16), w_out[i].astype(BF16), col_g=c_gml)

        x2 = _peer(x2, _row(norm_ffn_g[i]), peer_w_q[i].astype(BF16), peer_keys1[i].astype(BF16),
                   peer_keys2[i].astype(BF16), peer_u[i].astype(BF16), peer_v[i])

        x2 = _ple(x2, p[i].reshape(t, -1), _row(norm_ple_g[i]), ple_w_gate[i].astype(BF16),
                  ple_w_proj[i].astype(BF16), _row(final_norm_g), final=(i == depth - 1))
    return x2.reshape(bsz, seq, d)
```

```python
import functools
import math

import jax
import jax.numpy as jnp
from jax import lax
from jax.experimental import pallas as pl
from jax.experimental.pallas import tpu as pltpu

F32 = jnp.float32
BF16 = jnp.bfloat16
EPS = 1e-6
LOG2E = math.log2(math.e)

CONV_WIDTH = 4
ML_HEADS = 8
ML_DQK = 64
ML_DV = 128
SSM_HEADS = 16
SSM_HEADDIM = 64
SSM_GROUPS = 2
SSM_STATE = 128
FOX_HEADS = 8
FOX_HEADDIM = 128
PEER_HEADS = 8
PEER_NKEYS = 128
PEER_QDIM = 256
PEER_TOPK = 16

LANES = 128
SUB = 8
HALO = SUB
MXU_TILE = 256

COL_ML_I = 0
COL_ML_F = COL_ML_I + ML_HEADS
COL_DT = COL_ML_F + ML_HEADS
COL_FOX_F = COL_DT + SSM_HEADS

V7X_VMEM_BYTES = 64 * 1024 * 1024
VMEM_LIMIT = V7X_VMEM_BYTES - 8 * 1024 * 1024


def _cparams(sem):
    return pltpu.CompilerParams(dimension_semantics=sem, vmem_limit_bytes=VMEM_LIMIT)


def _log_sigmoid(v):
    return jnp.minimum(v, 0.0) - jnp.log1p(jnp.exp(-jnp.abs(v)))


def _sigmoid(v):
    return 1.0 / (1.0 + jnp.exp(-v))


def _silu(v):
    return v * _sigmoid(v)


def _softplus(v):
    return jnp.maximum(v, 0.0) + jnp.log1p(jnp.exp(-jnp.abs(v)))


def _tril(n):
    r = lax.broadcasted_iota(jnp.int32, (n, n), 0)
    c = lax.broadcasted_iota(jnp.int32, (n, n), 1)
    return r >= c


def _cumsum_rows(v):
    n = v.shape[0]
    return jnp.dot(_tril(n).astype(F32), v, precision=lax.Precision.HIGHEST,
                   preferred_element_type=F32)


def _inproj_kernel(x_ref, g_ref, wm_ref, ws_ref, zm_ref, zs_ref, h_scr):
    @pl.when(pl.program_id(1) == 0)
    def _():
        x = x_ref[...]
        ms = jnp.mean(x * x, axis=-1, keepdims=True)
        h = (x * lax.rsqrt(ms + EPS) * g_ref[...]).astype(BF16)
        h_scr[...] = h
        zs_ref[...] = jnp.dot(h, ws_ref[...], preferred_element_type=F32)

    zm_ref[...] = jnp.dot(h_scr[...], wm_ref[...], preferred_element_type=F32).astype(zm_ref.dtype)


def _inproj(x2, g, w_main, w_small, tm=2048, tn=512):
    t, d = x2.shape
    tm = min(tm, t)
    n = w_main.shape[1]
    return pl.pallas_call(
        _inproj_kernel,
        grid=(t // tm, n // tn),
        in_specs=[
            pl.BlockSpec((tm, d), lambda i, j: (i, 0)),
            pl.BlockSpec((1, d), lambda i, j: (0, 0)),
            pl.BlockSpec((d, tn), lambda i, j: (0, j)),
            pl.BlockSpec((d, LANES), lambda i, j: (0, 0)),
        ],
        out_specs=[
            pl.BlockSpec((tm, tn), lambda i, j: (i, j)),
            pl.BlockSpec((tm, LANES), lambda i, j: (i, 0)),
        ],
        out_shape=[jax.ShapeDtypeStruct((t, n), BF16), jax.ShapeDtypeStruct((t, LANES), F32)],
        scratch_shapes=[pltpu.VMEM((tm, d), BF16)],
        compiler_params=_cparams(("parallel", "arbitrary")),
        name="inproj",
    )(x2, g, w_main, w_small)


def _conv_chunk(buf_ref, w_ref, b_ref, start, length):
    acc = None
    for j in range(CONV_WIDTH):
        off = HALO - (CONV_WIDTH - 1) + j + start
        term = buf_ref[off:off + length, :] * w_ref[j:j + 1, :]
        acc = term if acc is None else acc + term
    return acc + b_ref[...]


def _mlstm_kernel(qk_ref, v_ref, o_ref, gs_ref, cw_ref, cb_ref, gb_ref, ng_ref, y_ref,
                  buf, c_scr, m_scr, *, ts, chunk):
    nh, dk, dv = ML_HEADS, ML_DQK, ML_DV
    qkw = nh * dk
    bsz = qk_ref.shape[0]

    @pl.when(pl.program_id(0) == 0)
    def _():
        buf[:, 0:HALO, :] = jnp.zeros((bsz, HALO, buf.shape[2]), F32)
        c_scr[...] = jnp.zeros(c_scr.shape, F32)
        m_scr[...] = jnp.zeros(m_scr.shape, F32)

    buf[:, HALO:HALO + ts, :] = qk_ref[...].astype(F32)
    causal = _tril(chunk)
    lane = lax.broadcasted_iota(jnp.int32, (chunk, LANES), 1)
    ones_col = (lane == 0).astype(BF16)

    for c in range(ts // chunk):
        r0 = c * chunk
        for b in range(bsz):
            act = _silu(_conv_chunk(buf.at[b], cw_ref, cb_ref, r0, chunk))
            q_all = (act[:, :qkw] * (dk ** -0.5)).astype(BF16)
            kt_all = act[:, qkw:].T
            gates = gs_ref[b, r0:r0 + chunk, :] + gb_ref[...]
            bcum = _cumsum_rows(_log_sigmoid(gates))
            gates_t = gates.T
            bcum_t = bcum.T
            for h in range(nh):
                i_col = gates[:, COL_ML_I + h:COL_ML_I + h + 1]
                i_row = gates_t[COL_ML_I + h:COL_ML_I + h + 1, :]
                b_col = bcum[:, COL_ML_F + h:COL_ML_F + h + 1]
                b_row = bcum_t[COL_ML_F + h:COL_ML_F + h + 1, :]
                g_tot = b_col[chunk - 1:chunk, :]
                m_prev = m_scr[b, h:h + 1, 0:1]
                dmat = jnp.where(causal, b_col - b_row + i_row, -jnp.inf)
                inter = b_col + m_prev
                m_t = jnp.maximum(inter, jnp.max(dmat, axis=-1, keepdims=True))
                qh = q_all[:, h * dk:(h + 1) * dk]
                kt = kt_all[h * dk:(h + 1) * dk, :]
                s = jnp.dot(qh, kt.astype(BF16), preferred_element_type=F32)
                sc = (s * jnp.exp(dmat - m_t)).astype(BF16)
                w_inter = jnp.exp(inter - m_t)
                vaug = jnp.concatenate([v_ref[b, r0:r0 + chunk, h * dv:(h + 1) * dv], ones_col], axis=1)
                cmem = c_scr[b, h]
                tot = (jnp.dot(sc, vaug, preferred_element_type=F32)
                       + w_inter * jnp.dot(qh, cmem.astype(BF16), preferred_element_type=F32))
                num = tot[:, :dv]
                den = tot[:, dv:dv + 1]
                hv = num / jnp.maximum(jnp.abs(den), jnp.exp(-m_t))
                hn = hv * lax.rsqrt(jnp.mean(hv * hv, axis=-1, keepdims=True) + EPS)
                hn = hn * ng_ref[:, h * dv:(h + 1) * dv]
                og = _sigmoid(o_ref[b, r0:r0 + chunk, h * dv:(h + 1) * dv].astype(F32))
                y_ref[b, r0:r0 + chunk, h * dv:(h + 1) * dv] = (og * hn).astype(y_ref.dtype)
                a_row = g_tot - b_row + i_row
                m_new = jnp.maximum(g_tot + m_prev, jnp.max(a_row, axis=-1, keepdims=True))
                wa_row = jnp.exp(a_row - m_new)
                decay = jnp.exp(g_tot + m_prev - m_new)
                kw = (kt * wa_row).astype(BF16)
                c_scr[b, h] = decay * cmem + jnp.dot(kw, vaug, preferred_element_type=F32)
                m_scr[b, h:h + 1, :] = jnp.broadcast_to(m_new, (1, LANES))

    buf[:, 0:HALO, :] = buf[:, ts:ts + HALO, :]


def _mlstm(zm, zs, conv_w, conv_b, gate_bias, norm_g, *, bsz, seq, ts=256, chunk=256):
    t, n = zm.shape
    w = ML_HEADS * ML_DV
    zm3 = zm.reshape(bsz, seq, n)
    zs3 = zs.reshape(bsz, seq, LANES)
    full = lambda a: pl.BlockSpec(a.shape, lambda s: (0, 0))
    y = pl.pallas_call(
        functools.partial(_mlstm_kernel, ts=ts, chunk=chunk),
        grid=(seq // ts,),
        in_specs=[
            pl.BlockSpec((bsz, ts, w), lambda s: (0, s, 0)),
            pl.BlockSpec((bsz, ts, w), lambda s: (0, s, 1)),
            pl.BlockSpec((bsz, ts, w), lambda s: (0, s, 2)),
            pl.BlockSpec((bsz, ts, LANES), lambda s: (0, s, 0)),
            full(conv_w), full(conv_b), full(gate_bias), full(norm_g),
        ],
        out_specs=pl.BlockSpec((bsz, ts, w), lambda s: (0, s, 0)),
        out_shape=jax.ShapeDtypeStruct((bsz, seq, w), BF16),
        scratch_shapes=[
            pltpu.VMEM((bsz, HALO + ts, w), F32),
            pltpu.VMEM((bsz, ML_HEADS, ML_DQK, 2 * ML_DV), F32),
            pltpu.VMEM((bsz, ML_HEADS, LANES), F32),
        ],
        compiler_params=_cparams(("arbitrary",)),
        name="mlstm",
    )(zm3, zm3, zm3, zs3, conv_w, conv_b, gate_bias, norm_g)
    return y.reshape(t, w)


def _ssd_kernel(z_ref, x_ref, bc_ref, gs_ref, cwx_ref, cbx_ref, cwbc_ref, cbbc_ref, gb_ref,
                alog_ref, d_ref, ng_ref, y_ref, bufx, bufbc, st_scr, *, ts, chunk):
    ng, nst, p = SSM_GROUPS, SSM_STATE, SSM_HEADDIM
    hg = SSM_HEADS // ng
    gw = hg * p
    pairs_per_group = gw // LANES

    @pl.when(pl.program_id(1) == 0)
    def _():
        bufx[0:HALO, :] = jnp.zeros((HALO, bufx.shape[1]), F32)
        bufbc[0:HALO, :] = jnp.zeros((HALO, bufbc.shape[1]), F32)
        st_scr[...] = jnp.zeros(st_scr.shape, F32)

    bufx[HALO:HALO + ts, :] = x_ref[...].astype(F32)
    bufbc[HALO:HALO + ts, :] = bc_ref[...].astype(F32)
    causal = _tril(chunk)
    low_half = lax.broadcasted_iota(jnp.int32, (chunk, LANES), 1) < p
    a_row_all = -jnp.exp(alog_ref[...])

    for c in range(ts // chunk):
        r0 = c * chunk
        xa = _silu(_conv_chunk(bufx, cwx_ref, cbx_ref, r0, chunk))
        bca = _silu(_conv_chunk(bufbc, cwbc_ref, cbbc_ref, r0, chunk))
        dt = _softplus(gs_ref[r0:r0 + chunk, :] + gb_ref[...])
        acum = _cumsum_rows(dt * a_row_all)
        acum_t = acum.T
        y_parts = []
        for g in range(ng):
            bg = bca[:, g * nst:(g + 1) * nst]
            cg = bca[:, ng * nst + g * nst:ng * nst + (g + 1) * nst].astype(BF16)
            bg_t = bg.T
            cbt = jnp.dot(cg, bg_t.astype(BF16), preferred_element_type=F32)
            state = st_scr[g]
            inter = jnp.dot(cg, state.astype(BF16), preferred_element_type=F32)
            xs_scaled, last_parts = [], []
            for pp in range(pairs_per_group):
                pidx = g * pairs_per_group + pp
                ha = 2 * pidx
                ca, cb = COL_DT + ha, COL_DT + ha + 1
                sl = slice(pidx * LANES, (pidx + 1) * LANES)
                xa_p = xa[:, sl]
                dt_pair = jnp.where(low_half, dt[:, ca:ca + 1], dt[:, cb:cb + 1])
                ac_pair = jnp.where(low_half, acum[:, ca:ca + 1], acum[:, cb:cb + 1])
                xs_p = xa_p * dt_pair
                xs_b = xs_p.astype(BF16)
                lm_a = jnp.exp(jnp.where(causal, acum[:, ca:ca + 1] - acum_t[ca:ca + 1, :], -jnp.inf))
                lm_b = jnp.exp(jnp.where(causal, acum[:, cb:cb + 1] - acum_t[cb:cb + 1, :], -jnp.inf))
                ya = jnp.dot((cbt * lm_a).astype(BF16), xs_b, preferred_element_type=F32)
                yb = jnp.dot((cbt * lm_b).astype(BF16), xs_b, preferred_element_type=F32)
                y_p = (jnp.where(low_half, ya, yb)
                       + inter[:, pp * LANES:(pp + 1) * LANES] * jnp.exp(ac_pair)
                       + d_ref[:, sl] * xa_p)
                y_parts.append(y_p)
                last = ac_pair[chunk - 1:chunk, :]
                xs_scaled.append((xs_p * jnp.exp(last - ac_pair)).astype(BF16))
                last_parts.append(last)
            xs_g = jnp.concatenate(xs_scaled, axis=1)
            last_g = jnp.concatenate(last_parts, axis=1)
            st_scr[g] = jnp.exp(last_g) * state + jnp.dot(bg_t.astype(BF16), xs_g,
                                                          preferred_element_type=F32)
        ys = jnp.concatenate(y_parts, axis=1)
        ys = ys * _silu(z_ref[r0:r0 + chunk, :].astype(F32))
        outs = []
        for g in range(ng):
            yg = ys[:, g * gw:(g + 1) * gw]
            yn = yg * lax.rsqrt(jnp.mean(yg * yg, axis=-1, keepdims=True) + EPS)
            outs.append(yn * ng_ref[:, g * gw:(g + 1) * gw])
        y_ref[r0:r0 + chunk, :] = jnp.concatenate(outs, axis=1).astype(y_ref.dtype)

    bufx[0:HALO, :] = bufx[ts:ts + HALO, :]
    bufbc[0:HALO, :] = bufbc[ts:ts + HALO, :]


def _ssd(zm, zs, cwx, cbx, cwbc, cbbc, gate_bias, alog_row, d_row, norm_g, *, bsz, seq,
         col_z, col_x, col_bc, ts=256, chunk=128):
    t = zm.shape[0]
    w = SSM_HEADS * SSM_HEADDIM
    bcw = 2 * SSM_GROUPS * SSM_STATE
    nblk = seq // ts
    row = lambda b, s: b * nblk + s
    full = lambda a: pl.BlockSpec(a.shape, lambda b, s: (0, 0))
    return pl.pallas_call(
        functools.partial(_ssd_kernel, ts=ts, chunk=chunk),
        grid=(bsz, nblk),
        in_specs=[
            pl.BlockSpec((ts, w), lambda b, s: (row(b, s), col_z // w)),
            pl.BlockSpec((ts, w), lambda b, s: (row(b, s), col_x // w)),
            pl.BlockSpec((ts, bcw), lambda b, s: (row(b, s), col_bc // bcw)),
            pl.BlockSpec((ts, LANES), lambda b, s: (row(b, s), 0)),
            full(cwx), full(cbx), full(cwbc), full(cbbc), full(gate_bias),
            full(alog_row), full(d_row), full(norm_g),
        ],
        out_specs=pl.BlockSpec((ts, w), lambda b, s: (row(b, s), 0)),
        out_shape=jax.ShapeDtypeStruct((t, w), BF16),
        scratch_shapes=[
            pltpu.VMEM((HALO + ts, w), F32),
            pltpu.VMEM((HALO + ts, bcw), F32),
            pltpu.VMEM((SSM_GROUPS, SSM_STATE, w // SSM_GROUPS), F32),
        ],
        compiler_params=_cparams(("parallel", "arbitrary")),
        name="ssd",
    )(zm, zm, zm, zs, cwx, cbx, cwbc, cbbc, gate_bias, alog_row, d_row, norm_g)


def _fox_prep_kernel(q_ref, k_ref, gs_ref, gb_ref, gq_ref, gk_ref, qn_ref, kn_ref, ft_ref,
                     carry, *, ts):
    d = FOX_HEADDIM

    @pl.when(pl.program_id(1) == 0)
    def _():
        carry[...] = jnp.zeros(carry.shape, F32)

    for h in range(FOX_HEADS):
        sl = slice(h * d, (h + 1) * d)
        qh = q_ref[:, sl].astype(F32)
        qn = qh * lax.rsqrt(jnp.mean(qh * qh, axis=-1, keepdims=True) + EPS) * gq_ref[...]
        qn_ref[:, sl] = qn.astype(qn_ref.dtype)
        kh = k_ref[:, sl].astype(F32)
        kn = kh * lax.rsqrt(jnp.mean(kh * kh, axis=-1, keepdims=True) + EPS) * gk_ref[...]
        kn_ref[:, sl] = (kn * (d ** -0.5 * LOG2E)).astype(kn_ref.dtype)

    lf = _log_sigmoid(gs_ref[...] + gb_ref[...])
    fc = _cumsum_rows(lf) + carry[0:1, :]
    carry[0:1, :] = fc[ts - 1:ts, :]
    ft_ref[...] = fc.T[COL_FOX_F:COL_FOX_F + FOX_HEADS, :] * LOG2E


def _fox_prep(zm, zs, gate_bias, gq, gk, *, bsz, seq, col_q, col_k, ts=256):
    t = zm.shape[0]
    w = FOX_HEADS * FOX_HEADDIM
    nblk = seq // ts
    row = lambda b, s: b * nblk + s
    full = lambda a: pl.BlockSpec(a.shape, lambda b, s: (0, 0))
    return pl.pallas_call(
        functools.partial(_fox_prep_kernel, ts=ts),
        grid=(bsz, nblk),
        in_specs=[
            pl.BlockSpec((ts, w), lambda b, s: (row(b, s), col_q // w)),
            pl.BlockSpec((ts, w), lambda b, s: (row(b, s), col_k // w)),
            pl.BlockSpec((ts, LANES), lambda b, s: (row(b, s), 0)),
            full(gate_bias), full(gq), full(gk),
        ],
        out_specs=[
            pl.BlockSpec((ts, w), lambda b, s: (row(b, s), 0)),
            pl.BlockSpec((ts, w), lambda b, s: (row(b, s), 0)),
            pl.BlockSpec((None, FOX_HEADS, ts), lambda b, s: (b, 0, s)),
        ],
        out_shape=[jax.ShapeDtypeStruct((t, w), BF16), jax.ShapeDtypeStruct((t, w), BF16),
                   jax.ShapeDtypeStruct((bsz, FOX_HEADS, seq), F32)],
        scratch_shapes=[pltpu.VMEM((8, LANES), F32)],
        compiler_params=_cparams(("parallel", "arbitrary")),
        name="fox_prep",
    )(zm, zm, zs, gate_bias, gq, gk)


def _fox_attn_kernel(qi_ref, kj_ref, q_ref, k_ref, v_ref, f_ref, o_ref, m_scr, acc_scr, vaug_scr,
                     *, tq, tk, rb, rb_diag):
    pr = pl.program_id(2)
    qi = qi_ref[pr]
    kj = kj_ref[pr]
    h = pl.program_id(1)
    d = FOX_HEADDIM
    ratio = tq // tk
    per_tile = tk // rb

    @pl.when(kj == 0)
    def _():
        m_scr[...] = jnp.full(m_scr.shape, -jnp.inf, F32)
        acc_scr[...] = jnp.zeros(acc_scr.shape, F32)

    vaug_scr[:, :d] = v_ref[...]
    vaug_scr[:, d:] = (lax.broadcasted_iota(jnp.int32, (tk, d), 1) == 0).astype(BF16)
    fk_all = f_ref[pl.ds(h, 1), :]

    def row_block(row0, nrows, diag_row0):
        rows = slice(row0, row0 + nrows)
        ncol = tk if diag_row0 is None else diag_row0 + nrows
        s = lax.dot_general(q_ref[rows, :], k_ref[:ncol, :], (((1,), (1,)), ((), ())),
                            preferred_element_type=F32) - fk_all[:, :ncol]
        if diag_row0 is not None:
            tail = jnp.where(_tril(nrows), s[:, ncol - nrows:], -jnp.inf)
            s = tail if ncol == nrows else jnp.concatenate([s[:, :ncol - nrows], tail], axis=1)
        m_old = m_scr[rows, :]
        m_new = jnp.maximum(m_old, jnp.max(s, axis=-1, keepdims=True))
        alpha = jnp.exp2(m_old - m_new)
        p = jnp.exp2(s - m_new[:, :1]).astype(BF16)
        pv = jnp.dot(p, vaug_scr[:ncol, :], preferred_element_type=F32)
        acc_scr[rows, :] = jnp.concatenate([alpha, alpha], axis=1) * acc_scr[rows, :] + pv
        m_scr[rows, :] = m_new

    @pl.when(kj < qi * ratio)
    def _():
        for r in range(tq // rb):
            row_block(r * rb, rb, None)

    for c in range(ratio):
        @pl.when(kj == qi * ratio + c)
        def _(c=c):
            for r in reversed(range(tk // rb_diag)):
                row_block(c * tk + r * rb_diag, rb_diag, r * rb_diag)
            for r in range((c + 1) * per_tile, tq // rb):
                row_block(r * rb, rb, None)

    @pl.when(kj == qi * ratio + ratio - 1)
    def _():
        acc = acc_scr[...]
        o_ref[...] = (acc[:, :d] / acc[:, d:d + 1]).astype(o_ref.dtype)


def _fox_attn(qn, kn, zm, ft, *, bsz, seq, col_v, tk=2048, ratio=2, rb=128, rb_diag=512):
    t = qn.shape[0]
    d = FOX_HEADDIM
    tk = min(tk, seq // ratio)
    tq = ratio * tk
    rb = min(rb, tk)
    rb_diag = min(rb_diag, tk)
    nq = seq // tq
    nkt = seq // tk
    qi_tab, kj_tab = [], []
    for i in range(nq):
        for j in range(ratio * (i + 1)):
            qi_tab.append(i)
            kj_tab.append(j)
    qi_tab = jnp.asarray(qi_tab, jnp.int32)
    kj_tab = jnp.asarray(kj_tab, jnp.int32)
    vblk = col_v // d
    grid_spec = pltpu.PrefetchScalarGridSpec(
        num_scalar_prefetch=2,
        grid=(bsz, FOX_HEADS, qi_tab.shape[0]),
        in_specs=[
            pl.BlockSpec((tq, d), lambda b, h, p, qi, kj: (b * nq + qi[p], h)),
            pl.BlockSpec((tk, d), lambda b, h, p, qi, kj: (b * nkt + kj[p], h)),
            pl.BlockSpec((tk, d), lambda b, h, p, qi, kj: (b * nkt + kj[p], vblk + h)),
            pl.BlockSpec((None, FOX_HEADS, tk), lambda b, h, p, qi, kj: (b, 0, kj[p])),
        ],
        out_specs=pl.BlockSpec((tq, d), lambda b, h, p, qi, kj: (b * nq + qi[p], h)),
        scratch_shapes=[pltpu.VMEM((tq, LANES), F32), pltpu.VMEM((tq, 2 * d), F32),
                        pltpu.VMEM((tk, 2 * d), BF16)],
    )
    return pl.pallas_call(
        functools.partial(_fox_attn_kernel, tq=tq, tk=tk, rb=rb, rb_diag=rb_diag),
        grid_spec=grid_spec,
        out_shape=jax.ShapeDtypeStruct((t, FOX_HEADS * d), BF16),
        compiler_params=_cparams(("parallel", "parallel", "arbitrary")),
        name="fox_attn",
    )(qi_tab, kj_tab, qn, kn, zm, ft)


def _merge_kernel(x_ref, yml_ref, yssm_ref, yfox_ref, gml_ref, gssm_ref, gfox_ref,
                  pml_ref, pssm_ref, pfox_ref, wout_ref, o_ref):
    def branch(y_ref, g_ref, p_ref):
        proj = jnp.dot(y_ref[...], p_ref[...], preferred_element_type=F32)
        return _sigmoid(g_ref[...].astype(F32)) * proj

    merged = (branch(yml_ref, gml_ref, pml_ref) + branch(yssm_ref, gssm_ref, pssm_ref)
              + branch(yfox_ref, gfox_ref, pfox_ref))
    o_ref[...] = x_ref[...] + jnp.dot(merged.astype(BF16), wout_ref[...], preferred_element_type=F32)


def _merge(x2, y_ml, y_ssm, y_fox, zm, p_ml, p_ssm, p_fox, w_out, *, col_g, tm=512):
    t, d = x2.shape
    rowblk = lambda c: pl.BlockSpec((tm, d), lambda i: (i, c))
    wfull = lambda a: pl.BlockSpec(a.shape, lambda i: (0, 0))
    gblk = col_g // d
    return pl.pallas_call(
        _merge_kernel,
        grid=(t // tm,),
        in_specs=[rowblk(0), rowblk(0), rowblk(0), rowblk(0),
                  rowblk(gblk), rowblk(gblk + 1), rowblk(gblk + 2),
                  wfull(p_ml), wfull(p_ssm), wfull(p_fox), wfull(w_out)],
        out_specs=rowblk(0),
        out_shape=jax.ShapeDtypeStruct((t, d), F32),
        compiler_params=_cparams(("parallel",)),
        name="merge",
    )(x2, y_ml, y_ssm, y_fox, zm, zm, zm, p_ml, p_ssm, p_fox, w_out)


def _erf(v):
    return lax.erf(v)


def _gelu_x2(v):
    return v * (1.0 + _erf(v * (2.0 ** -0.5)))


def _sorted_top(tiles):
    n = len(tiles)
    a = list(tiles)

    def keep_larger_first(i, j):
        a[i], a[j] = jnp.maximum(a[i], a[j]), jnp.minimum(a[i], a[j])

    def merge(lo_stride):
        j = lo_stride
        while j >= 1:
            for i in range(n):
                if i ^ j > i:
                    keep_larger_first(i, i ^ j)
            j //= 2

    k = 2
    while k <= n:
        j = k // 2
        while j >= 1:
            for i in range(n):
                l = i ^ j
                if l > i:
                    if i & k == 0:
                        keep_larger_first(i, l)
                    else:
                        keep_larger_first(l, i)
            j //= 2
        k *= 2
    shift = SUB // 2
    while shift >= 1:
        b = [pltpu.roll(x, shift, 0) for x in a]
        a = [jnp.maximum(a[j], b[n - 1 - j]) for j in range(n)]
        merge(n // 2)
        shift //= 2
    return a


def _as_rows(vals):
    sub = lax.broadcasted_iota(jnp.int32, (SUB, LANES), 0)
    tiles = []
    for g in range(len(vals) // SUB):
        t = vals[SUB * g]
        for r in range(1, SUB):
            t = jnp.where(sub == r, vals[SUB * g + r], t)
        tiles.append(t)
    return tiles


def _candidate_tiles(v1, v2, v1_rows, v2_rows, k):
    sub = lax.broadcasted_iota(jnp.int32, (SUB, LANES), 0)
    tiles = [v1[0] + v2_rows[g] for g in range(k // SUB)]
    a = 1
    while k // (a + 1) > 1:
        nb = k // (a + 1)
        assert nb <= SUB
        t = v1[a] + v2_rows[0]
        tiles.append(t if nb == SUB else jnp.where(sub < nb, t, -jnp.inf))
        a += 1
    assert a % SUB == 0
    tiles.extend(v1_rows[g] + v2[0] for g in range(a // SUB, k // SUB))
    assert len(tiles) <= 16
    tiles.extend([jnp.full((SUB, LANES), -jnp.inf, F32)] * (16 - len(tiles)))
    return tiles


def _peer_kernel(x_ref, g_ref, wq_ref, k1_ref, k2_ref, u_ref, vt_ref, o_ref,
                 ht_scr, q_scr, theta_scr, e1_scr, s2_scr, e2_scr, acc_scr, g0_scr, g1_scr,
                 w0_scr, w1_scr, *, tb, eb, n_e):
    j = pl.program_id(1)
    nk, topk, nhead = PEER_NKEYS, PEER_TOPK, PEER_HEADS
    assert nk // SUB == topk
    half = PEER_QDIM // 2
    n_lc = tb // LANES
    pk = SUB

    @pl.when(j == 0)
    def _():
        x = x_ref[...]
        hn32 = x * lax.rsqrt(jnp.mean(x * x, axis=-1, keepdims=True) + EPS) * g_ref[...]
        ht_scr[...] = hn32.T.astype(BF16)
        q = jnp.dot(hn32.astype(BF16), wq_ref[...], preferred_element_type=F32).astype(BF16)
        for c in range(q.shape[1] // half):
            q_scr[c] = q[:, c * half:(c + 1) * half]
        acc_scr[...] = jnp.zeros(acc_scr.shape, F32)

        def head_body(h, carry):
            nt = (((1,), (1,)), ((), ()))
            s1 = lax.dot_general(k1_ref[h], q_scr[2 * h], nt, preferred_element_type=F32)
            s2 = lax.dot_general(k2_ref[h], q_scr[2 * h + 1], nt, preferred_element_type=F32)
            for lc in range(n_lc):
                ls = slice(lc * LANES, (lc + 1) * LANES)
                s1c = s1[:, ls].reshape(nk // SUB, SUB, LANES)
                s2c = s2[:, ls].reshape(nk // SUB, SUB, LANES)
                v1 = _sorted_top([s1c[i] for i in range(nk // SUB)])
                v2 = _sorted_top([s2c[i] for i in range(nk // SUB)])
                v1_rows, v2_rows = _as_rows(v1), _as_rows(v2)
                tv = _sorted_top(_candidate_tiles(v1, v2, v1_rows, v2_rows, topk))
                tau, mx = tv[topk - 1], tv[0]
                zsum = jnp.zeros_like(mx)
                for a in range(topk):
                    zsum = zsum + jnp.exp(tv[a] - mx)
                th_rows = [jnp.full((SUB, LANES), jnp.inf, F32) for _ in v1_rows]
                for b in range(topk):
                    th_rows = [jnp.where(v1_rows[g] + v2[b] >= tau, v2[b], th_rows[g])
                               for g in range(len(v1_rows))]
                theta = jnp.full(s1c.shape, jnp.inf, F32)
                for a in range(topk):
                    th_a = jnp.broadcast_to(th_rows[a // SUB][a % SUB:a % SUB + 1, :], (SUB, LANES))
                    theta = jnp.where(s1c == v1[a][None], th_a[None], theta)
                theta_scr[h, lc] = theta.reshape(nk, LANES)
                e1_scr[h, lc] = (jnp.exp(s1c - v1[0][None]) * (0.5 / zsum)[None]).reshape(nk, LANES)
                s2_scr[h, lc] = s2c
                e2_scr[h, lc] = jnp.exp(s2c - v2[0][None])
            return carry

        lax.fori_loop(0, nhead, head_body, 0)

    n_r = eb // nk

    def stage_a(w_wr, lc, r_range):
        for r in r_range:
            i1 = j * n_r + r
            wsum = None
            for h in range(nhead):
                th = jnp.broadcast_to(theta_scr[h, lc, pl.ds(i1, 1), :], (pk, LANES))
                e1 = jnp.broadcast_to(e1_scr[h, lc, pl.ds(i1, 1), :], (pk, LANES))
                term = jnp.where(s2_scr[h, lc] >= th[None], e2_scr[h, lc] * e1[None], 0.0)
                wsum = term if wsum is None else wsum + term
            w_wr[lc, r * (nk // pk):(r + 1) * (nk // pk)] = wsum

    cpt = MXU_TILE // LANES
    e_piece = eb // cpt
    d_piece = acc_scr.shape[0] // cpt

    def stage_b(w_rd, g_wr, n, m):
        ts_ = slice(n * MXU_TILE, (n + 1) * MXU_TILE)
        es = slice(m * e_piece, (m + 1) * e_piece)
        ps = slice(m * (e_piece // pk), (m + 1) * (e_piece // pk))
        act = _gelu_x2(jnp.dot(u_ref[es, :], ht_scr[:, ts_], preferred_element_type=F32))
        w_prev = jnp.concatenate([w_rd[cpt * n + c, ps].reshape(e_piece, LANES) for c in range(cpt)], axis=1)
        g_wr[es, ts_] = (act * w_prev).astype(BF16)

    def stage_c(g_rd, n, m):
        ts_ = slice(n * MXU_TILE, (n + 1) * MXU_TILE)
        ds_ = slice(m * d_piece, (m + 1) * d_piece)
        acc_scr[ds_, ts_] += jnp.dot(vt_ref[ds_, :], g_rd[:, ts_], preferred_element_type=F32)

    def step(parity, do_a=True, do_b=True, do_c=True):
        w_wr, w_rd, g_wr, g_rd = ((w0_scr, w1_scr, g1_scr, g0_scr) if parity == 0
                                  else (w1_scr, w0_scr, g0_scr, g1_scr))
        for n in range(n_lc // cpt):
            for m in range(cpt):
                if do_b:
                    stage_b(w_rd, g_wr, n, m)
                if do_a:
                    stage_a(w_wr, cpt * n + m, range(0, n_r // 2))
                if do_c:
                    stage_c(g_rd, n, m)
                if do_a:
                    stage_a(w_wr, cpt * n + m, range(n_r // 2, n_r))

    @pl.when(j == 0)
    def _():
        step(0, do_b=False, do_c=False)

    @pl.when(j == 1)
    def _():
        step(1, do_c=False)

    steady = jnp.logical_and(j >= 2, j < n_e)

    @pl.when(jnp.logical_and(steady, j % 2 == 0))
    def _():
        step(0)

    @pl.when(jnp.logical_and(steady, j % 2 == 1))
    def _():
        step(1)

    @pl.when(j == n_e)
    def _():
        step(n_e % 2, do_a=False)

    @pl.when(j == n_e + 1)
    def _():
        step((n_e + 1) % 2, do_a=False, do_b=False)

    @pl.when(j == n_e + 1)
    def _():
        o_ref[...] = x_ref[...] + acc_scr[...].T


def _peer(x2, g, w_q, keys1, keys2, u, v, *, tb=512, eb=512):
    t, d = x2.shape
    tb = min(tb, t)
    ne = u.shape[0]
    nq = w_q.shape[1]
    n_e = ne // eb
    n_lc = tb // LANES
    pk = SUB
    v_t = v.reshape(n_e, eb, d).transpose(0, 2, 1).astype(BF16)
    once = pl.Buffered(1)
    return pl.pallas_call(
        functools.partial(_peer_kernel, tb=tb, eb=eb, n_e=n_e),
        grid=(t // tb, n_e + 2),
        in_specs=[
            pl.BlockSpec((tb, d), lambda i, j: (i, 0), pipeline_mode=once),
            pl.BlockSpec((1, d), lambda i, j: (0, 0)),
            pl.BlockSpec((d, nq), lambda i, j: (0, 0), pipeline_mode=once),
            pl.BlockSpec(keys1.shape, lambda i, j: (0, 0, 0)),
            pl.BlockSpec(keys2.shape, lambda i, j: (0, 0, 0)),
            pl.BlockSpec((eb, d), lambda i, j: (jnp.clip(j - 1, 0, n_e - 1), 0)),
            pl.BlockSpec((None, d, eb), lambda i, j: (jnp.clip(j - 2, 0, n_e - 1), 0, 0)),
        ],
        out_specs=pl.BlockSpec((tb, d), lambda i, j: (i, 0)),
        out_shape=jax.ShapeDtypeStruct((t, d), F32),
        scratch_shapes=[
            pltpu.VMEM((d, tb), BF16),
            pltpu.VMEM((2 * PEER_HEADS, tb, PEER_QDIM // 2), BF16),
            pltpu.VMEM((PEER_HEADS, n_lc, PEER_NKEYS, LANES), F32),
            pltpu.VMEM((PEER_HEADS, n_lc, PEER_NKEYS, LANES), F32),
            pltpu.VMEM((PEER_HEADS, n_lc, PEER_NKEYS // pk, pk, LANES), F32),
            pltpu.VMEM((PEER_HEADS, n_lc, PEER_NKEYS // pk, pk, LANES), F32),
            pltpu.VMEM((d, tb), F32),
            pltpu.VMEM((eb, tb), BF16),
            pltpu.VMEM((eb, tb), BF16),
            pltpu.VMEM((n_lc, eb // pk, pk, LANES), F32),
            pltpu.VMEM((n_lc, eb // pk, pk, LANES), F32),
        ],
        compiler_params=_cparams(("parallel", "arbitrary")),
        name="peer",
    )(x2, g, w_q, keys1, keys2, u, v_t)


def _ple_kernel(x_ref, p_ref, g_ref, wg_ref, wp_ref, fg_ref, o_ref, *, final):
    x = x_ref[...]
    hn = (x * lax.rsqrt(jnp.mean(x * x, axis=-1, keepdims=True) + EPS) * g_ref[...]).astype(BF16)
    gate = _sigmoid(jnp.dot(hn, wg_ref[...], preferred_element_type=F32))
    proj = jnp.dot(p_ref[...].astype(BF16), wp_ref[...], preferred_element_type=F32)
    y = x + gate * proj
    if final:
        y = y * lax.rsqrt(jnp.mean(y * y, axis=-1, keepdims=True) + EPS) * fg_ref[...]
    o_ref[...] = y


def _ple(x2, p2, g, w_gate, w_proj, final_g, *, final, tm=512):
    t, d = x2.shape
    pd = p2.shape[1]
    full = lambda a: pl.BlockSpec(a.shape, lambda i: (0, 0))
    return pl.pallas_call(
        functools.partial(_ple_kernel, final=final),
        grid=(t // tm,),
        in_specs=[pl.BlockSpec((tm, d), lambda i: (i, 0)), pl.BlockSpec((tm, pd), lambda i: (i, 0)),
                  full(g), full(w_gate), full(w_proj), full(final_g)],
        out_specs=pl.BlockSpec((tm, d), lambda i: (i, 0)),
        out_shape=jax.ShapeDtypeStruct((t, d), F32),
        compiler_params=_cparams(("parallel",)),
        name="ple",
    )(x2, p2, g, w_gate, w_proj, final_g)


def _row(v):
    return v.reshape(1, -1).astype(F32)


def _small_gate_block(ml_i, ml_f, dt, fox_f):
    parts = [ml_i, ml_f, dt, fox_f]
    used = sum(p_.shape[-1] for p_ in parts)
    pad = jnp.zeros(ml_i.shape[:-1] + (LANES - used,), ml_i.dtype)
    return jnp.concatenate(parts + [pad], axis=-1)


def _pad_rows(w, rows):
    return jnp.concatenate([w, jnp.zeros((rows - w.shape[0],) + w.shape[1:], w.dtype)], axis=0)


def kernel(x, p, norm_mix_g, w_in, ml_conv_w, ml_conv_b, ml_b_i, ml_b_f, ml_norm_g, ssm_conv_w, ssm_conv_b, ssm_dt_bias, ssm_a_log, ssm_d, ssm_norm_g, fox_q_norm_g, fox_k_norm_g, fox_b_f, w_branch_ml, w_branch_ssm, w_branch_fox, w_out, norm_ffn_g, peer_w_q, peer_keys1, peer_keys2, peer_u, peer_v, norm_ple_g, ple_w_gate, ple_w_proj, final_norm_g):
    bsz, seq, d = x.shape
    depth = w_in.shape[0]
    t = bsz * seq
    ml_qk = ML_HEADS * ML_DQK
    ml_w = ML_HEADS * ML_DV
    ssm_w = SSM_HEADS * SSM_HEADDIM
    ssm_bc = SSM_GROUPS * SSM_STATE
    fox_w = FOX_HEADS * FOX_HEADDIM
    splits = (ml_qk, ml_qk, ml_w, ml_w, ML_HEADS, ML_HEADS, ssm_w, ssm_w, ssm_bc, ssm_bc, SSM_HEADS,
              fox_w, fox_w, fox_w, FOX_HEADS, d, d, d)
    offs = [0]
    for s_ in splits:
        offs.append(offs[-1] + s_)
    (o_mlq, o_mlk, o_mlv, o_mlo, o_mli, o_mlf, o_sz, o_sx, o_sb, o_sc, o_sdt,
     o_fq, o_fk, o_fv, o_ff, o_gml, o_gssm, o_gfox) = offs[:-1]

    main_segments = [(o_mlq, 2 * ml_qk), (o_mlv, ml_w), (o_mlo, ml_w), (o_sz, ssm_w), (o_sx, ssm_w),
                     (o_fq, fox_w), (o_fk, fox_w), (o_fv, fox_w), (o_gml, d), (o_gssm, d), (o_gfox, d),
                     (o_sb, 2 * ssm_bc)]
    seg_off = [0]
    for _, wd in main_segments:
        seg_off.append(seg_off[-1] + wd)
    (c_mlqk, c_mlv, c_mlo, c_sz, c_sx, c_fq, c_fk, c_fv, c_gml, c_gssm, c_gfox, c_bc) = seg_off[:-1]
    assert (c_mlqk, c_mlv, c_mlo) == (0, ml_w, 2 * ml_w)

    x2 = x.reshape(t, d)
    for i in range(depth):
        w = w_in[i]
        w_main = jnp.concatenate([w[:, o:o + wd] for o, wd in main_segments], axis=1).astype(BF16)
        w_small = _small_gate_block(w[:, o_mli:o_mli + ML_HEADS], w[:, o_mlf:o_mlf + ML_HEADS],
                                    w[:, o_sdt:o_sdt + SSM_HEADS], w[:, o_ff:o_ff + FOX_HEADS]).astype(BF16)
        gate_bias = _small_gate_block(ml_b_i[i], ml_b_f[i], ssm_dt_bias[i],
                                      fox_b_f[i]).astype(F32).reshape(1, LANES)

        zm, zs = _inproj(x2, _row(norm_mix_g[i]), w_main, w_small)

        y_ml = _mlstm(zm, zs, _pad_rows(ml_conv_w[i].astype(F32), 8), _row(ml_conv_b[i]), gate_bias,
                      _row(ml_norm_g[i]), bsz=bsz, seq=seq)

        scw = ssm_conv_w[i].astype(F32)
        scb = ssm_conv_b[i].astype(F32)
        no_heads = jnp.zeros((ML_HEADS,), ssm_a_log.dtype)
        alog_row = _small_gate_block(no_heads, no_heads, ssm_a_log[i],
                                     no_heads).astype(F32).reshape(1, LANES)
        d_row = jnp.repeat(ssm_d[i].astype(F32), SSM_HEADDIM).reshape(1, ssm_w)
        y_ssm = _ssd(zm, zs, _pad_rows(scw[:, :ssm_w], 8), _row(scb[:ssm_w]),
                     _pad_rows(scw[:, ssm_w:], 8), _row(scb[ssm_w:]), gate_bias, alog_row, d_row,
                     _row(ssm_norm_g[i]), bsz=bsz, seq=seq, col_z=c_sz, col_x=c_sx, col_bc=c_bc)

        qn, kn, ft = _fox_prep(zm, zs, gate_bias, _row(fox_q_norm_g[i]), _row(fox_k_norm_g[i]),
                               bsz=bsz, seq=seq, col_q=c_fq, col_k=c_fk)
        y_fox = _fox_attn(qn, kn, zm, ft, bsz=bsz, seq=seq, col_v=c_fv)

        x2 = _merge(x2, y_ml, y_ssm, y_fox, zm, w_branch_ml[i].astype(BF16), w_branch_ssm[i].astype(BF16),
                    w_branch_fox[i].astype(BF16), w_out[i].astype(BF16), col_g=c_gml)

        x2 = _peer(x2, _row(norm_ffn_g[i]), peer_w_q[i].astype(BF16), peer_keys1[i].astype(BF16),
                   peer_keys2[i].astype(BF16), peer_u[i].astype(BF16), peer_v[i])

        x2 = _ple(x2, p[i].reshape(t, -1), _row(norm_ple_g[i]), ple_w_gate[i].astype(BF16),
                  ple_w_proj[i].astype(BF16), _row(final_norm_g), final=(i == depth - 1))
    return x2.reshape(bsz, seq, d)
```
